```python
import math
import jax, jax.numpy as jnp
from jax import lax
import numpy as np


D_MODEL = 1024
BATCH = 8
SEQ = 2048
DEPTH = 4
DEC_BATCH = 128
DEC_SEQ = 4
PAST_LEN = 8192
PAGE_SIZE = 128

HEAD_DIM = 64
MIX_WIDTH = D_MODEL
H_A = MIX_WIDTH // 256
H_B = MIX_WIDTH // 128
KV_HEADS = 2
GQA_GROUP = H_B // KV_HEADS
H_C = MIX_WIDTH // 256
W_A = H_A * HEAD_DIM
W_B = H_B * HEAD_DIM
W_KV = KV_HEADS * HEAD_DIM
W_C = H_C * HEAD_DIM
CHUNK = 128
WINDOW = 128
CONV_W = 3
D_FF = ((8 * D_MODEL) // 3 + 255) // 256 * 256
EPS = 1e-6
NEG = -1e30
SCALE = HEAD_DIM ** -0.5
PROJ_SIZES = [W_A, W_A, W_B, W_KV, W_KV, W_C, W_C, W_C]
IN_WIDTH = sum(PROJ_SIZES)
PROJ_SPLITS = np.cumsum(PROJ_SIZES)[:-1].tolist()

kernel_name = "hymba_style_gmlp_swa_conv_macaron_step"


def rms_norm(x, g):
    xf = x.astype(jnp.float32)
    y = xf * lax.rsqrt(jnp.mean(xf * xf, axis=-1, keepdims=True) + EPS)
    return (y * g.astype(jnp.float32)).astype(x.dtype)


def head_layer_norm(v, g):
    vf = v.astype(jnp.float32)
    mu = jnp.mean(vf, axis=-1, keepdims=True)
    var = jnp.mean(jnp.square(vf - mu), axis=-1, keepdims=True)
    return ((vf - mu) * lax.rsqrt(var + EPS) * g.astype(jnp.float32)).astype(v.dtype)


def swiglu(x, w_gu, w_down):
    gate, up = jnp.split(x @ w_gu, 2, axis=-1)
    return (jax.nn.silu(gate) * up) @ w_down


def half_ffn(x, g_pre, g_post, w_gu, w_down):
    return x + 0.5 * rms_norm(swiglu(rms_norm(x, g_pre), w_gu, w_down), g_post)


def alibi_slopes():
    return jnp.asarray(2.0 ** (-8.0 * np.arange(1, H_B + 1) / H_B), dtype=jnp.float32)


def sink_softmax(scores, sink):
    sink = sink.astype(jnp.float32)
    m = jnp.maximum(jnp.max(scores, axis=-1, keepdims=True), sink)
    p = jnp.exp(scores - m)
    return p / (jnp.sum(p, axis=-1, keepdims=True) + jnp.exp(sink - m))


def chunk_mlp(ua, va, w_sgu, b_sgu, g_sgu):
    B, S, _ = ua.shape
    L = min(S, CHUNK)
    n = S // L
    u = jax.nn.gelu(ua, approximate=False).reshape(B, n, L, H_A, HEAD_DIM)
    v = head_layer_norm(jax.nn.gelu(va, approximate=False).reshape(B, S, H_A, HEAD_DIM),
                        g_sgu.reshape(H_A, HEAD_DIM))
    w = jnp.where(jnp.tril(jnp.ones((L, L), dtype=bool)), w_sgu[:, :L, :L], 0)
    mix = jnp.einsum('hts,bnshd->bnthd', w, v.reshape(B, n, L, H_A, HEAD_DIM))
    mix = mix + b_sgu[:, :L].T[None, None, :, :, None]
    return (u * mix).reshape(B, S, W_A), v


def swa_prompt(q, k, v, sinks, slopes):
    B, S = q.shape[:2]
    nb = S // WINDOW
    qb = q.reshape(B, nb, WINDOW, KV_HEADS, GQA_GROUP, HEAD_DIM)

    def band(t):
        tb = t.reshape(B, nb, WINDOW, KV_HEADS, HEAD_DIM)
        prev = jnp.pad(tb[:, :-1], ((0, 0), (1, 0), (0, 0), (0, 0), (0, 0)))
        return jnp.concatenate([prev, tb], axis=2)

    kb, vb = band(k), band(v)
    kpos = jnp.arange(2 * WINDOW)
    dist = (WINDOW + jnp.arange(WINDOW))[:, None] - kpos[None, :]
    valid = ((dist >= 0) & (dist < WINDOW))[None] & \
        ((jnp.arange(nb) > 0)[:, None, None] | (kpos >= WINDOW)[None, None, :])
    scores = jnp.einsum('bnqkgd,bnskd->bnkgqs', qb, kb,
                        preferred_element_type=jnp.float32) * SCALE
    scores = scores - slopes.reshape(KV_HEADS, GQA_GROUP)[:, :, None, None] * dist.astype(jnp.float32)
    scores = jnp.where(valid[None, :, None, None], scores, NEG)
    probs = sink_softmax(scores, sinks.reshape(KV_HEADS, GQA_GROUP)[:, :, None, None])
    out = jnp.einsum('bnkgqs,bnskd->bnqkgd', probs.astype(vb.dtype), vb)
    return out.reshape(B, S, W_B)


def swa_sample(q, k, v, k_buf, v_buf, sinks, slopes):
    Bd, T = q.shape[:2]
    k_all = jnp.concatenate([k_buf, k], axis=1)
    v_all = jnp.concatenate([v_buf, v], axis=1)
    dist = (WINDOW + jnp.arange(T))[:, None] - jnp.arange(WINDOW + T)[None, :]
    valid = (dist >= 0) & (dist < WINDOW)
    scores = jnp.einsum('btkgd,bskd->bkgts', q, k_all,
                        preferred_element_type=jnp.float32) * SCALE
    scores = scores - slopes.reshape(KV_HEADS, GQA_GROUP)[:, :, None, None] * dist.astype(jnp.float32)
    scores = jnp.where(valid[None, None, None], scores, NEG)
    probs = sink_softmax(scores, sinks.reshape(KV_HEADS, GQA_GROUP)[:, :, None, None])
    out = jnp.einsum('bkgts,bskd->btkgd', probs.astype(v_all.dtype), v_all)
    return out.reshape(Bd, T, W_B), k_all[:, T:], v_all[:, T:]


def short_conv(zp, w_conv, T):
    y = zp[:, 0:T] * w_conv[0]
    for j in range(1, CONV_W):
        y = y + zp[:, j:j + T] * w_conv[j]
    return y


def merge_groups(ya, yb, yc, g_out, w_out):
    ga, gb, gc = jnp.split(g_out, [W_A, W_A + W_B])
    y = jnp.concatenate([rms_norm(ya, ga), rms_norm(yb, gb), rms_norm(yc, gc)], axis=-1)
    return y @ w_out


def mixer_prompt(h, w_in, w_out, g_out, w_sgu, b_sgu, g_sgu, sinks, w_conv, slopes):
    B, S, _ = h.shape
    ua, va, q, k, v, gb, gc, hc = jnp.split(h @ w_in, PROJ_SPLITS, axis=-1)
    ya, _ = chunk_mlp(ua, va, w_sgu, b_sgu, g_sgu)
    k = k.reshape(B, S, KV_HEADS, HEAD_DIM)
    v = v.reshape(B, S, KV_HEADS, HEAD_DIM)
    yb = swa_prompt(q.reshape(B, S, KV_HEADS, GQA_GROUP, HEAD_DIM), k, v, sinks, slopes)
    z = gc * hc
    zp = jnp.pad(z, ((0, 0), (CONV_W - 1, 0), (0, 0)))
    yc = gb * short_conv(zp, w_conv, S)
    y = merge_groups(ya, yb, yc, g_out, w_out)
    return y, k[:, S - WINDOW:], v[:, S - WINDOW:], z[:, S - (CONV_W - 1):]


def mixer_sample(h, k_buf, v_buf, conv_buf, w_in, w_out, g_out, w_sgu, b_sgu, g_sgu, sinks, w_conv, slopes):
    Bd, T, _ = h.shape
    ua, va, q, k, v, gb, gc, hc = jnp.split(h @ w_in, PROJ_SPLITS, axis=-1)
    ya, v_sgu = chunk_mlp(ua, va, w_sgu, b_sgu, g_sgu)
    yb, k_new, v_new = swa_sample(q.reshape(Bd, T, KV_HEADS, GQA_GROUP, HEAD_DIM),
                                  k.reshape(Bd, T, KV_HEADS, HEAD_DIM),
                                  v.reshape(Bd, T, KV_HEADS, HEAD_DIM),
                                  k_buf, v_buf, sinks, slopes)
    z = gc * hc
    zp = jnp.concatenate([conv_buf, z], axis=1)
    yc = gb * short_conv(zp, w_conv, T)
    y = merge_groups(ya, yb, yc, g_out, w_out)
    return y, v_sgu, k_new, v_new, zp[:, T:]


def setup_inputs(seed: int = 0) -> dict:
    key = jax.random.key(seed)
    ks = jax.random.split(key, 16)
    f32 = jnp.float32

    def nrm(k, shape, s):
        return jax.random.normal(k, shape, f32) * s

    return {
        "x_prompt": nrm(ks[0], (BATCH, SEQ, D_MODEL), 1.0),
        "x_sample": nrm(ks[1], (DEC_BATCH, DEC_SEQ, D_MODEL), 1.0),
        "cache_swa_k": nrm(ks[2], (DEPTH, DEC_BATCH, WINDOW, KV_HEADS, HEAD_DIM), 1.0),
        "cache_swa_v": nrm(ks[3], (DEPTH, DEC_BATCH, WINDOW, KV_HEADS, HEAD_DIM), 1.0),
        "cache_conv": nrm(ks[4], (DEPTH, DEC_BATCH, CONV_W - 1, W_C), 1.0),
        "norm_g": 1.0 + nrm(ks[5], (DEPTH, 6, D_MODEL), 0.05),
        "w_ffn_gu": nrm(ks[6], (DEPTH, 2, D_MODEL, 2 * D_FF), D_MODEL ** -0.5),
        "w_ffn_down": nrm(ks[7], (DEPTH, 2, D_FF, D_MODEL), D_FF ** -0.5),
        "w_mix_in": nrm(ks[8], (DEPTH, D_MODEL, IN_WIDTH), D_MODEL ** -0.5),
        "w_mix_out": nrm(ks[9], (DEPTH, MIX_WIDTH, D_MODEL), MIX_WIDTH ** -0.5),
        "g_mix_out": 1.0 + nrm(ks[10], (DEPTH, MIX_WIDTH), 0.05),
        "w_sgu": nrm(ks[11], (DEPTH, H_A, CHUNK, CHUNK), CHUNK ** -0.5),
        "b_sgu": 1.0 + nrm(ks[12], (DEPTH, H_A, CHUNK), 0.1),
        "g_sgu": 1.0 + nrm(ks[13], (DEPTH, W_A), 0.05),
        "attn_sinks": nrm(ks[14], (DEPTH, H_B), 1.0),
        "w_conv": nrm(ks[15], (DEPTH, CONV_W, W_C), CONV_W ** -0.5),
    }


def reference(x_prompt, x_sample, cache_swa_k, cache_swa_v, cache_conv, norm_g, w_ffn_gu, w_ffn_down,
              w_mix_in, w_mix_out, g_mix_out, w_sgu, b_sgu, g_sgu, attn_sinks, w_conv):
    slopes = alibi_slopes()
    xp, xs = x_prompt, x_sample
    sgu_s, kp_l, vp_l, ks_l, vs_l, cp_l, cs_l = [], [], [], [], [], [], []
    for l in range(DEPTH):
        g = norm_g[l]
        xp = half_ffn(xp, g[0], g[1], w_ffn_gu[l, 0], w_ffn_down[l, 0])
        xs = half_ffn(xs, g[0], g[1], w_ffn_gu[l, 0], w_ffn_down[l, 0])
        mp, kp, vp, cp = mixer_prompt(rms_norm(xp, g[2]), w_mix_in[l], w_mix_out[l], g_mix_out[l],
                                      w_sgu[l], b_sgu[l], g_sgu[l], attn_sinks[l], w_conv[l], slopes)
        xp = xp + rms_norm(mp, g[3])
        ms, vsg, kn, vn, cn = mixer_sample(rms_norm(xs, g[2]), cache_swa_k[l], cache_swa_v[l], cache_conv[l],
                                           w_mix_in[l], w_mix_out[l], g_mix_out[l], w_sgu[l], b_sgu[l],
                                           g_sgu[l], attn_sinks[l], w_conv[l], slopes)
        xs = xs + rms_norm(ms, g[3])
        xp = half_ffn(xp, g[4], g[5], w_ffn_gu[l, 1], w_ffn_down[l, 1])
        xs = half_ffn(xs, g[4], g[5], w_ffn_gu[l, 1], w_ffn_down[l, 1])
        sgu_s.append(vsg)
        kp_l.append(kp)
        vp_l.append(vp)
        ks_l.append(kn)
        vs_l.append(vn)
        cp_l.append(cp)
        cs_l.append(cn)
    return (xp, xs, jnp.stack(sgu_s), jnp.stack(kp_l), jnp.stack(vp_l), jnp.stack(ks_l),
            jnp.stack(vs_l), jnp.stack(cp_l), jnp.stack(cs_l))
```

```python
import functools

import numpy as np
import jax
import jax.numpy as jnp
from jax import lax
from jax.experimental import pallas as pl
from jax.experimental.pallas import tpu as pltpu

F32 = jnp.float32
BF16 = jnp.bfloat16

HEAD_DIM = 64
KV_HEADS = 2
WINDOW = 128
CHUNK = 128
CONV_W = 3
EPS = 1e-6
NEG = -1e30
SCALE = HEAD_DIM ** -0.5
LANES = 128

FFN_TM = 512
FFN_TF = 256
MIX_T = 512
SAMPLE_G = 16
MIB = 1024 * 1024


def _rms(x, g):
    return x * lax.rsqrt(jnp.mean(x * x, axis=-1, keepdims=True) + EPS) * g


def _gelu(x):
    return 0.5 * x * (1.0 + lax.erf(x * np.float32(np.sqrt(0.5))))


def _head_layer_norm(x, g):
    lane = lax.broadcasted_iota(jnp.int32, (1, x.shape[1]), 1)
    out = jnp.zeros_like(x)
    for hh in range(x.shape[1] // HEAD_DIM):
        m = (lane >= hh * HEAD_DIM) & (lane < (hh + 1) * HEAD_DIM)
        mu = jnp.sum(jnp.where(m, x, 0.0), axis=-1, keepdims=True) / HEAD_DIM
        d = jnp.where(m, x - mu, 0.0)
        var = jnp.sum(d * d, axis=-1, keepdims=True) / HEAD_DIM
        out = out + d * lax.rsqrt(var + EPS)
    return out * g


def _dot(a, b):
    return jnp.dot(a, b, preferred_element_type=F32)


def _dot_nt(a, b):
    return lax.dot_general(a, b, (((1,), (1,)), ((), ())), preferred_element_type=F32)


def _sink_softmax(s, sink):
    m = jnp.maximum(jnp.max(s, axis=-1, keepdims=True), sink)
    p = jnp.exp(s - m)
    den = jnp.sum(p, axis=-1, keepdims=True) + jnp.exp(sink - m)
    return p * (1.0 / den)


def _alibi_slopes(n_heads):
    return [float(2.0 ** (-8.0 * h / n_heads)) for h in range(1, n_heads + 1)]


def _ffn_kernel(x_ref, gpre_ref, gpost_ref, wgu_ref, wdn_ref, o_ref, act_ref, *, d_ff, tf):
    x = x_ref[...]
    h = _rms(x, gpre_ref[...]).astype(BF16)
    for c in range(d_ff // tf):
        gate = _dot(h, wgu_ref[:, c * tf:(c + 1) * tf])
        up = _dot(h, wgu_ref[:, d_ff + c * tf:d_ff + (c + 1) * tf])
        act_ref[:, c * tf:(c + 1) * tf] = (jax.nn.silu(gate) * up).astype(BF16)
    y = _dot(act_ref[...], wdn_ref[...])
    o_ref[...] = x + 0.5 * _rms(y, gpost_ref[...])


def _ffn(x, g_pre, g_post, wgu, wdn):
    m, d = x.shape
    d_ff = wdn.shape[0]
    tm = FFN_TM
    assert m % tm == 0 and d_ff % FFN_TF == 0
    const = lambda i: (0, 0)
    vmem = (wgu.size + wdn.size) * 2 + 4 * tm * d * 4 + tm * d_ff * 2 + 12 * MIB
    return pl.pallas_call(
        functools.partial(_ffn_kernel, d_ff=d_ff, tf=FFN_TF),
        grid=(m // tm,),
        in_specs=[
            pl.BlockSpec((tm, d), lambda i: (i, 0)),
            pl.BlockSpec((1, d), const),
            pl.BlockSpec((1, d), const),
            pl.BlockSpec((d, 2 * d_ff), const, pipeline_mode=pl.Buffered(1)),
            pl.BlockSpec((d_ff, d), const, pipeline_mode=pl.Buffered(1)),
        ],
        out_specs=pl.BlockSpec((tm, d), lambda i: (i, 0)),
        out_shape=jax.ShapeDtypeStruct((m, d), F32),
        scratch_shapes=[pltpu.VMEM((tm, d_ff), BF16)],
        compiler_params=pltpu.CompilerParams(
            dimension_semantics=("arbitrary",), vmem_limit_bytes=vmem),
        name="ffn_half_step",
    )(x, g_pre, g_post, wgu, wdn)


def _mixp_kernel(sinks_ref, x_ref, g2_ref, g3_ref, win_ref, wout_ref, gout_ref, wsgu_ref,
                 bsgu_ref, gsgu_ref, wconv_ref,
                 xo_ref, ko_ref, vo_ref, co_ref,
                 proj_ref, kb_ref, krb_ref, vb_ref, vrb_ref, zbuf_ref, ycat_ref,
                 *, slopes, w_a, w_b, w_kv, w_c):
    j = pl.program_id(1)
    t_rows = x_ref.shape[0]
    nblk = t_rows // WINDOW
    o_q = 2 * w_a
    o_k = o_q + w_b
    o_v = o_k + w_kv
    o_gb = o_v + w_kv
    o_gc = o_gb + w_c
    o_hc = o_gc + w_c

    @pl.when(j == 0)
    def _():
        kb_ref[0:WINDOW, :] = jnp.zeros((WINDOW, w_kv), BF16)
        krb_ref[0:WINDOW, :] = jnp.zeros((WINDOW, w_kv), BF16)
        vb_ref[0:WINDOW, :] = jnp.zeros((WINDOW, w_kv), BF16)
        vrb_ref[0:WINDOW, :] = jnp.zeros((WINDOW, w_kv), BF16)
        zbuf_ref[0:8, :] = jnp.zeros((8, w_c), F32)

    x = x_ref[...]
    h = _rms(x, g2_ref[...]).astype(BF16)
    proj_ref[...] = _dot(h, win_ref[...])

    u = _gelu(proj_ref[:, 0:w_a])
    v = _head_layer_norm(_gelu(proj_ref[:, w_a:2 * w_a]), gsgu_ref[...])
    lane_a = lax.broadcasted_iota(jnp.int32, (1, w_a), 1)
    ri = lax.broadcasted_iota(jnp.int32, (CHUNK, CHUNK), 0)
    ci = lax.broadcasted_iota(jnp.int32, (CHUNK, CHUNK), 1)
    n_heads_a = w_a // HEAD_DIM
    w_tril = [jnp.where(ri >= ci, wsgu_ref[hh], 0.0).astype(BF16) for hh in range(n_heads_a)]
    bias = bsgu_ref[...]
    ya_parts = []
    for n in range(nblk):
        vblk = v[n * CHUNK:(n + 1) * CHUNK]
        mix = bias
        for hh in range(n_heads_a):
            mh = (lane_a >= hh * HEAD_DIM) & (lane_a < (hh + 1) * HEAD_DIM)
            mix = mix + _dot(w_tril[hh], jnp.where(mh, vblk, 0.0).astype(BF16))
        ya_parts.append(u[n * CHUNK:(n + 1) * CHUNK] * mix)
    ya = jnp.concatenate(ya_parts, axis=0)
    ycat_ref[:, 0:w_a] = _rms(ya, gout_ref[:, 0:w_a]).astype(BF16)

    lane = lax.broadcasted_iota(jnp.int32, (1, LANES), 1)
    lo = lane < HEAD_DIM
    k = proj_ref[:, o_k:o_k + w_kv]
    vv = proj_ref[:, o_v:o_v + w_kv]
    kb_ref[WINDOW:, :] = k.astype(BF16)
    krb_ref[WINDOW:, :] = pltpu.roll(k, HEAD_DIM, 1).astype(BF16)
    vb_ref[WINDOW:, :] = vv.astype(BF16)
    vrb_ref[WINDOW:, :] = pltpu.roll(vv, HEAD_DIM, 1).astype(BF16)

    qi = lax.broadcasted_iota(jnp.int32, (WINDOW, 2 * WINDOW), 0)
    kj = lax.broadcasted_iota(jnp.int32, (WINDOW, 2 * WINDOW), 1)
    dist_i = WINDOW + qi - kj
    band = (dist_i >= 0) & (dist_i < WINDOW)
    dist = dist_i.astype(F32)
    has_prev = jnp.full((WINDOW, 2 * WINDOW), j, jnp.int32) > 0
    band_first = band & ((kj >= WINDOW) | has_prev)

    n_q_cols = w_b // LANES
    half = n_q_cols // 2
    yb_parts = []
    for n in range(nblk):
        rows = slice(n * WINDOW, (n + 1) * WINDOW)
        valid = band_first if n == 0 else band
        qcols = [proj_ref[rows, o_q + c * LANES:o_q + (c + 1) * LANES] for c in range(n_q_cols)]
        keep_a = [lo if c < half else ~lo for c in range(n_q_cols)]
        lhs_a = jnp.concatenate([jnp.where(keep_a[c], qcols[c], 0.0) for c in range(n_q_cols)],
                                axis=0).astype(BF16)
        lhs_b = jnp.concatenate([jnp.where(keep_a[c], 0.0, qcols[c]) for c in range(n_q_cols)],
                                axis=0).astype(BF16)
        band_rows = slice(n * WINDOW, (n + 2) * WINDOW)
        s_a = _dot_nt(lhs_a, kb_ref[band_rows, :])
        s_b = _dot_nt(lhs_b, krb_ref[band_rows, :])
        p_a, p_b = [], []
        for c in range(n_q_cols):
            head_a = 2 * c if c < half else 2 * c + 1
            head_b = 2 * c + 1 if c < half else 2 * c
            for s_all, head, acc in ((s_a, head_a, p_a), (s_b, head_b, p_b)):
                s = s_all[c * WINDOW:(c + 1) * WINDOW] * SCALE - slopes[head] * dist
                s = jnp.where(valid, s, NEG)
                acc.append(_sink_softmax(s, sinks_ref[head]).astype(BF16))
        o_a = _dot(jnp.concatenate(p_a, axis=0), vb_ref[band_rows, :])
        o_b = _dot(jnp.concatenate(p_b, axis=0), vrb_ref[band_rows, :])
        cols = []
        for c in range(n_q_cols):
            ra = o_a[c * WINDOW:(c + 1) * WINDOW]
            rb = o_b[c * WINDOW:(c + 1) * WINDOW]
            cols.append(jnp.where(lo, ra, rb) if c < half else jnp.where(lo, rb, ra))
        yb_parts.append(jnp.concatenate(cols, axis=1))
    yb = jnp.concatenate(yb_parts, axis=0)
    ycat_ref[:, w_a:w_a + w_b] = _rms(yb, gout_ref[:, w_a:w_a + w_b]).astype(BF16)
    kb_ref[0:WINDOW, :] = kb_ref[t_rows:t_rows + WINDOW, :]
    krb_ref[0:WINDOW, :] = krb_ref[t_rows:t_rows + WINDOW, :]
    vb_ref[0:WINDOW, :] = vb_ref[t_rows:t_rows + WINDOW, :]
    vrb_ref[0:WINDOW, :] = vrb_ref[t_rows:t_rows + WINDOW, :]

    z = proj_ref[:, o_gc:o_gc + w_c] * proj_ref[:, o_hc:o_hc + w_c]
    zbuf_ref[8:8 + t_rows, :] = z
    conv = zbuf_ref[8 - (CONV_W - 1):8 - (CONV_W - 1) + t_rows, :] * wconv_ref[0:1, :]
    for jj in range(1, CONV_W):
        off = 8 - (CONV_W - 1) + jj
        conv = conv + zbuf_ref[off:off + t_rows, :] * wconv_ref[jj:jj + 1, :]
    yc = proj_ref[:, o_gb:o_gb + w_c] * conv
    ycat_ref[:, w_a + w_b:] = _rms(yc, gout_ref[:, w_a + w_b:]).astype(BF16)
    z_tail = zbuf_ref[8 + t_rows - (CONV_W - 1):8 + t_rows, :]
    zbuf_ref[8 - (CONV_W - 1):8, :] = z_tail

    y = _dot(ycat_ref[...], wout_ref[...])
    xo_ref[...] = x + _rms(y, g3_ref[...])

    @pl.when(j == pl.num_programs(1) - 1)
    def _():
        ko_ref[0] = proj_ref[t_rows - WINDOW:, o_k:o_k + w_kv]
        vo_ref[0] = proj_ref[t_rows - WINDOW:, o_v:o_v + w_kv]
        co_ref[0] = z_tail


def _mixer_prompt(x, batch, seq, g2, g3, win, wout, gout, wsgu, bsgu_full, gsgu, sinks, wconv,
                  w_a, w_b, w_kv, w_c):
    m, d = x.shape
    t = MIX_T
    assert seq % t == 0 and t % WINDOW == 0
    nt = seq // t
    in_w = win.shape[1]
    n_heads_b = w_b // HEAD_DIM
    const2 = lambda b, j, *_: (0, 0)
    const3 = lambda b, j, *_: (0, 0, 0)
    row = lambda b, j, *_: (b * nt + j, 0)
    per_b = lambda b, j, *_: (b, 0, 0)
    kernel = functools.partial(_mixp_kernel, slopes=_alibi_slopes(n_heads_b),
                               w_a=w_a, w_b=w_b, w_kv=w_kv, w_c=w_c)
    grid_spec = pltpu.PrefetchScalarGridSpec(
        num_scalar_prefetch=1,
        grid=(batch, nt),
        in_specs=[
            pl.BlockSpec((t, d), row),
            pl.BlockSpec((1, d), const2),
            pl.BlockSpec((1, d), const2),
            pl.BlockSpec((d, in_w), const2, pipeline_mode=pl.Buffered(1)),
            pl.BlockSpec(wout.shape, const2, pipeline_mode=pl.Buffered(1)),
            pl.BlockSpec((1, gout.shape[1]), const2),
            pl.BlockSpec(wsgu.shape, const3),
            pl.BlockSpec(bsgu_full.shape, const2),
            pl.BlockSpec((1, w_a), const2),
            pl.BlockSpec(wconv.shape, const2),
        ],
        out_specs=[
            pl.BlockSpec((t, d), row),
            pl.BlockSpec((1, WINDOW, w_kv), per_b),
            pl.BlockSpec((1, WINDOW, w_kv), per_b),
            pl.BlockSpec((1, CONV_W - 1, w_c), per_b),
        ],
        scratch_shapes=[
            pltpu.VMEM((t, in_w), F32),
            pltpu.VMEM((t + WINDOW, w_kv), BF16),
            pltpu.VMEM((t + WINDOW, w_kv), BF16),
            pltpu.VMEM((t + WINDOW, w_kv), BF16),
            pltpu.VMEM((t + WINDOW, w_kv), BF16),
            pltpu.VMEM((t + 8, w_c), F32),
            pltpu.VMEM((t, gout.shape[1]), BF16),
        ],
    )
    vmem = (win.size + wout.size) * 2 + 4 * t * d * 4 + t * in_w * 4 + 24 * MIB
    return pl.pallas_call(
        kernel,
        grid_spec=grid_spec,
        out_shape=[
            jax.ShapeDtypeStruct((m, d), F32),
            jax.ShapeDtypeStruct((batch, WINDOW, w_kv), F32),
            jax.ShapeDtypeStruct((batch, WINDOW, w_kv), F32),
            jax.ShapeDtypeStruct((batch, CONV_W - 1, w_c), F32),
        ],
        compiler_params=pltpu.CompilerParams(
            dimension_semantics=("arbitrary", "arbitrary"), vmem_limit_bytes=vmem),
        name="mixer_prompt",
    )(sinks, x, g2, g3, win, wout, gout, wsgu, bsgu_full, gsgu, wconv)


def _mixs_kernel(x_ref, g2_ref, g3_ref, win_ref, wout_ref, gout_ref, coef_ref, bias_ref, gsgu_ref,
                 wconv_ref, cexp1_ref, cexp2_ref, sinkcol_ref, kc_ref, vc_ref,
                 xo_ref, vsgu_ref, z_ref, ko_ref, vo_ref,
                 proj_ref, qprep_ref, ybuf_ref, ycat_ref,
                 *, slopes, t_new, w_a, w_b, w_kv, w_c):
    g = pl.program_id(0)
    rows_all = x_ref.shape[0]
    o_q = 2 * w_a
    o_k = o_q + w_b
    o_v = o_k + w_kv
    o_gb = o_v + w_kv
    o_gc = o_gb + w_c
    o_hc = o_gc + w_c
    n_heads = w_b // HEAD_DIM
    group = n_heads // KV_HEADS
    lane = lax.broadcasted_iota(jnp.int32, (1, LANES), 1)
    lo = lane < HEAD_DIM

    @pl.when(g == 0)
    def _():
        x = x_ref[...]
        h = _rms(x, g2_ref[...]).astype(BF16)
        proj_ref[...] = _dot(h, win_ref[...])
        tpos = lax.broadcasted_iota(jnp.int32, (rows_all, 1), 0) % t_new

        u = _gelu(proj_ref[:, 0:w_a])
        v = _head_layer_norm(_gelu(proj_ref[:, w_a:2 * w_a]), gsgu_ref[...])
        vsgu_ref[...] = v
        mix = bias_ref[...] + coef_ref[0] * v
        for dlt in range(1, t_new):
            mix = mix + coef_ref[dlt] * pltpu.roll(v, dlt, 0)
        ycat_ref[:, 0:w_a] = _rms(u * mix, gout_ref[:, 0:w_a]).astype(BF16)

        z = proj_ref[:, o_gc:o_gc + w_c] * proj_ref[:, o_hc:o_hc + w_c]
        z_ref[...] = z
        s2 = jnp.where(tpos >= 2, pltpu.roll(z, 2, 0), 0.0) + cexp2_ref[...]
        s1 = jnp.where(tpos >= 1, pltpu.roll(z, 1, 0), 0.0) + cexp1_ref[...]
        conv = s2 * wconv_ref[0:1, :] + s1 * wconv_ref[1:2, :] + z * wconv_ref[2:3, :]
        yc = proj_ref[:, o_gb:o_gb + w_c] * conv
        ycat_ref[:, w_a + w_b:] = _rms(yc, gout_ref[:, w_a + w_b:]).astype(BF16)

        for hd in range(n_heads):
            c, hf, kvh = hd // 2, hd % 2, hd // group
            piece = proj_ref[:, o_q + c * LANES:o_q + (c + 1) * LANES]
            if hf != kvh:
                piece = pltpu.roll(piece, HEAD_DIM, 1)
            qprep_ref[hd] = jnp.where(lo if kvh == 0 else ~lo, piece, 0.0)

    pair_rows = 2 * t_new
    n_pairs = kc_ref.shape[0] // (2 * WINDOW)
    lrows = n_heads * pair_rows
    ri = lax.broadcasted_iota(jnp.int32, (lrows, 1), 0)
    r_in = ri % pair_rows
    seq_q = r_in // t_new
    t_q = r_in % t_new
    cj = lax.broadcasted_iota(jnp.int32, (1, 2 * WINDOW), 1)
    valid_c = (cj // WINDOW == seq_q) & (cj % WINDOW >= t_q + 1)
    dist_c = (WINDOW + t_q - cj % WINDOW).astype(F32)
    nj = lax.broadcasted_iota(jnp.int32, (1, LANES), 1)
    valid_n = (nj < pair_rows) & (nj // t_new == seq_q) & (nj % t_new <= t_q)
    dist_n = (t_q - nj % t_new).astype(F32)
    slope_col = jnp.zeros((lrows, 1), F32)
    for hd in range(n_heads):
        slope_col = jnp.where(ri // pair_rows == hd, slopes[hd], slope_col)
    sink = sinkcol_ref[...]
    bias_c = slope_col * dist_c
    bias_n = slope_col * dist_n
    zpad = jnp.zeros((LANES - pair_rows, w_kv), F32)

    r0 = pl.multiple_of(g * (n_pairs * pair_rows), n_pairs * pair_rows)
    for p in range(n_pairs):
        rows = pl.ds(r0 + p * pair_rows, pair_rows)
        lhs = jnp.concatenate([qprep_ref[hd, rows, :] for hd in range(n_heads)], axis=0).astype(BF16)
        kn = proj_ref[rows, o_k:o_k + w_kv]
        vn = proj_ref[rows, o_v:o_v + w_kv]
        crow = slice(p * 2 * WINDOW, (p + 1) * 2 * WINDOW)
        kc = kc_ref[crow, :]
        vc = vc_ref[crow, :]
        s_c = _dot_nt(lhs, kc.astype(BF16)) * SCALE - bias_c
        s_n = _dot_nt(lhs, jnp.concatenate([kn, zpad], axis=0).astype(BF16)) * SCALE - bias_n
        s_c = jnp.where(valid_c, s_c, NEG)
        s_n = jnp.where(valid_n, s_n, NEG)
        m = jnp.maximum(jnp.maximum(jnp.max(s_c, axis=-1, keepdims=True),
                                    jnp.max(s_n, axis=-1, keepdims=True)), sink)
        p_c = jnp.exp(s_c - m)
        p_n = jnp.exp(s_n - m)
        den = (jnp.sum(p_c, axis=-1, keepdims=True) + jnp.sum(p_n, axis=-1, keepdims=True)
               + jnp.exp(sink - m))
        inv = 1.0 / den
        o = (_dot((p_c * inv).astype(BF16), vc.astype(BF16))
             + _dot((p_n * inv).astype(BF16), jnp.concatenate([vn, zpad], axis=0).astype(BF16)))
        cols = []
        for c in range(n_heads // 2):
            pieces = []
            for hf in range(2):
                hd = 2 * c + hf
                piece = o[hd * pair_rows:(hd + 1) * pair_rows]
                if hd // group != hf:
                    piece = pltpu.roll(piece, HEAD_DIM, 1)
                pieces.append(piece)
            cols.append(jnp.where(lo, pieces[0], pieces[1]))
        ybuf_ref[rows, :] = jnp.concatenate(cols, axis=1)
        for sl in range(2):
            b0 = (2 * p + sl) * WINDOW
            ko_ref[b0:b0 + WINDOW - t_new, :] = kc_ref[b0 + t_new:b0 + WINDOW, :]
            vo_ref[b0:b0 + WINDOW - t_new, :] = vc_ref[b0 + t_new:b0 + WINDOW, :]
            ko_ref[b0 + WINDOW - t_new:b0 + WINDOW, :] = kn[sl * t_new:(sl + 1) * t_new]
            vo_ref[b0 + WINDOW - t_new:b0 + WINDOW, :] = vn[sl * t_new:(sl + 1) * t_new]

    @pl.when(g == pl.num_programs(0) - 1)
    def _():
        ycat_ref[:, w_a:w_a + w_b] = _rms(ybuf_ref[...], gout_ref[:, w_a:w_a + w_b]).astype(BF16)
        y = _dot(ycat_ref[...], wout_ref[...])
        xo_ref[...] = x_ref[...] + _rms(y, g3_ref[...])


def _mixer_sample(x, n_seq, t_new, g2, g3, win, wout, gout, coef, bias, gsgu, wconv, cexp1, cexp2,
                  sink_col, kc, vc, w_a, w_b, w_kv, w_c):
    m, d = x.shape
    in_w = win.shape[1]
    n_heads = w_b // HEAD_DIM
    gsz = SAMPLE_G
    assert n_seq % gsz == 0 and gsz % 2 == 0 and 2 * t_new == 8
    ng = n_seq // gsz
    const2 = lambda g: (0, 0)
    const3 = lambda g: (0, 0, 0)
    step = lambda g: (g, 0)
    kernel = functools.partial(_mixs_kernel, slopes=_alibi_slopes(n_heads), t_new=t_new,
                               w_a=w_a, w_b=w_b, w_kv=w_kv, w_c=w_c)
    vmem = ((win.size + wout.size) * 2 + 4 * m * d * 4 + m * in_w * 4
            + 8 * gsz * WINDOW * w_kv * 4 + coef.size * 4 * 2 + 24 * MIB)
    return pl.pallas_call(
        kernel,
        grid=(ng,),
        in_specs=[
            pl.BlockSpec((m, d), const2),
            pl.BlockSpec((1, d), const2),
            pl.BlockSpec((1, d), const2),
            pl.BlockSpec((d, in_w), const2, pipeline_mode=pl.Buffered(1)),
            pl.BlockSpec(wout.shape, const2, pipeline_mode=pl.Buffered(1)),
            pl.BlockSpec((1, gout.shape[1]), const2),
            pl.BlockSpec(coef.shape, const3),
            pl.BlockSpec(bias.shape, const2),
            pl.BlockSpec((1, w_a), const2),
            pl.BlockSpec(wconv.shape, const2),
            pl.BlockSpec(cexp1.shape, const2),
            pl.BlockSpec(cexp2.shape, const2),
            pl.BlockSpec(sink_col.shape, const2),
            pl.BlockSpec((gsz * WINDOW, w_kv), step),
            pl.BlockSpec((gsz * WINDOW, w_kv), step),
        ],
        out_specs=[
            pl.BlockSpec((m, d), const2),
            pl.BlockSpec((m, w_a), const2),
            pl.BlockSpec((m, w_c), const2),
            pl.BlockSpec((gsz * WINDOW, w_kv), step),
            pl.BlockSpec((gsz * WINDOW, w_kv), step),
        ],
        out_shape=[
            jax.ShapeDtypeStruct((m, d), F32),
            jax.ShapeDtypeStruct((m, w_a), F32),
            jax.ShapeDtypeStruct((m, w_c), F32),
            jax.ShapeDtypeStruct(kc.shape, F32),
            jax.ShapeDtypeStruct(vc.shape, F32),
        ],
        scratch_shapes=[
            pltpu.VMEM((m, in_w), F32),
            pltpu.VMEM((n_heads, m, LANES), F32),
            pltpu.VMEM((m, w_b), F32),
            pltpu.VMEM((m, gout.shape[1]), BF16),
        ],
        compiler_params=pltpu.CompilerParams(
            dimension_semantics=("arbitrary",), vmem_limit_bytes=vmem),
        name="mixer_sample",
    )(x, g2, g3, win, wout, gout, coef, bias, gsgu, wconv, cexp1, cexp2, sink_col, kc, vc)


def kernel(x_prompt, x_sample, cache_swa_k, cache_swa_v, cache_conv, norm_g, w_ffn_gu, w_ffn_down,
           w_mix_in, w_mix_out, g_mix_out, w_sgu, b_sgu, g_sgu, attn_sinks, w_conv):
    batch, seq, d = x_prompt.shape
    n_seq, t_new, _ = x_sample.shape
    depth = norm_g.shape[0]
    w_a = g_sgu.shape[1]
    w_c = w_conv.shape[2]
    w_kv = KV_HEADS * HEAD_DIM
    w_b = g_mix_out.shape[1] - w_a - w_c
    n_heads_a = w_a // HEAD_DIM
    n_heads_b = w_b // HEAD_DIM
    assert t_new <= CHUNK and seq % CHUNK == 0

    xp = x_prompt.reshape(batch * seq, d)
    xs = x_sample.reshape(n_seq * t_new, d)
    outs = {k: [] for k in ("sgu", "kp", "vp", "ks", "vs", "cp", "cs")}
    for l in range(depth):
        g = norm_g[l]
        grow = lambda i: g[i:i + 1]
        wgu = [w_ffn_gu[l, i].astype(BF16) for i in range(2)]
        wdn = [w_ffn_down[l, i].astype(BF16) for i in range(2)]
        win = w_mix_in[l].astype(BF16)
        wout = w_mix_out[l].astype(BF16)
        gout = g_mix_out[l][None, :]
        gsgu = g_sgu[l][None, :]

        xp = _ffn(xp, grow(0), grow(1), wgu[0], wdn[0])
        xs = _ffn(xs, grow(0), grow(1), wgu[0], wdn[0])

        bsgu_full = jnp.repeat(b_sgu[l].T, HEAD_DIM, axis=1)
        xp, kp, vp, cp = _mixer_prompt(
            xp, batch, seq, grow(2), grow(3), win, wout, gout, w_sgu[l], bsgu_full, gsgu,
            attn_sinks[l], w_conv[l], w_a, w_b, w_kv, w_c)

        w4 = jnp.tril(w_sgu[l, :, :t_new, :t_new])
        tt = np.arange(t_new)
        coef = jnp.stack([
            jnp.where((tt >= dlt)[None, :], w4[:, tt, np.maximum(tt - dlt, 0)], 0.0)
            for dlt in range(t_new)])
        coef = jnp.repeat(coef.transpose(0, 2, 1), HEAD_DIM, axis=2)
        coef = jnp.tile(coef, (1, n_seq, 1))
        bias_s = jnp.tile(jnp.repeat(b_sgu[l, :, :t_new].T, HEAD_DIM, axis=1), (n_seq, 1))
        cc = cache_conv[l]
        cexp2 = jnp.pad(cc, ((0, 0), (0, t_new - (CONV_W - 1)), (0, 0))).reshape(n_seq * t_new, w_c)
        cexp1 = jnp.pad(cc[:, 1:], ((0, 0), (0, t_new - 1), (0, 0))).reshape(n_seq * t_new, w_c)
        sink_col = jnp.repeat(attn_sinks[l], 2 * t_new)[:, None]
        xs, vsgu, z_s, ks, vs = _mixer_sample(
            xs, n_seq, t_new, grow(2), grow(3), win, wout, gout, coef, bias_s, gsgu, w_conv[l],
            cexp1, cexp2, sink_col,
            cache_swa_k[l].reshape(n_seq * WINDOW, w_kv), cache_swa_v[l].reshape(n_seq * WINDOW, w_kv),
            w_a, w_b, w_kv, w_c)

        xp = _ffn(xp, grow(4), grow(5), wgu[1], wdn[1])
        xs = _ffn(xs, grow(4), grow(5), wgu[1], wdn[1])

        outs["sgu"].append(vsgu.reshape(n_seq, t_new, n_heads_a, HEAD_DIM))
        outs["kp"].append(kp.reshape(batch, WINDOW, KV_HEADS, HEAD_DIM))
        outs["vp"].append(vp.reshape(batch, WINDOW, KV_HEADS, HEAD_DIM))
        outs["ks"].append(ks.reshape(n_seq, WINDOW, KV_HEADS, HEAD_DIM))
        outs["vs"].append(vs.reshape(n_seq, WINDOW, KV_HEADS, HEAD_DIM))
        outs["cp"].append(cp)
        outs["cs"].append(z_s.reshape(n_seq, t_new, w_c)[:, t_new - (CONV_W - 1):])
    return (xp.reshape(batch, seq, d), xs.reshape(n_seq, t_new, d),
            jnp.stack(outs["sgu"]), jnp.stack(outs["kp"]), jnp.stack(outs["vp"]),
            jnp.stack(outs["ks"]), jnp.stack(outs["vs"]), jnp.stack(outs["cp"]),
            jnp.stack(outs["cs"]))
```

```python
import functools

import numpy as np
import jax
import jax.numpy as jnp
from jax import lax
from jax.experimental import pallas as pl
from jax.experimental.pallas import tpu as pltpu

F32 = jnp.float32
BF16 = jnp.bfloat16

HEAD_DIM = 64
KV_HEADS = 2
WINDOW = 128
CHUNK = 128
CONV_W = 3
EPS = 1e-6
NEG = -1e30
SCALE = HEAD_DIM ** -0.5
LANES = 128

FFN_TM = 512
FFN_TF = 256
MIX_T = 512
MIB = 1024 * 1024


def _rms(x, g):
    return x * lax.rsqrt(jnp.mean(x * x, axis=-1, keepdims=True) + EPS) * g


def _gelu(x):
    return 0.5 * x * (1.0 + lax.erf(x * np.float32(np.sqrt(0.5))))


def _head_layer_norm(x, g):
    lane = lax.broadcasted_iota(jnp.int32, (1, x.shape[1]), 1)
    out = jnp.zeros_like(x)
    for hh in range(x.shape[1] // HEAD_DIM):
        m = (lane >= hh * HEAD_DIM) & (lane < (hh + 1) * HEAD_DIM)
        mu = jnp.sum(jnp.where(m, x, 0.0), axis=-1, keepdims=True) / HEAD_DIM
        d = jnp.where(m, x - mu, 0.0)
        var = jnp.sum(d * d, axis=-1, keepdims=True) / HEAD_DIM
        out = out + d * lax.rsqrt(var + EPS)
    return out * g


def _dot(a, b):
    return jnp.dot(a, b, preferred_element_type=F32)


def _dot_nt(a, b):
    return lax.dot_general(a, b, (((1,), (1,)), ((), ())), preferred_element_type=F32)


def _sink_softmax(s, sink):
    m = jnp.maximum(jnp.max(s, axis=-1, keepdims=True), sink)
    p = jnp.exp(s - m)
    den = jnp.sum(p, axis=-1, keepdims=True) + jnp.exp(sink - m)
    return p * (1.0 / den)


def _alibi_slopes(n_heads):
    return [float(2.0 ** (-8.0 * h / n_heads)) for h in range(1, n_heads + 1)]


def _ffn_kernel(x_ref, gpre_ref, gpost_ref, wgu_ref, wdn_ref, o_ref, act_ref, *, d_ff, tf):
    x = x_ref[...]
    h = _rms(x, gpre_ref[...]).astype(BF16)
    for c in range(d_ff // tf):
        gate = _dot(h, wgu_ref[:, c * tf:(c + 1) * tf])
        up = _dot(h, wgu_ref[:, d_ff + c * tf:d_ff + (c + 1) * tf])
        act_ref[:, c * tf:(c + 1) * tf] = (jax.nn.silu(gate) * up).astype(BF16)
    y = _dot(act_ref[...], wdn_ref[...])
    o_ref[...] = x + 0.5 * _rms(y, gpost_ref[...])


def _ffn(x, g_pre, g_post, wgu, wdn):
    m, d = x.shape
    d_ff = wdn.shape[0]
    tm = FFN_TM
    assert m % tm == 0 and d_ff % FFN_TF == 0
    const = lambda i: (0, 0)
    vmem = (wgu.size + wdn.size) * 2 + 4 * tm * d * 4 + tm * d_ff * 2 + 12 * MIB
    return pl.pallas_call(
        functools.partial(_ffn_kernel, d_ff=d_ff, tf=FFN_TF),
        grid=(m // tm,),
        in_specs=[
            pl.BlockSpec((tm, d), lambda i: (i, 0)),
            pl.BlockSpec((1, d), const),
            pl.BlockSpec((1, d), const),
            pl.BlockSpec((d, 2 * d_ff), const, pipeline_mode=pl.Buffered(1)),
            pl.BlockSpec((d_ff, d), const, pipeline_mode=pl.Buffered(1)),
        ],
        out_specs=pl.BlockSpec((tm, d), lambda i: (i, 0)),
        out_shape=jax.ShapeDtypeStruct((m, d), F32),
        scratch_shapes=[pltpu.VMEM((tm, d_ff), BF16)],
        compiler_params=pltpu.CompilerParams(
            dimension_semantics=("arbitrary",), vmem_limit_bytes=vmem),
        name="ffn_half_step",
    )(x, g_pre, g_post, wgu, wdn)


def _mixp_kernel(sinks_ref, x_ref, g2_ref, g3_ref, win_ref, wout_ref, gout_ref, wsgu_ref,
                 bsgu_ref, gsgu_ref, wconv_ref,
                 xo_ref, ko_ref, vo_ref, co_ref,
                 proj_ref, kb_ref, krb_ref, vb_ref, vrb_ref, zbuf_ref, ycat_ref,
                 *, slopes, w_a, w_b, w_kv, w_c):
    j = pl.program_id(1)
    t_rows = x_ref.shape[0]
    nblk = t_rows // WINDOW
    o_q = 2 * w_a
    o_k = o_q + w_b
    o_v = o_k + w_kv
    o_gb = o_v + w_kv
    o_gc = o_gb + w_c
    o_hc = o_gc + w_c

    @pl.when(j == 0)
    def _():
        kb_ref[0:WINDOW, :] = jnp.zeros((WINDOW, w_kv), BF16)
        krb_ref[0:WINDOW, :] = jnp.zeros((WINDOW, w_kv), BF16)
        vb_ref[0:WINDOW, :] = jnp.zeros((WINDOW, w_kv), BF16)
        vrb_ref[0:WINDOW, :] = jnp.zeros((WINDOW, w_kv), BF16)
        zbuf_ref[0:8, :] = jnp.zeros((8, w_c), F32)

    x = x_ref[...]
    h = _rms(x, g2_ref[...]).astype(BF16)
    proj_ref[...] = _dot(h, win_ref[...])

    u = _gelu(proj_ref[:, 0:w_a])
    v = _head_layer_norm(_gelu(proj_ref[:, w_a:2 * w_a]), gsgu_ref[...])
    lane_a = lax.broadcasted_iota(jnp.int32, (1, w_a), 1)
    ri = lax.broadcasted_iota(jnp.int32, (CHUNK, CHUNK), 0)
    ci = lax.broadcasted_iota(jnp.int32, (CHUNK, CHUNK), 1)
    n_heads_a = w_a // HEAD_DIM
    w_tril = [jnp.where(ri >= ci, wsgu_ref[hh], 0.0).astype(BF16) for hh in range(n_heads_a)]
    bias = bsgu_ref[...]
    ya_parts = []
    for n in range(nblk):
        vblk = v[n * CHUNK:(n + 1) * CHUNK]
        mix = bias
        for hh in range(n_heads_a):
            mh = (lane_a >= hh * HEAD_DIM) & (lane_a < (hh + 1) * HEAD_DIM)
            mix = mix + _dot(w_tril[hh], jnp.where(mh, vblk, 0.0).astype(BF16))
        ya_parts.append(u[n * CHUNK:(n + 1) * CHUNK] * mix)
    ya = jnp.concatenate(ya_parts, axis=0)
    ycat_ref[:, 0:w_a] = _rms(ya, gout_ref[:, 0:w_a]).astype(BF16)

    lane = lax.broadcasted_iota(jnp.int32, (1, LANES), 1)
    lo = lane < HEAD_DIM
    k = proj_ref[:, o_k:o_k + w_kv]
    vv = proj_ref[:, o_v:o_v + w_kv]
    kb_ref[WINDOW:, :] = k.astype(BF16)
    krb_ref[WINDOW:, :] = pltpu.roll(k, HEAD_DIM, 1).astype(BF16)
    vb_ref[WINDOW:, :] = vv.astype(BF16)
    vrb_ref[WINDOW:, :] = pltpu.roll(vv, HEAD_DIM, 1).astype(BF16)

    qi = lax.broadcasted_iota(jnp.int32, (WINDOW, 2 * WINDOW), 0)
    kj = lax.broadcasted_iota(jnp.int32, (WINDOW, 2 * WINDOW), 1)
    dist_i = WINDOW + qi - kj
    band = (dist_i >= 0) & (dist_i < WINDOW)
    dist = dist_i.astype(F32)
    has_prev = jnp.full((WINDOW, 2 * WINDOW), j, jnp.int32) > 0
    band_first = band & ((kj >= WINDOW) | has_prev)

    n_q_cols = w_b // LANES
    half = n_q_cols // 2
    yb_parts = []
    for n in range(nblk):
        rows = slice(n * WINDOW, (n + 1) * WINDOW)
        valid = band_first if n == 0 else band
        qcols = [proj_ref[rows, o_q + c * LANES:o_q + (c + 1) * LANES] for c in range(n_q_cols)]
        keep_a = [lo if c < half else ~lo for c in range(n_q_cols)]
        lhs_a = jnp.concatenate([jnp.where(keep_a[c], qcols[c], 0.0) for c in range(n_q_cols)],
                                axis=0).astype(BF16)
        lhs_b = jnp.concatenate([jnp.where(keep_a[c], 0.0, qcols[c]) for c in range(n_q_cols)],
                                axis=0).astype(BF16)
        band_rows = slice(n * WINDOW, (n + 2) * WINDOW)
        s_a = _dot_nt(lhs_a, kb_ref[band_rows, :])
        s_b = _dot_nt(lhs_b, krb_ref[band_rows, :])
        p_a, p_b = [], []
        for c in range(n_q_cols):
            head_a = 2 * c if c < half else 2 * c + 1
            head_b = 2 * c + 1 if c < half else 2 * c
            for s_all, head, acc in ((s_a, head_a, p_a), (s_b, head_b, p_b)):
                s = s_all[c * WINDOW:(c + 1) * WINDOW] * SCALE - slopes[head] * dist
                s = jnp.where(valid, s, NEG)
                acc.append(_sink_softmax(s, sinks_ref[head]).astype(BF16))
        o_a = _dot(jnp.concatenate(p_a, axis=0), vb_ref[band_rows, :])
        o_b = _dot(jnp.concatenate(p_b, axis=0), vrb_ref[band_rows, :])
        cols = []
        for c in range(n_q_cols):
            ra = o_a[c * WINDOW:(c + 1) * WINDOW]
            rb = o_b[c * WINDOW:(c + 1) * WINDOW]
            cols.append(jnp.where(lo, ra, rb) if c < half else jnp.where(lo, rb, ra))
        yb_parts.append(jnp.concatenate(cols, axis=1))
    yb = jnp.concatenate(yb_parts, axis=0)
    ycat_ref[:, w_a:w_a + w_b] = _rms(yb, gout_ref[:, w_a:w_a + w_b]).astype(BF16)
    kb_ref[0:WINDOW, :] = kb_ref[t_rows:t_rows + WINDOW, :]
    krb_ref[0:WINDOW, :] = krb_ref[t_rows:t_rows + WINDOW, :]
    vb_ref[0:WINDOW, :] = vb_ref[t_rows:t_rows + WINDOW, :]
    vrb_ref[0:WINDOW, :] = vrb_ref[t_rows:t_rows + WINDOW, :]

    z = proj_ref[:, o_gc:o_gc + w_c] * proj_ref[:, o_hc:o_hc + w_c]
    zbuf_ref[8:8 + t_rows, :] = z
    conv = zbuf_ref[8 - (CONV_W - 1):8 - (CONV_W - 1) + t_rows, :] * wconv_ref[0:1, :]
    for jj in range(1, CONV_W):
        off = 8 - (CONV_W - 1) + jj
        conv = conv + zbuf_ref[off:off + t_rows, :] * wconv_ref[jj:jj + 1, :]
    yc = proj_ref[:, o_gb:o_gb + w_c] * conv
    ycat_ref[:, w_a + w_b:] = _rms(yc, gout_ref[:, w_a + w_b:]).astype(BF16)
    z_tail = zbuf_ref[8 + t_rows - (CONV_W - 1):8 + t_rows, :]
    zbuf_ref[8 - (CONV_W - 1):8, :] = z_tail

    y = _dot(ycat_ref[...], wout_ref[...])
    xo_ref[...] = x + _rms(y, g3_ref[...])

    @pl.when(j == pl.num_programs(1) - 1)
    def _():
        ko_ref[0] = proj_ref[t_rows - WINDOW:, o_k:o_k + w_kv].T
        vo_ref[0] = proj_ref[t_rows - WINDOW:, o_v:o_v + w_kv].T
        co_ref[0] = z_tail


def _mixer_prompt(x, batch, seq, g2, g3, win, wout, gout, wsgu, bsgu_full, gsgu, sinks, wconv,
                  w_a, w_b, w_kv, w_c):
    m, d = x.shape
    t = MIX_T
    assert seq % t == 0 and t % WINDOW == 0
    nt = seq // t
    in_w = win.shape[1]
    n_heads_b = w_b // HEAD_DIM
    const2 = lambda b, j, *_: (0, 0)
    const3 = lambda b, j, *_: (0, 0, 0)
    row = lambda b, j, *_: (b * nt + j, 0)
    per_b = lambda b, j, *_: (b, 0, 0)
    kernel = functools.partial(_mixp_kernel, slopes=_alibi_slopes(n_heads_b),
                               w_a=w_a, w_b=w_b, w_kv=w_kv, w_c=w_c)
    grid_spec = pltpu.PrefetchScalarGridSpec(
        num_scalar_prefetch=1,
        grid=(batch, nt),
        in_specs=[
            pl.BlockSpec((t, d), row),
            pl.BlockSpec((1, d), const2),
            pl.BlockSpec((1, d), const2),
            pl.BlockSpec((d, in_w), const2, pipeline_mode=pl.Buffered(1)),
            pl.BlockSpec(wout.shape, const2, pipeline_mode=pl.Buffered(1)),
            pl.BlockSpec((1, gout.shape[1]), const2),
            pl.BlockSpec(wsgu.shape, const3),
            pl.BlockSpec(bsgu_full.shape, const2),
            pl.BlockSpec((1, w_a), const2),
            pl.BlockSpec(wconv.shape, const2),
        ],
        out_specs=[
            pl.BlockSpec((t, d), row),
            pl.BlockSpec((1, w_kv, WINDOW), per_b),
            pl.BlockSpec((1, w_kv, WINDOW), per_b),
            pl.BlockSpec((1, CONV_W - 1, w_c), per_b),
        ],
        scratch_shapes=[
            pltpu.VMEM((t, in_w), F32),
            pltpu.VMEM((t + WINDOW, w_kv), BF16),
            pltpu.VMEM((t + WINDOW, w_kv), BF16),
            pltpu.VMEM((t + WINDOW, w_kv), BF16),
            pltpu.VMEM((t + WINDOW, w_kv), BF16),
            pltpu.VMEM((t + 8, w_c), F32),
            pltpu.VMEM((t, gout.shape[1]), BF16),
        ],
    )
    vmem = (win.size + wout.size) * 2 + 4 * t * d * 4 + t * in_w * 4 + 24 * MIB
    return pl.pallas_call(
        kernel,
        grid_spec=grid_spec,
        out_shape=[
            jax.ShapeDtypeStruct((m, d), F32),
            jax.ShapeDtypeStruct((batch, w_kv, WINDOW), F32),
            jax.ShapeDtypeStruct((batch, w_kv, WINDOW), F32),
            jax.ShapeDtypeStruct((batch, CONV_W - 1, w_c), F32),
        ],
        compiler_params=pltpu.CompilerParams(
            dimension_semantics=("arbitrary", "arbitrary"), vmem_limit_bytes=vmem),
        name="mixer_prompt",
    )(sinks, x, g2, g3, win, wout, gout, wsgu, bsgu_full, gsgu, wconv)


def _mixs_kernel(x_ref, g2_ref, g3_ref, win_ref, wout_ref, gout_ref, coef_ref, bias_ref, gsgu_ref,
                 wconv_ref, cexp1_ref, cexp2_ref, sinkcol_ref, kc_ref, vc_ref,
                 xo_ref, vsgu_ref, z_ref, ko_ref, vo_ref,
                 proj_ref, qprep_ref, ybuf_ref, ycat_ref, knt_ref, vnt_ref, vtmp_ref,
                 *, slopes, t_new, w_a, w_b, w_kv, w_c):
    g = pl.program_id(0)
    rows_all = x_ref.shape[0]
    n_seq = rows_all // t_new
    gsz = kc_ref.shape[0]
    o_q = 2 * w_a
    o_k = o_q + w_b
    o_v = o_k + w_kv
    o_gb = o_v + w_kv
    o_gc = o_gb + w_c
    o_hc = o_gc + w_c
    n_heads = w_b // HEAD_DIM
    group = n_heads // KV_HEADS
    lane = lax.broadcasted_iota(jnp.int32, (1, LANES), 1)
    lo = lane < HEAD_DIM

    @pl.when(g == 0)
    def _():
        x = x_ref[...]
        h = _rms(x, g2_ref[...]).astype(BF16)
        proj_ref[...] = _dot(h, win_ref[...])
        tpos = lax.broadcasted_iota(jnp.int32, (rows_all, 1), 0) % t_new

        u = _gelu(proj_ref[:, 0:w_a])
        v = _head_layer_norm(_gelu(proj_ref[:, w_a:2 * w_a]), gsgu_ref[...])
        for c in range(w_a // LANES):
            vtmp_ref[c] = v[:, c * LANES:(c + 1) * LANES]
        for tt in range(t_new):
            for c in range(w_a // LANES):
                vsgu_ref[tt, c * LANES:(c + 1) * LANES, :] = (
                    vtmp_ref[c, pl.ds(tt, n_seq, stride=t_new), :].T)
        mix = bias_ref[...] + coef_ref[0] * v
        for dlt in range(1, t_new):
            mix = mix + coef_ref[dlt] * pltpu.roll(v, dlt, 0)
        ycat_ref[:, 0:w_a] = _rms(u * mix, gout_ref[:, 0:w_a]).astype(BF16)

        z = proj_ref[:, o_gc:o_gc + w_c] * proj_ref[:, o_hc:o_hc + w_c]
        z_ref[...] = z
        s2 = jnp.where(tpos >= 2, pltpu.roll(z, 2, 0), 0.0) + cexp2_ref[...]
        s1 = jnp.where(tpos >= 1, pltpu.roll(z, 1, 0), 0.0) + cexp1_ref[...]
        conv = s2 * wconv_ref[0:1, :] + s1 * wconv_ref[1:2, :] + z * wconv_ref[2:3, :]
        yc = proj_ref[:, o_gb:o_gb + w_c] * conv
        ycat_ref[:, w_a + w_b:] = _rms(yc, gout_ref[:, w_a + w_b:]).astype(BF16)

        for hd in range(n_heads):
            c, hf, kvh = hd // 2, hd % 2, hd // group
            piece = proj_ref[:, o_q + c * LANES:o_q + (c + 1) * LANES]
            if hf != kvh:
                piece = pltpu.roll(piece, HEAD_DIM, 1)
            qprep_ref[hd] = jnp.where(lo if kvh == 0 else ~lo, piece, 0.0)

        for c in range(rows_all // LANES):
            knt_ref[c] = proj_ref[c * LANES:(c + 1) * LANES, o_k:o_k + w_kv].T
            vnt_ref[c] = proj_ref[c * LANES:(c + 1) * LANES, o_v:o_v + w_kv].T

    pair_rows = 2 * t_new
    n_pairs = gsz // 2
    lrows = n_heads * pair_rows
    ri = lax.broadcasted_iota(jnp.int32, (lrows, 1), 0)
    r_in = ri % pair_rows
    seq_q = r_in // t_new
    t_q = r_in % t_new
    cj = lax.broadcasted_iota(jnp.int32, (1, 2 * WINDOW), 1)
    valid_c = (cj // WINDOW == seq_q) & (cj % WINDOW >= t_q + 1)
    dist_c = (WINDOW + t_q - cj % WINDOW).astype(F32)
    nj = lax.broadcasted_iota(jnp.int32, (1, LANES), 1)
    valid_n = (nj < pair_rows) & (nj // t_new == seq_q) & (nj % t_new <= t_q)
    dist_n = (t_q - nj % t_new).astype(F32)
    slope_col = jnp.zeros((lrows, 1), F32)
    for hd in range(n_heads):
        slope_col = jnp.where(ri // pair_rows == hd, slopes[hd], slope_col)
    sink = sinkcol_ref[...]
    bias_c = slope_col * dist_c
    bias_n = slope_col * dist_n
    zpad = jnp.zeros((LANES - pair_rows, w_kv), F32)

    r0 = pl.multiple_of(g * (n_pairs * pair_rows), n_pairs * pair_rows)
    for p in range(n_pairs):
        rows = pl.ds(r0 + p * pair_rows, pair_rows)
        lhs = jnp.concatenate([qprep_ref[hd, rows, :] for hd in range(n_heads)], axis=0).astype(BF16)
        kn = proj_ref[rows, o_k:o_k + w_kv]
        vn = proj_ref[rows, o_v:o_v + w_kv]
        kc = jnp.concatenate([kc_ref[2 * p], kc_ref[2 * p + 1]], axis=1)
        vc = jnp.concatenate([vc_ref[2 * p], vc_ref[2 * p + 1]], axis=1)
        s_c = _dot(lhs, kc.astype(BF16)) * SCALE - bias_c
        s_n = _dot_nt(lhs, jnp.concatenate([kn, zpad], axis=0).astype(BF16)) * SCALE - bias_n
        s_c = jnp.where(valid_c, s_c, NEG)
        s_n = jnp.where(valid_n, s_n, NEG)
        m = jnp.maximum(jnp.maximum(jnp.max(s_c, axis=-1, keepdims=True),
                                    jnp.max(s_n, axis=-1, keepdims=True)), sink)
        p_c = jnp.exp(s_c - m)
        p_n = jnp.exp(s_n - m)
        den = (jnp.sum(p_c, axis=-1, keepdims=True) + jnp.sum(p_n, axis=-1, keepdims=True)
               + jnp.exp(sink - m))
        inv = 1.0 / den
        o = (_dot_nt((p_c * inv).astype(BF16), vc.astype(BF16))
             + _dot((p_n * inv).astype(BF16), jnp.concatenate([vn, zpad], axis=0).astype(BF16)))
        cols = []
        for c in range(n_heads // 2):
            pieces = []
            for hf in range(2):
                hd = 2 * c + hf
                piece = o[hd * pair_rows:(hd + 1) * pair_rows]
                if hd // group != hf:
                    piece = pltpu.roll(piece, HEAD_DIM, 1)
                pieces.append(piece)
            cols.append(jnp.where(lo, pieces[0], pieces[1]))
        ybuf_ref[rows, :] = jnp.concatenate(cols, axis=1)

    keep = lane < WINDOW - t_new
    kn_t = knt_ref[g]
    vn_t = vnt_ref[g]
    for i in range(gsz):
        shift = (WINDOW - t_new - t_new * i) % LANES
        nk = pltpu.roll(kn_t, shift, 1) if shift else kn_t
        nv = pltpu.roll(vn_t, shift, 1) if shift else vn_t
        ko_ref[i] = jnp.where(keep, pltpu.roll(kc_ref[i], WINDOW - t_new, 1), nk)
        vo_ref[i] = jnp.where(keep, pltpu.roll(vc_ref[i], WINDOW - t_new, 1), nv)

    @pl.when(g == pl.num_programs(0) - 1)
    def _():
        ycat_ref[:, w_a:w_a + w_b] = _rms(ybuf_ref[...], gout_ref[:, w_a:w_a + w_b]).astype(BF16)
        y = _dot(ycat_ref[...], wout_ref[...])
        xo_ref[...] = x_ref[...] + _rms(y, g3_ref[...])


def _mixer_sample(x, layer, n_seq, t_new, g2, g3, win, wout, gout, coef, bias, gsgu, wconv, cexp1,
                  cexp2, sink_col, kc_all, vc_all, w_a, w_b, w_kv, w_c):
    m, d = x.shape
    in_w = win.shape[1]
    n_heads = w_b // HEAD_DIM
    gsz = LANES // t_new
    assert n_seq % gsz == 0 and gsz % 2 == 0 and 2 * t_new == 8 and WINDOW == LANES
    ng = n_seq // gsz
    const2 = lambda g: (0, 0)
    const3 = lambda g: (0, 0, 0)
    step_in = lambda g: (layer, g, 0, 0)
    step_out = lambda g: (g, 0, 0)
    kernel = functools.partial(_mixs_kernel, slopes=_alibi_slopes(n_heads), t_new=t_new,
                               w_a=w_a, w_b=w_b, w_kv=w_kv, w_c=w_c)
    vmem = ((win.size + wout.size) * 2 + 4 * m * d * 4 + m * in_w * 4
            + 8 * gsz * WINDOW * w_kv * 4 + coef.size * 4 * 2 + 24 * MIB)
    return pl.pallas_call(
        kernel,
        grid=(ng,),
        in_specs=[
            pl.BlockSpec((m, d), const2),
            pl.BlockSpec((1, d), const2),
            pl.BlockSpec((1, d), const2),
            pl.BlockSpec((d, in_w), const2, pipeline_mode=pl.Buffered(1)),
            pl.BlockSpec(wout.shape, const2, pipeline_mode=pl.Buffered(1)),
            pl.BlockSpec((1, gout.shape[1]), const2),
            pl.BlockSpec(coef.shape, const3),
            pl.BlockSpec(bias.shape, const2),
            pl.BlockSpec((1, w_a), const2),
            pl.BlockSpec(wconv.shape, const2),
            pl.BlockSpec(cexp1.shape, const2),
            pl.BlockSpec(cexp2.shape, const2),
            pl.BlockSpec(sink_col.shape, const2),
            pl.BlockSpec((None, gsz, w_kv, WINDOW), step_in),
            pl.BlockSpec((None, gsz, w_kv, WINDOW), step_in),
        ],
        out_specs=[
            pl.BlockSpec((m, d), const2),
            pl.BlockSpec((t_new, w_a, n_seq), const3),
            pl.BlockSpec((m, w_c), const2),
            pl.BlockSpec((gsz, w_kv, WINDOW), step_out),
            pl.BlockSpec((gsz, w_kv, WINDOW), step_out),
        ],
        out_shape=[
            jax.ShapeDtypeStruct((m, d), F32),
            jax.ShapeDtypeStruct((t_new, w_a, n_seq), F32),
            jax.ShapeDtypeStruct((m, w_c), F32),
            jax.ShapeDtypeStruct((n_seq, w_kv, WINDOW), F32),
            jax.ShapeDtypeStruct((n_seq, w_kv, WINDOW), F32),
        ],
        scratch_shapes=[
            pltpu.VMEM((m, in_w), F32),
            pltpu.VMEM((n_heads, m, LANES), F32),
            pltpu.VMEM((m, w_b), F32),
            pltpu.VMEM((m, gout.shape[1]), BF16),
            pltpu.VMEM((ng, w_kv, LANES), F32),
            pltpu.VMEM((ng, w_kv, LANES), F32),
            pltpu.VMEM((w_a // LANES, m, LANES), F32),
        ],
        compiler_params=pltpu.CompilerParams(
            dimension_semantics=("arbitrary",), vmem_limit_bytes=vmem),
        name="mixer_sample",
    )(x, g2, g3, win, wout, gout, coef, bias, gsgu, wconv, cexp1, cexp2, sink_col, kc_all, vc_all)


def kernel(x_prompt, x_sample, cache_swa_k, cache_swa_v, cache_conv, norm_g, w_ffn_gu, w_ffn_down,
           w_mix_in, w_mix_out, g_mix_out, w_sgu, b_sgu, g_sgu, attn_sinks, w_conv):
    batch, seq, d = x_prompt.shape
    n_seq, t_new, _ = x_sample.shape
    depth = norm_g.shape[0]
    w_a = g_sgu.shape[1]
    w_c = w_conv.shape[2]
    w_kv = KV_HEADS * HEAD_DIM
    w_b = g_mix_out.shape[1] - w_a - w_c
    n_heads_a = w_a // HEAD_DIM
    n_heads_b = w_b // HEAD_DIM
    assert t_new <= CHUNK and seq % CHUNK == 0

    xp = x_prompt.reshape(batch * seq, d)
    xs = x_sample.reshape(n_seq * t_new, d)
    to_dp = lambda c: jnp.transpose(c, (0, 1, 3, 4, 2)).reshape(depth, c.shape[1], w_kv, c.shape[2])
    from_dp = lambda c: jnp.transpose(
        c.reshape(depth, c.shape[1], KV_HEADS, HEAD_DIM, c.shape[3]), (0, 1, 4, 2, 3))
    kc_all = to_dp(cache_swa_k)
    vc_all = to_dp(cache_swa_v)
    outs = {k: [] for k in ("sgu", "kp", "vp", "ks", "vs", "cp", "cs")}
    for l in range(depth):
        g = norm_g[l]
        grow = lambda i: g[i:i + 1]
        wgu = [w_ffn_gu[l, i].astype(BF16) for i in range(2)]
        wdn = [w_ffn_down[l, i].astype(BF16) for i in range(2)]
        win = w_mix_in[l].astype(BF16)
        wout = w_mix_out[l].astype(BF16)
        gout = g_mix_out[l][None, :]
        gsgu = g_sgu[l][None, :]

        xp = _ffn(xp, grow(0), grow(1), wgu[0], wdn[0])
        xs = _ffn(xs, grow(0), grow(1), wgu[0], wdn[0])

        bsgu_full = jnp.repeat(b_sgu[l].T, HEAD_DIM, axis=1)
        xp, kp, vp, cp = _mixer_prompt(
            xp, batch, seq, grow(2), grow(3), win, wout, gout, w_sgu[l], bsgu_full, gsgu,
            attn_sinks[l], w_conv[l], w_a, w_b, w_kv, w_c)

        w4 = jnp.tril(w_sgu[l, :, :t_new, :t_new])
        tt = np.arange(t_new)
        coef = jnp.stack([
            jnp.where((tt >= dlt)[None, :], w4[:, tt, np.maximum(tt - dlt, 0)], 0.0)
            for dlt in range(t_new)])
        coef = jnp.repeat(coef.transpose(0, 2, 1), HEAD_DIM, axis=2)
        coef = jnp.tile(coef, (1, n_seq, 1))
        bias_s = jnp.tile(jnp.repeat(b_sgu[l, :, :t_new].T, HEAD_DIM, axis=1), (n_seq, 1))
        cc = cache_conv[l]
        cexp2 = jnp.pad(cc, ((0, 0), (0, t_new - (CONV_W - 1)), (0, 0))).reshape(n_seq * t_new, w_c)
        cexp1 = jnp.pad(cc[:, 1:], ((0, 0), (0, t_new - 1), (0, 0))).reshape(n_seq * t_new, w_c)
        sink_col = jnp.repeat(attn_sinks[l], 2 * t_new)[:, None]
        xs, vsgu, z_s, ks, vs = _mixer_sample(
            xs, l, n_seq, t_new, grow(2), grow(3), win, wout, gout, coef, bias_s, gsgu, w_conv[l],
            cexp1, cexp2, sink_col, kc_all, vc_all, w_a, w_b, w_kv, w_c)

        xp = _ffn(xp, grow(4), grow(5), wgu[1], wdn[1])
        xs = _ffn(xs, grow(4), grow(5), wgu[1], wdn[1])

        outs["sgu"].append(vsgu)
        outs["kp"].append(kp)
        outs["vp"].append(vp)
        outs["ks"].append(ks)
        outs["vs"].append(vs)
        outs["cp"].append(cp)
        outs["cs"].append(z_s.reshape(n_seq, t_new, w_c)[:, t_new - (CONV_W - 1):])
    sgu = jnp.transpose(
        jnp.stack(outs["sgu"]).reshape(depth, t_new, n_heads_a, HEAD_DIM, n_seq), (0, 4, 1, 2, 3))
    return (xp.reshape(batch, seq, d), xs.reshape(n_seq, t_new, d), sgu,
            from_dp(jnp.stack(outs["kp"])), from_dp(jnp.stack(outs["vp"])),
            from_dp(jnp.stack(outs["ks"])), from_dp(jnp.stack(outs["vs"])),
            jnp.stack(outs["cp"]), jnp.stack(outs["cs"]))
```

```python
import functools

import numpy as np
import jax
import jax.numpy as jnp
from jax import lax
from jax.experimental import pallas as pl
from jax.experimental.pallas import tpu as pltpu

F32 = jnp.float32
BF16 = jnp.bfloat16

HEAD_DIM = 64
KV_HEADS = 2
WINDOW = 128
CHUNK = 128
CONV_W = 3
EPS = 1e-6
NEG = -1e30
SCALE = HEAD_DIM ** -0.5
LANES = 128

FFN_TM = 512
FFN_TF = 256
MIX_T = 512
MIB = 1024 * 1024


def _rms(x, g):
    return x * lax.rsqrt(jnp.mean(x * x, axis=-1, keepdims=True) + EPS) * g


def _gelu(x):
    return 0.5 * x * (1.0 + lax.erf(x * np.float32(np.sqrt(0.5))))


def _head_layer_norm(x, g):
    lane = lax.broadcasted_iota(jnp.int32, (1, x.shape[1]), 1)
    out = jnp.zeros_like(x)
    for hh in range(x.shape[1] // HEAD_DIM):
        m = (lane >= hh * HEAD_DIM) & (lane < (hh + 1) * HEAD_DIM)
        mu = jnp.sum(jnp.where(m, x, 0.0), axis=-1, keepdims=True) / HEAD_DIM
        d = jnp.where(m, x - mu, 0.0)
        var = jnp.sum(d * d, axis=-1, keepdims=True) / HEAD_DIM
        out = out + d * lax.rsqrt(var + EPS)
    return out * g


def _dot(a, b):
    return jnp.dot(a, b, preferred_element_type=F32)


def _dot_nt(a, b):
    return lax.dot_general(a, b, (((1,), (1,)), ((), ())), preferred_element_type=F32)


def _sink_softmax(s, sink):
    m = jnp.maximum(jnp.max(s, axis=-1, keepdims=True), sink)
    p = jnp.exp(s - m)
    den = jnp.sum(p, axis=-1, keepdims=True) + jnp.exp(sink - m)
    return p * (1.0 / den)


def _alibi_slopes(n_heads):
    return [float(2.0 ** (-8.0 * h / n_heads)) for h in range(1, n_heads + 1)]


def _ffn_kernel(*refs, layer, half, n_prompt_tiles, split_in, split_out, d_ff, tf, n_stream):
    refs = list(refs)
    xp_ref = refs.pop(0)
    xs_ref = refs.pop(0) if split_in else None
    norm_ref, wgu_hbm, wdn_hbm = refs[:3]
    refs = refs[3:]
    op_ref = refs.pop(0)
    os_ref = refs.pop(0) if split_out else None
    wgu16_ref, wdn16_ref, stage_gu_ref, stage_dn_ref, sem_ref, act_ref = refs
    i = pl.program_id(0)
    cw = stage_gu_ref.shape[2]
    rw = stage_dn_ref.shape[1]

    def gu_copy(c, slot):
        return pltpu.make_async_copy(wgu_hbm.at[layer, half, :, pl.ds(c * cw, cw)],
                                     stage_gu_ref.at[slot], sem_ref.at[0, slot])

    def dn_copy(c, slot):
        return pltpu.make_async_copy(wdn_hbm.at[layer, half, pl.ds(c * rw, rw), :],
                                     stage_dn_ref.at[slot], sem_ref.at[1, slot])

    @pl.when(i == 0)
    def _():
        gu_copy(0, 0).start()
        dn_copy(0, 0).start()
        for c in range(n_stream):
            slot = c % 2
            if c + 1 < n_stream:
                gu_copy(c + 1, 1 - slot).start()
                dn_copy(c + 1, 1 - slot).start()
            gu_copy(c, slot).wait()
            dn_copy(c, slot).wait()
            wgu16_ref[:, c * cw:(c + 1) * cw] = stage_gu_ref[slot].astype(BF16)
            wdn16_ref[c * rw:(c + 1) * rw, :] = stage_dn_ref[slot].astype(BF16)

    if split_in:
        x = jnp.where(i < n_prompt_tiles, xp_ref[...], xs_ref[...])
    else:
        x = xp_ref[...]
    g_pre = norm_ref[layer, 4 * half:4 * half + 1, :]
    g_post = norm_ref[layer, 4 * half + 1:4 * half + 2, :]
    h = _rms(x, g_pre).astype(BF16)
    for c in range(d_ff // tf):
        gate = _dot(h, wgu16_ref[:, c * tf:(c + 1) * tf])
        up = _dot(h, wgu16_ref[:, d_ff + c * tf:d_ff + (c + 1) * tf])
        act_ref[:, c * tf:(c + 1) * tf] = (jax.nn.silu(gate) * up).astype(BF16)
    y = _dot(act_ref[...], wdn16_ref[...])
    if split_out:
        @pl.when(i < n_prompt_tiles)
        def _():
            op_ref[...] = x + 0.5 * _rms(y, g_post)

        @pl.when(i == n_prompt_tiles)
        def _():
            os_ref[...] = x + 0.5 * _rms(y, g_post)
    else:
        op_ref[...] = x + 0.5 * _rms(y, g_post)


def _ffn(xs_in, norm_g, w_gu, w_down, layer, half, n_prompt_rows, split_out):
    split_in = len(xs_in) == 2
    d = xs_in[0].shape[1]
    d_ff = w_down.shape[2]
    tm = FFN_TM
    n_sample_rows = xs_in[1].shape[0] if split_in else xs_in[0].shape[0] - n_prompt_rows
    assert n_prompt_rows % tm == 0 and n_sample_rows == tm and d_ff % FFN_TF == 0
    npt = n_prompt_rows // tm
    n_stream = d_ff // FFN_TF
    cw = 2 * d_ff // n_stream
    rw = d_ff // n_stream
    assert cw % LANES == 0 and rw % 8 == 0
    prompt_tile = lambda i: (jnp.minimum(i, npt - 1), 0)
    const = lambda i: (0, 0)
    in_specs = []
    if split_in:
        in_specs += [pl.BlockSpec((tm, d), prompt_tile),
                     pl.BlockSpec((tm, d), const, pipeline_mode=pl.Buffered(1))]
    else:
        in_specs += [pl.BlockSpec((tm, d), lambda i: (i, 0))]
    in_specs += [pl.BlockSpec(norm_g.shape, lambda i: (0, 0, 0)),
                 pl.BlockSpec(memory_space=pl.ANY),
                 pl.BlockSpec(memory_space=pl.ANY)]
    if split_out:
        out_specs = [pl.BlockSpec((tm, d), prompt_tile), pl.BlockSpec((tm, d), const)]
        out_shape = [jax.ShapeDtypeStruct((n_prompt_rows, d), F32),
                     jax.ShapeDtypeStruct((n_sample_rows, d), F32)]
    else:
        out_specs = pl.BlockSpec((tm, d), lambda i: (i, 0))
        out_shape = jax.ShapeDtypeStruct((n_prompt_rows + n_sample_rows, d), F32)
    weights16 = 3 * d * d_ff * 2
    staging = 2 * (d * cw + rw * d) * 4
    vmem = weights16 + staging + 7 * tm * d * 4 + tm * d_ff * 2 + 10 * MIB
    return pl.pallas_call(
        functools.partial(_ffn_kernel, layer=layer, half=half, n_prompt_tiles=npt,
                          split_in=split_in, split_out=split_out, d_ff=d_ff, tf=FFN_TF,
                          n_stream=n_stream),
        grid=(npt + 1,),
        in_specs=in_specs,
        out_specs=out_specs,
        out_shape=out_shape,
        scratch_shapes=[
            pltpu.VMEM((d, 2 * d_ff), BF16),
            pltpu.VMEM((d_ff, d), BF16),
            pltpu.VMEM((2, d, cw), F32),
            pltpu.VMEM((2, rw, d), F32),
            pltpu.SemaphoreType.DMA((2, 2)),
            pltpu.VMEM((tm, d_ff), BF16),
        ],
        compiler_params=pltpu.CompilerParams(
            dimension_semantics=("arbitrary",), vmem_limit_bytes=vmem),
        name="ffn_half_step",
    )(*xs_in, norm_g, w_gu, w_down)


def _mixp_kernel(sinks_all_ref, x_ref, norm_ref, win_ref, wout_ref, gout_all_ref, wsgu_ref,
                 bsgu_ref, gsgu_all_ref, wconv_all_ref,
                 xo_ref, ko_ref, vo_ref, co_ref,
                 proj_ref, kb_ref, krb_ref, vb_ref, vrb_ref, zbuf_ref, ycat_ref,
                 *, layer, slopes, w_a, w_b, w_kv, w_c):
    g2_ref = norm_ref.at[layer, pl.ds(2, 1)]
    g3_ref = norm_ref.at[layer, pl.ds(3, 1)]
    gout_ref = gout_all_ref.at[pl.ds(layer, 1)]
    gsgu_ref = gsgu_all_ref.at[pl.ds(layer, 1)]
    wconv_ref = wconv_all_ref.at[layer]
    j = pl.program_id(1)
    t_rows = x_ref.shape[0]
    nblk = t_rows // WINDOW
    o_q = 2 * w_a
    o_k = o_q + w_b
    o_v = o_k + w_kv
    o_gb = o_v + w_kv
    o_gc = o_gb + w_c
    o_hc = o_gc + w_c

    @pl.when(j == 0)
    def _():
        kb_ref[0:WINDOW, :] = jnp.zeros((WINDOW, w_kv), BF16)
        krb_ref[0:WINDOW, :] = jnp.zeros((WINDOW, w_kv), BF16)
        vb_ref[0:WINDOW, :] = jnp.zeros((WINDOW, w_kv), BF16)
        vrb_ref[0:WINDOW, :] = jnp.zeros((WINDOW, w_kv), BF16)
        zbuf_ref[0:8, :] = jnp.zeros((8, w_c), F32)

    x = x_ref[...]
    h = _rms(x, g2_ref[...]).astype(BF16)
    proj_ref[...] = _dot(h, win_ref[...])

    u = _gelu(proj_ref[:, 0:w_a])
    v = _head_layer_norm(_gelu(proj_ref[:, w_a:2 * w_a]), gsgu_ref[...])
    lane_a = lax.broadcasted_iota(jnp.int32, (1, w_a), 1)
    ri = lax.broadcasted_iota(jnp.int32, (CHUNK, CHUNK), 0)
    ci = lax.broadcasted_iota(jnp.int32, (CHUNK, CHUNK), 1)
    n_heads_a = w_a // HEAD_DIM
    w_tril = [jnp.where(ri >= ci, wsgu_ref[hh], 0.0).astype(BF16) for hh in range(n_heads_a)]
    bias = bsgu_ref[...]
    ya_parts = []
    for n in range(nblk):
        vblk = v[n * CHUNK:(n + 1) * CHUNK]
        mix = bias
        for hh in range(n_heads_a):
            mh = (lane_a >= hh * HEAD_DIM) & (lane_a < (hh + 1) * HEAD_DIM)
            mix = mix + _dot(w_tril[hh], jnp.where(mh, vblk, 0.0).astype(BF16))
        ya_parts.append(u[n * CHUNK:(n + 1) * CHUNK] * mix)
    ya = jnp.concatenate(ya_parts, axis=0)
    ycat_ref[:, 0:w_a] = _rms(ya, gout_ref[:, 0:w_a]).astype(BF16)

    lane = lax.broadcasted_iota(jnp.int32, (1, LANES), 1)
    lo = lane < HEAD_DIM
    k = proj_ref[:, o_k:o_k + w_kv]
    vv = proj_ref[:, o_v:o_v + w_kv]
    kb_ref[WINDOW:, :] = k.astype(BF16)
    krb_ref[WINDOW:, :] = pltpu.roll(k, HEAD_DIM, 1).astype(BF16)
    vb_ref[WINDOW:, :] = vv.astype(BF16)
    vrb_ref[WINDOW:, :] = pltpu.roll(vv, HEAD_DIM, 1).astype(BF16)

    qi = lax.broadcasted_iota(jnp.int32, (WINDOW, 2 * WINDOW), 0)
    kj = lax.broadcasted_iota(jnp.int32, (WINDOW, 2 * WINDOW), 1)
    dist_i = WINDOW + qi - kj
    band = (dist_i >= 0) & (dist_i < WINDOW)
    dist = dist_i.astype(F32)
    has_prev = jnp.full((WINDOW, 2 * WINDOW), j, jnp.int32) > 0
    band_first = band & ((kj >= WINDOW) | has_prev)

    n_q_cols = w_b // LANES
    half = n_q_cols // 2
    yb_parts = []
    for n in range(nblk):
        rows = slice(n * WINDOW, (n + 1) * WINDOW)
        valid = band_first if n == 0 else band
        qcols = [proj_ref[rows, o_q + c * LANES:o_q + (c + 1) * LANES] for c in range(n_q_cols)]
        keep_a = [lo if c < half else ~lo for c in range(n_q_cols)]
        lhs_a = jnp.concatenate([jnp.where(keep_a[c], qcols[c], 0.0) for c in range(n_q_cols)],
                                axis=0).astype(BF16)
        lhs_b = jnp.concatenate([jnp.where(keep_a[c], 0.0, qcols[c]) for c in range(n_q_cols)],
                                axis=0).astype(BF16)
        band_rows = slice(n * WINDOW, (n + 2) * WINDOW)
        s_a = _dot_nt(lhs_a, kb_ref[band_rows, :])
        s_b = _dot_nt(lhs_b, krb_ref[band_rows, :])
        p_a, p_b = [], []
        for c in range(n_q_cols):
            head_a = 2 * c if c < half else 2 * c + 1
            head_b = 2 * c + 1 if c < half else 2 * c
            for s_all, head, acc in ((s_a, head_a, p_a), (s_b, head_b, p_b)):
                s = s_all[c * WINDOW:(c + 1) * WINDOW] * SCALE - slopes[head] * dist
                s = jnp.where(valid, s, NEG)
                acc.append(_sink_softmax(s, sinks_all_ref[layer, head]).astype(BF16))
        o_a = _dot(jnp.concatenate(p_a, axis=0), vb_ref[band_rows, :])
        o_b = _dot(jnp.concatenate(p_b, axis=0), vrb_ref[band_rows, :])
        cols = []
        for c in range(n_q_cols):
            ra = o_a[c * WINDOW:(c + 1) * WINDOW]
            rb = o_b[c * WINDOW:(c + 1) * WINDOW]
            cols.append(jnp.where(lo, ra, rb) if c < half else jnp.where(lo, rb, ra))
        yb_parts.append(jnp.concatenate(cols, axis=1))
    yb = jnp.concatenate(yb_parts, axis=0)
    ycat_ref[:, w_a:w_a + w_b] = _rms(yb, gout_ref[:, w_a:w_a + w_b]).astype(BF16)
    kb_ref[0:WINDOW, :] = kb_ref[t_rows:t_rows + WINDOW, :]
    krb_ref[0:WINDOW, :] = krb_ref[t_rows:t_rows + WINDOW, :]
    vb_ref[0:WINDOW, :] = vb_ref[t_rows:t_rows + WINDOW, :]
    vrb_ref[0:WINDOW, :] = vrb_ref[t_rows:t_rows + WINDOW, :]

    z = proj_ref[:, o_gc:o_gc + w_c] * proj_ref[:, o_hc:o_hc + w_c]
    zbuf_ref[8:8 + t_rows, :] = z
    conv = zbuf_ref[8 - (CONV_W - 1):8 - (CONV_W - 1) + t_rows, :] * wconv_ref[0:1, :]
    for jj in range(1, CONV_W):
        off = 8 - (CONV_W - 1) + jj
        conv = conv + zbuf_ref[off:off + t_rows, :] * wconv_ref[jj:jj + 1, :]
    yc = proj_ref[:, o_gb:o_gb + w_c] * conv
    ycat_ref[:, w_a + w_b:] = _rms(yc, gout_ref[:, w_a + w_b:]).astype(BF16)
    z_tail = zbuf_ref[8 + t_rows - (CONV_W - 1):8 + t_rows, :]
    zbuf_ref[8 - (CONV_W - 1):8, :] = z_tail

    y = _dot(ycat_ref[...], wout_ref[...])
    xo_ref[...] = x + _rms(y, g3_ref[...])

    @pl.when(j == pl.num_programs(1) - 1)
    def _():
        ko_ref[0] = proj_ref[t_rows - WINDOW:, o_k:o_k + w_kv].T
        vo_ref[0] = proj_ref[t_rows - WINDOW:, o_v:o_v + w_kv].T
        co_ref[0] = z_tail


def _mixer_prompt(x_all, layer, batch, seq, norm_g, win, wout, gout_all, wsgu_all, bsgu_full_all,
                  gsgu_all, sinks_all, wconv_all, w_a, w_b, w_kv, w_c):
    d = x_all.shape[1]
    m = batch * seq
    t = MIX_T
    assert seq % t == 0 and t % WINDOW == 0
    nt = seq // t
    in_w = win.shape[2]
    n_heads_b = w_b // HEAD_DIM
    const2 = lambda b, j, *_: (0, 0)
    const3 = lambda b, j, *_: (0, 0, 0)
    layer3 = lambda b, j, *_: (layer, 0, 0)
    layer4 = lambda b, j, *_: (layer, 0, 0, 0)
    row = lambda b, j, *_: (b * nt + j, 0)
    per_b = lambda b, j, *_: (b, 0, 0)
    kernel = functools.partial(_mixp_kernel, layer=layer, slopes=_alibi_slopes(n_heads_b),
                               w_a=w_a, w_b=w_b, w_kv=w_kv, w_c=w_c)
    grid_spec = pltpu.PrefetchScalarGridSpec(
        num_scalar_prefetch=1,
        grid=(batch, nt),
        in_specs=[
            pl.BlockSpec((t, d), row),
            pl.BlockSpec(norm_g.shape, const3),
            pl.BlockSpec((None,) + win.shape[1:], layer3, pipeline_mode=pl.Buffered(1)),
            pl.BlockSpec((None,) + wout.shape[1:], layer3, pipeline_mode=pl.Buffered(1)),
            pl.BlockSpec(gout_all.shape, const2),
            pl.BlockSpec((None,) + wsgu_all.shape[1:], layer4),
            pl.BlockSpec((None,) + bsgu_full_all.shape[1:], layer3),
            pl.BlockSpec(gsgu_all.shape, const2),
            pl.BlockSpec(wconv_all.shape, const3),
        ],
        out_specs=[
            pl.BlockSpec((t, d), row),
            pl.BlockSpec((1, w_kv, WINDOW), per_b),
            pl.BlockSpec((1, w_kv, WINDOW), per_b),
            pl.BlockSpec((1, CONV_W - 1, w_c), per_b),
        ],
        scratch_shapes=[
            pltpu.VMEM((t, in_w), F32),
            pltpu.VMEM((t + WINDOW, w_kv), BF16),
            pltpu.VMEM((t + WINDOW, w_kv), BF16),
            pltpu.VMEM((t + WINDOW, w_kv), BF16),
            pltpu.VMEM((t + WINDOW, w_kv), BF16),
            pltpu.VMEM((t + 8, w_c), F32),
            pltpu.VMEM((t, gout_all.shape[1]), BF16),
        ],
    )
    depth = win.shape[0]
    vmem = (win.size + wout.size) // depth * 2 + 4 * t * d * 4 + t * in_w * 4 + 24 * MIB
    return pl.pallas_call(
        kernel,
        grid_spec=grid_spec,
        out_shape=[
            jax.ShapeDtypeStruct((m, d), F32),
            jax.ShapeDtypeStruct((batch, w_kv, WINDOW), F32),
            jax.ShapeDtypeStruct((batch, w_kv, WINDOW), F32),
            jax.ShapeDtypeStruct((batch, CONV_W - 1, w_c), F32),
        ],
        compiler_params=pltpu.CompilerParams(
            dimension_semantics=("arbitrary", "arbitrary"), vmem_limit_bytes=vmem),
        name="mixer_prompt",
    )(sinks_all, x_all, norm_g, win, wout, gout_all, wsgu_all, bsgu_full_all, gsgu_all, wconv_all)


def _mixs_kernel(x_ref, norm_ref, win_ref, wout_ref, gout_all_ref, coef_ref, bias_ref, gsgu_all_ref,
                 wconv_all_ref, cexp1_ref, cexp2_ref, sinkcol_ref, kc_ref, vc_ref,
                 xo_ref, vsgu_ref, z_ref, ko_ref, vo_ref,
                 proj_ref, qprep_ref, ybuf_ref, ycat_ref, knt_ref, vnt_ref, vtmp_ref,
                 *, layer, slopes, t_new, w_a, w_b, w_kv, w_c):
    g2_ref = norm_ref.at[layer, pl.ds(2, 1)]
    g3_ref = norm_ref.at[layer, pl.ds(3, 1)]
    gout_ref = gout_all_ref.at[pl.ds(layer, 1)]
    gsgu_ref = gsgu_all_ref.at[pl.ds(layer, 1)]
    wconv_ref = wconv_all_ref.at[layer]
    g = pl.program_id(0)
    rows_all = x_ref.shape[0]
    n_seq = rows_all // t_new
    gsz = kc_ref.shape[0]
    o_q = 2 * w_a
    o_k = o_q + w_b
    o_v = o_k + w_kv
    o_gb = o_v + w_kv
    o_gc = o_gb + w_c
    o_hc = o_gc + w_c
    n_heads = w_b // HEAD_DIM
    group = n_heads // KV_HEADS
    lane = lax.broadcasted_iota(jnp.int32, (1, LANES), 1)
    lo = lane < HEAD_DIM

    @pl.when(g == 0)
    def _():
        x = x_ref[...]
        h = _rms(x, g2_ref[...]).astype(BF16)
        proj_ref[...] = _dot(h, win_ref[...])
        tpos = lax.broadcasted_iota(jnp.int32, (rows_all, 1), 0) % t_new

        u = _gelu(proj_ref[:, 0:w_a])
        v = _head_layer_norm(_gelu(proj_ref[:, w_a:2 * w_a]), gsgu_ref[...])
        for c in range(w_a // LANES):
            vtmp_ref[c] = v[:, c * LANES:(c + 1) * LANES]
        for tt in range(t_new):
            for c in range(w_a // LANES):
                vsgu_ref[tt, c * LANES:(c + 1) * LANES, :] = (
                    vtmp_ref[c, pl.ds(tt, n_seq, stride=t_new), :].T)
        mix = bias_ref[...] + coef_ref[0] * v
        for dlt in range(1, t_new):
            mix = mix + coef_ref[dlt] * pltpu.roll(v, dlt, 0)
        ycat_ref[:, 0:w_a] = _rms(u * mix, gout_ref[:, 0:w_a]).astype(BF16)

        z = proj_ref[:, o_gc:o_gc + w_c] * proj_ref[:, o_hc:o_hc + w_c]
        z_ref[...] = z
        s2 = jnp.where(tpos >= 2, pltpu.roll(z, 2, 0), 0.0) + cexp2_ref[...]
        s1 = jnp.where(tpos >= 1, pltpu.roll(z, 1, 0), 0.0) + cexp1_ref[...]
        conv = s2 * wconv_ref[0:1, :] + s1 * wconv_ref[1:2, :] + z * wconv_ref[2:3, :]
        yc = proj_ref[:, o_gb:o_gb + w_c] * conv
        ycat_ref[:, w_a + w_b:] = _rms(yc, gout_ref[:, w_a + w_b:]).astype(BF16)

        for hd in range(n_heads):
            c, hf, kvh = hd // 2, hd % 2, hd // group
            piece = proj_ref[:, o_q + c * LANES:o_q + (c + 1) * LANES]
            if hf != kvh:
                piece = pltpu.roll(piece, HEAD_DIM, 1)
            qprep_ref[hd] = jnp.where(lo if kvh == 0 else ~lo, piece, 0.0)

        for c in range(rows_all // LANES):
            knt_ref[c] = proj_ref[c * LANES:(c + 1) * LANES, o_k:o_k + w_kv].T
            vnt_ref[c] = proj_ref[c * LANES:(c + 1) * LANES, o_v:o_v + w_kv].T

    pair_rows = 2 * t_new
    n_pairs = gsz // 2
    lrows = n_heads * pair_rows
    ri = lax.broadcasted_iota(jnp.int32, (lrows, 1), 0)
    r_in = ri % pair_rows
    seq_q = r_in // t_new
    t_q = r_in % t_new
    cj = lax.broadcasted_iota(jnp.int32, (1, 2 * WINDOW), 1)
    valid_c = (cj // WINDOW == seq_q) & (cj % WINDOW >= t_q + 1)
    dist_c = (WINDOW + t_q - cj % WINDOW).astype(F32)
    nj = lax.broadcasted_iota(jnp.int32, (1, LANES), 1)
    valid_n = (nj < pair_rows) & (nj // t_new == seq_q) & (nj % t_new <= t_q)
    dist_n = (t_q - nj % t_new).astype(F32)
    slope_col = jnp.zeros((lrows, 1), F32)
    for hd in range(n_heads):
        slope_col = jnp.where(ri // pair_rows == hd, slopes[hd], slope_col)
    sink = sinkcol_ref[...]
    bias_c = slope_col * dist_c
    bias_n = slope_col * dist_n
    zpad = jnp.zeros((LANES - pair_rows, w_kv), F32)

    r0 = pl.multiple_of(g * (n_pairs * pair_rows), n_pairs * pair_rows)
    for p in range(n_pairs):
        rows = pl.ds(r0 + p * pair_rows, pair_rows)
        lhs = jnp.concatenate([qprep_ref[hd, rows, :] for hd in range(n_heads)], axis=0).astype(BF16)
        kn = proj_ref[rows, o_k:o_k + w_kv]
        vn = proj_ref[rows, o_v:o_v + w_kv]
        kc = jnp.concatenate([kc_ref[2 * p], kc_ref[2 * p + 1]], axis=1)
        vc = jnp.concatenate([vc_ref[2 * p], vc_ref[2 * p + 1]], axis=1)
        s_c = _dot(lhs, kc.astype(BF16)) * SCALE - bias_c
        s_n = _dot_nt(lhs, jnp.concatenate([kn, zpad], axis=0).astype(BF16)) * SCALE - bias_n
        s_c = jnp.where(valid_c, s_c, NEG)
        s_n = jnp.where(valid_n, s_n, NEG)
        m = jnp.maximum(jnp.maximum(jnp.max(s_c, axis=-1, keepdims=True),
                                    jnp.max(s_n, axis=-1, keepdims=True)), sink)
        p_c = jnp.exp(s_c - m)
        p_n = jnp.exp(s_n - m)
        den = (jnp.sum(p_c, axis=-1, keepdims=True) + jnp.sum(p_n, axis=-1, keepdims=True)
               + jnp.exp(sink - m))
        inv = 1.0 / den
        o = (_dot_nt((p_c * inv).astype(BF16), vc.astype(BF16))
             + _dot((p_n * inv).astype(BF16), jnp.concatenate([vn, zpad], axis=0).astype(BF16)))
        cols = []
        for c in range(n_heads // 2):
            pieces = []
            for hf in range(2):
                hd = 2 * c + hf
                piece = o[hd * pair_rows:(hd + 1) * pair_rows]
                if hd // group != hf:
                    piece = pltpu.roll(piece, HEAD_DIM, 1)
                pieces.append(piece)
            cols.append(jnp.where(lo, pieces[0], pieces[1]))
        ybuf_ref[rows, :] = jnp.concatenate(cols, axis=1)

    keep = lane < WINDOW - t_new
    kn_t = knt_ref[g]
    vn_t = vnt_ref[g]
    for i in range(gsz):
        shift = (WINDOW - t_new - t_new * i) % LANES
        nk = pltpu.roll(kn_t, shift, 1) if shift else kn_t
        nv = pltpu.roll(vn_t, shift, 1) if shift else vn_t
        ko_ref[i] = jnp.where(keep, pltpu.roll(kc_ref[i], WINDOW - t_new, 1), nk)
        vo_ref[i] = jnp.where(keep, pltpu.roll(vc_ref[i], WINDOW - t_new, 1), nv)

    @pl.when(g == pl.num_programs(0) - 1)
    def _():
        ycat_ref[:, w_a:w_a + w_b] = _rms(ybuf_ref[...], gout_ref[:, w_a:w_a + w_b]).astype(BF16)
        y = _dot(ycat_ref[...], wout_ref[...])
        xo_ref[...] = x_ref[...] + _rms(y, g3_ref[...])


def _mixer_sample(x_all, layer, n_prompt_rows, n_seq, t_new, norm_g, win, wout, gout_all, coef_all,
                  bias_all, gsgu_all, wconv_all, cexp1_all, cexp2_all, sink_col_all, kc_all, vc_all,
                  w_a, w_b, w_kv, w_c):
    d = x_all.shape[1]
    m = n_seq * t_new
    assert n_prompt_rows % m == 0
    in_w = win.shape[2]
    n_heads = w_b // HEAD_DIM
    gsz = LANES // t_new
    assert n_seq % gsz == 0 and gsz % 2 == 0 and 2 * t_new == 8 and WINDOW == LANES
    ng = n_seq // gsz
    const2 = lambda g: (0, 0)
    const3 = lambda g: (0, 0, 0)
    layer3 = lambda g: (layer, 0, 0)
    layer4 = lambda g: (layer, 0, 0, 0)
    step_in = lambda g: (layer, g, 0, 0)
    step_out = lambda g: (g, 0, 0)
    kernel = functools.partial(_mixs_kernel, layer=layer, slopes=_alibi_slopes(n_heads),
                               t_new=t_new, w_a=w_a, w_b=w_b, w_kv=w_kv, w_c=w_c)
    coef_bytes = coef_all.size // coef_all.shape[0] * 4
    depth = win.shape[0]
    vmem = ((win.size + wout.size) // depth * 2 + 4 * m * d * 4 + m * in_w * 4
            + 8 * gsz * WINDOW * w_kv * 4 + coef_bytes * 2 + 24 * MIB)
    return pl.pallas_call(
        kernel,
        grid=(ng,),
        in_specs=[
            pl.BlockSpec((m, d), lambda g: (n_prompt_rows // m, 0)),
            pl.BlockSpec(norm_g.shape, const3),
            pl.BlockSpec((None,) + win.shape[1:], layer3, pipeline_mode=pl.Buffered(1)),
            pl.BlockSpec((None,) + wout.shape[1:], layer3, pipeline_mode=pl.Buffered(1)),
            pl.BlockSpec(gout_all.shape, const2),
            pl.BlockSpec((None,) + coef_all.shape[1:], layer4),
            pl.BlockSpec((None,) + bias_all.shape[1:], layer3),
            pl.BlockSpec(gsgu_all.shape, const2),
            pl.BlockSpec(wconv_all.shape, const3),
            pl.BlockSpec((None,) + cexp1_all.shape[1:], layer3),
            pl.BlockSpec((None,) + cexp2_all.shape[1:], layer3),
            pl.BlockSpec((None,) + sink_col_all.shape[1:], layer3),
            pl.BlockSpec((None, gsz, w_kv, WINDOW), step_in),
            pl.BlockSpec((None, gsz, w_kv, WINDOW), step_in),
        ],
        out_specs=[
            pl.BlockSpec((m, d), const2),
            pl.BlockSpec((t_new, w_a, n_seq), const3),
            pl.BlockSpec((m, w_c), const2),
            pl.BlockSpec((gsz, w_kv, WINDOW), step_out),
            pl.BlockSpec((gsz, w_kv, WINDOW), step_out),
        ],
        out_shape=[
            jax.ShapeDtypeStruct((m, d), F32),
            jax.ShapeDtypeStruct((t_new, w_a, n_seq), F32),
            jax.ShapeDtypeStruct((m, w_c), F32),
            jax.ShapeDtypeStruct((n_seq, w_kv, WINDOW), F32),
            jax.ShapeDtypeStruct((n_seq, w_kv, WINDOW), F32),
        ],
        scratch_shapes=[
            pltpu.VMEM((m, in_w), F32),
            pltpu.VMEM((n_heads, m, LANES), F32),
            pltpu.VMEM((m, w_b), F32),
            pltpu.VMEM((m, gout_all.shape[1]), BF16),
            pltpu.VMEM((ng, w_kv, LANES), F32),
            pltpu.VMEM((ng, w_kv, LANES), F32),
            pltpu.VMEM((w_a // LANES, m, LANES), F32),
        ],
        compiler_params=pltpu.CompilerParams(
            dimension_semantics=("arbitrary",), vmem_limit_bytes=vmem),
        name="mixer_sample",
    )(x_all, norm_g, win, wout, gout_all, coef_all, bias_all, gsgu_all, wconv_all, cexp1_all,
      cexp2_all, sink_col_all, kc_all, vc_all)


def kernel(x_prompt, x_sample, cache_swa_k, cache_swa_v, cache_conv, norm_g, w_ffn_gu, w_ffn_down,
           w_mix_in, w_mix_out, g_mix_out, w_sgu, b_sgu, g_sgu, attn_sinks, w_conv):
    batch, seq, d = x_prompt.shape
    n_seq, t_new, _ = x_sample.shape
    depth = norm_g.shape[0]
    w_a = g_sgu.shape[1]
    w_c = w_conv.shape[2]
    w_kv = KV_HEADS * HEAD_DIM
    w_b = g_mix_out.shape[1] - w_a - w_c
    n_heads_a = w_a // HEAD_DIM
    n_heads_b = w_b // HEAD_DIM
    assert t_new <= CHUNK and seq % CHUNK == 0

    xp = x_prompt.reshape(batch * seq, d)
    xs = x_sample.reshape(n_seq * t_new, d)
    to_dp = lambda c: jnp.transpose(c, (0, 1, 3, 4, 2)).reshape(depth, c.shape[1], w_kv, c.shape[2])
    from_dp = lambda c: jnp.transpose(
        c.reshape(depth, c.shape[1], KV_HEADS, HEAD_DIM, c.shape[3]), (0, 1, 4, 2, 3))
    kc_all = to_dp(cache_swa_k)
    vc_all = to_dp(cache_swa_v)

    win_all = w_mix_in.astype(BF16)
    wout_all = w_mix_out.astype(BF16)
    bsgu_full_all = jnp.repeat(jnp.swapaxes(b_sgu, 1, 2), HEAD_DIM, axis=2)
    w4 = jnp.tril(w_sgu[:, :, :t_new, :t_new])
    tt = np.arange(t_new)
    coef_all = jnp.stack([
        jnp.where((tt >= dlt)[None, None, :], w4[:, :, tt, np.maximum(tt - dlt, 0)], 0.0)
        for dlt in range(t_new)], axis=1)
    coef_all = jnp.repeat(coef_all.transpose(0, 1, 3, 2), HEAD_DIM, axis=3)
    coef_all = jnp.tile(coef_all, (1, 1, n_seq, 1))
    bias_all = jnp.tile(jnp.repeat(jnp.swapaxes(b_sgu[:, :, :t_new], 1, 2), HEAD_DIM, axis=2),
                        (1, n_seq, 1))
    pad_t = lambda c: jnp.pad(c, ((0, 0), (0, 0), (0, t_new - c.shape[2]), (0, 0))).reshape(
        depth, n_seq * t_new, w_c)
    cexp2_all = pad_t(cache_conv)
    cexp1_all = pad_t(cache_conv[:, :, 1:])
    sink_col_all = jnp.repeat(attn_sinks, 2 * t_new, axis=1)[:, :, None]

    n_prompt_rows = batch * seq
    outs = {k: [] for k in ("sgu", "kp", "vp", "ks", "vs", "cp", "cs")}
    x_in = (xp, xs)
    for l in range(depth):
        x_all = _ffn(x_in, norm_g, w_ffn_gu, w_ffn_down, l, 0, n_prompt_rows, False)
        xp, kp, vp, cp = _mixer_prompt(
            x_all, l, batch, seq, norm_g, win_all, wout_all, g_mix_out, w_sgu, bsgu_full_all,
            g_sgu, attn_sinks, w_conv, w_a, w_b, w_kv, w_c)
        xs, vsgu, z_s, ks, vs = _mixer_sample(
            x_all, l, n_prompt_rows, n_seq, t_new, norm_g, win_all, wout_all, g_mix_out,
            coef_all, bias_all, g_sgu, w_conv, cexp1_all, cexp2_all, sink_col_all, kc_all, vc_all,
            w_a, w_b, w_kv, w_c)
        last = l == depth - 1
        res = _ffn((xp, xs), norm_g, w_ffn_gu, w_ffn_down, l, 1, n_prompt_rows, last)
        if last:
            xp, xs = res
        else:
            x_in = (res,)

        outs["sgu"].append(vsgu)
        outs["kp"].append(kp)
        outs["vp"].append(vp)
        outs["ks"].append(ks)
        outs["vs"].append(vs)
        outs["cp"].append(cp)
        outs["cs"].append(z_s.reshape(n_seq, t_new, w_c)[:, t_new - (CONV_W - 1):])
    sgu = jnp.transpose(
        jnp.stack(outs["sgu"]).reshape(depth, t_new, n_heads_a, HEAD_DIM, n_seq), (0, 4, 1, 2, 3))
    return (xp.reshape(batch, seq, d), xs.reshape(n_seq, t_new, d), sgu,
            from_dp(jnp.stack(outs["kp"])), from_dp(jnp.stack(outs["vp"])),
            from_dp(jnp.stack(outs["ks"])), from_dp(jnp.stack(outs["vs"])),
            jnp.stack(outs["cp"]), jnp.stack(outs["cs"]))
```

```python
import functools

import numpy as np
import jax
import jax.numpy as jnp
from jax import lax
from jax.experimental import pallas as pl
from jax.experimental.pallas import tpu as pltpu

F32 = jnp.float32
BF16 = jnp.bfloat16

HEAD_DIM = 64
KV_HEADS = 2
WINDOW = 128
CHUNK = 128
CONV_W = 3
EPS = 1e-6
NEG = -1e30
SCALE = HEAD_DIM ** -0.5
LOG2E = float(np.log2(np.e))
LANES = 128

FFN_TM = 512
FFN_TF = 256
MIX_T = 512
MIB = 1024 * 1024


def _rms(x, g):
    return x * lax.rsqrt(jnp.mean(x * x, axis=-1, keepdims=True) + EPS) * g


def _gelu(x):
    return 0.5 * x * (1.0 + lax.erf(x * np.float32(np.sqrt(0.5))))


def _head_layer_norm(x, g):
    lane = lax.broadcasted_iota(jnp.int32, (1, x.shape[1]), 1)
    out = jnp.zeros_like(x)
    for hh in range(x.shape[1] // HEAD_DIM):
        m = (lane >= hh * HEAD_DIM) & (lane < (hh + 1) * HEAD_DIM)
        mu = jnp.sum(jnp.where(m, x, 0.0), axis=-1, keepdims=True) / HEAD_DIM
        d = jnp.where(m, x - mu, 0.0)
        var = jnp.sum(d * d, axis=-1, keepdims=True) / HEAD_DIM
        out = out + d * lax.rsqrt(var + EPS)
    return out * g


def _dot(a, b):
    return jnp.dot(a, b, preferred_element_type=F32)


def _dot_nt(a, b):
    return lax.dot_general(a, b, (((1,), (1,)), ((), ())), preferred_element_type=F32)


def _sink_softmax(s, sink):
    m = jnp.maximum(jnp.max(s, axis=-1, keepdims=True), sink)
    p = jnp.exp(s - m)
    den = jnp.sum(p, axis=-1, keepdims=True) + jnp.exp(sink - m)
    return p * (1.0 / den)


def _alibi_slopes(n_heads):
    return [float(2.0 ** (-8.0 * h / n_heads)) for h in range(1, n_heads + 1)]


def _ffn_kernel(*refs, layer, half, n_prompt_tiles, split_in, split_out, d_ff, tf, n_stream):
    refs = list(refs)
    xp_ref = refs.pop(0)
    xs_ref = refs.pop(0) if split_in else None
    norm_ref, wgu_hbm, wdn_hbm = refs[:3]
    refs = refs[3:]
    op_ref = refs.pop(0)
    os_ref = refs.pop(0) if split_out else None
    wgu16_ref, wdn16_ref, stage_gu_ref, stage_dn_ref, sem_ref, act_ref = refs
    i = pl.program_id(0)
    cw = stage_gu_ref.shape[2]
    rw = stage_dn_ref.shape[1]

    def gu_copy(c, slot):
        return pltpu.make_async_copy(wgu_hbm.at[layer, half, :, pl.ds(c * cw, cw)],
                                     stage_gu_ref.at[slot], sem_ref.at[0, slot])

    def dn_copy(c, slot):
        return pltpu.make_async_copy(wdn_hbm.at[layer, half, pl.ds(c * rw, rw), :],
                                     stage_dn_ref.at[slot], sem_ref.at[1, slot])

    @pl.when(i == 0)
    def _():
        gu_copy(0, 0).start()
        dn_copy(0, 0).start()
        for c in range(n_stream):
            slot = c % 2
            if c + 1 < n_stream:
                gu_copy(c + 1, 1 - slot).start()
                dn_copy(c + 1, 1 - slot).start()
            gu_copy(c, slot).wait()
            dn_copy(c, slot).wait()
            wgu16_ref[:, c * cw:(c + 1) * cw] = stage_gu_ref[slot].astype(BF16)
            wdn16_ref[c * rw:(c + 1) * rw, :] = stage_dn_ref[slot].astype(BF16)

    if split_in:
        x = jnp.where(i < n_prompt_tiles, xp_ref[...], xs_ref[...])
    else:
        x = xp_ref[...]
    g_pre = norm_ref[layer, 4 * half:4 * half + 1, :]
    g_post = norm_ref[layer, 4 * half + 1:4 * half + 2, :]
    h = _rms(x, g_pre).astype(BF16)
    for c in range(d_ff // tf):
        gate = _dot(h, wgu16_ref[:, c * tf:(c + 1) * tf])
        up = _dot(h, wgu16_ref[:, d_ff + c * tf:d_ff + (c + 1) * tf])
        act_ref[:, c * tf:(c + 1) * tf] = (jax.nn.silu(gate) * up).astype(BF16)
    y = _dot(act_ref[...], wdn16_ref[...])
    if split_out:
        @pl.when(i < n_prompt_tiles)
        def _():
            op_ref[...] = x + 0.5 * _rms(y, g_post)

        @pl.when(i == n_prompt_tiles)
        def _():
            os_ref[...] = x + 0.5 * _rms(y, g_post)
    else:
        op_ref[...] = x + 0.5 * _rms(y, g_post)


def _ffn(xs_in, norm_g, w_gu, w_down, layer, half, n_prompt_rows, split_out):
    split_in = len(xs_in) == 2
    d = xs_in[0].shape[1]
    d_ff = w_down.shape[2]
    tm = FFN_TM
    n_sample_rows = xs_in[1].shape[0] if split_in else xs_in[0].shape[0] - n_prompt_rows
    assert n_prompt_rows % tm == 0 and n_sample_rows == tm and d_ff % FFN_TF == 0
    npt = n_prompt_rows // tm
    n_stream = d_ff // FFN_TF
    cw = 2 * d_ff // n_stream
    rw = d_ff // n_stream
    assert cw % LANES == 0 and rw % 8 == 0
    prompt_tile = lambda i: (jnp.minimum(i, npt - 1), 0)
    const = lambda i: (0, 0)
    in_specs = []
    if split_in:
        in_specs += [pl.BlockSpec((tm, d), prompt_tile),
                     pl.BlockSpec((tm, d), const, pipeline_mode=pl.Buffered(1))]
    else:
        in_specs += [pl.BlockSpec((tm, d), lambda i: (i, 0))]
    in_specs += [pl.BlockSpec(norm_g.shape, lambda i: (0, 0, 0)),
                 pl.BlockSpec(memory_space=pl.ANY),
                 pl.BlockSpec(memory_space=pl.ANY)]
    if split_out:
        out_specs = [pl.BlockSpec((tm, d), prompt_tile), pl.BlockSpec((tm, d), const)]
        out_shape = [jax.ShapeDtypeStruct((n_prompt_rows, d), F32),
                     jax.ShapeDtypeStruct((n_sample_rows, d), F32)]
    else:
        out_specs = pl.BlockSpec((tm, d), lambda i: (i, 0))
        out_shape = jax.ShapeDtypeStruct((n_prompt_rows + n_sample_rows, d), F32)
    weights16 = 3 * d * d_ff * 2
    staging = 2 * (d * cw + rw * d) * 4
    vmem = weights16 + staging + 7 * tm * d * 4 + tm * d_ff * 2 + 10 * MIB
    return pl.pallas_call(
        functools.partial(_ffn_kernel, layer=layer, half=half, n_prompt_tiles=npt,
                          split_in=split_in, split_out=split_out, d_ff=d_ff, tf=FFN_TF,
                          n_stream=n_stream),
        grid=(npt + 1,),
        in_specs=in_specs,
        out_specs=out_specs,
        out_shape=out_shape,
        scratch_shapes=[
            pltpu.VMEM((d, 2 * d_ff), BF16),
            pltpu.VMEM((d_ff, d), BF16),
            pltpu.VMEM((2, d, cw), F32),
            pltpu.VMEM((2, rw, d), F32),
            pltpu.SemaphoreType.DMA((2, 2)),
            pltpu.VMEM((tm, d_ff), BF16),
        ],
        compiler_params=pltpu.CompilerParams(
            dimension_semantics=("arbitrary",), vmem_limit_bytes=vmem),
        name="ffn_half_step",
    )(*xs_in, norm_g, w_gu, w_down)


def _mixp_kernel(*refs, **statics):
    parity = lax.rem(pl.program_id(0), 2)

    @pl.when(parity == 0)
    def _():
        _mixp_body(*refs, parity=0, **statics)

    @pl.when(parity == 1)
    def _():
        _mixp_body(*refs, parity=1, **statics)


def _mixp_body(sinks_all_ref, xnext_ref, x_ref, norm_ref, win_ref, wout_ref, gout_all_ref, wsgu_ref,
               bsgu_ref, gsgu_all_ref, wconv_all_ref,
               xo_ref, ko_ref, vo_ref, co_ref,
               proj_even_ref, proj_odd_ref, kb_ref, krb_ref, vb_ref, vrb_ref, zbuf_ref, ycat_ref,
               abias_ref,
               *, parity, layer, tiles_per_seq, slopes, w_a, w_b, w_kv, w_c):
    proj_next_ref, proj_ref = ((proj_even_ref, proj_odd_ref) if parity == 0
                               else (proj_odd_ref, proj_even_ref))
    g2_ref = norm_ref.at[layer, pl.ds(2, 1)]
    g3_ref = norm_ref.at[layer, pl.ds(3, 1)]
    gout_ref = gout_all_ref.at[pl.ds(layer, 1)]
    gsgu_ref = gsgu_all_ref.at[pl.ds(layer, 1)]
    wconv_ref = wconv_all_ref.at[layer]
    step = pl.program_id(0)
    tile = jnp.maximum(step - 1, 0)
    j = lax.rem(tile, tiles_per_seq)
    t_rows = x_ref.shape[0]
    nblk = t_rows // WINDOW
    o_q = 2 * w_a
    o_k = o_q + w_b
    o_v = o_k + w_kv
    o_gb = o_v + w_kv
    o_gc = o_gb + w_c
    o_hc = o_gc + w_c

    @pl.when(step == 0)
    def _():
        proj_ref[...] = jnp.zeros(proj_ref.shape, F32)
        qi = lax.broadcasted_iota(jnp.int32, (WINDOW, 2 * WINDOW), 0)
        kk = lax.broadcasted_iota(jnp.int32, (WINDOW, 2 * WINDOW), 1)
        dist_i = WINDOW + qi - kk
        band = (dist_i >= 0) & (dist_i < WINDOW)
        dist = dist_i.astype(F32)
        for hd in range(len(slopes)):
            abias_ref[hd] = jnp.where(band, (slopes[hd] * LOG2E) * dist, -NEG)

    @pl.when(j == 0)
    def _():
        kb_ref[0:WINDOW, :] = jnp.zeros((WINDOW, w_kv), BF16)
        krb_ref[0:WINDOW, :] = jnp.zeros((WINDOW, w_kv), BF16)
        vb_ref[0:WINDOW, :] = jnp.zeros((WINDOW, w_kv), BF16)
        vrb_ref[0:WINDOW, :] = jnp.zeros((WINDOW, w_kv), BF16)
        zbuf_ref[0:8, :] = jnp.zeros((8, w_c), F32)

    hn = _rms(xnext_ref[...], g2_ref[...]).astype(BF16)
    n_pchunks = 8
    pcw = proj_next_ref.shape[1] // n_pchunks

    def project_chunks(first, count):
        for c in range(first, first + count):
            proj_next_ref[:, c * pcw:(c + 1) * pcw] = _dot(hn, win_ref[:, c * pcw:(c + 1) * pcw])

    x = x_ref[...]
    project_chunks(0, 1)

    u = _gelu(proj_ref[:, 0:w_a])
    v = _head_layer_norm(_gelu(proj_ref[:, w_a:2 * w_a]), gsgu_ref[...])
    lane_a = lax.broadcasted_iota(jnp.int32, (1, w_a), 1)
    ri = lax.broadcasted_iota(jnp.int32, (CHUNK, CHUNK), 0)
    ci = lax.broadcasted_iota(jnp.int32, (CHUNK, CHUNK), 1)
    n_heads_a = w_a // HEAD_DIM
    w_tril = [jnp.where(ri >= ci, wsgu_ref[hh], 0.0).astype(BF16) for hh in range(n_heads_a)]
    bias = bsgu_ref[...]
    ya_parts = []
    for n in range(nblk):
        vblk = v[n * CHUNK:(n + 1) * CHUNK]
        mix = bias
        for hh in range(n_heads_a):
            mh = (lane_a >= hh * HEAD_DIM) & (lane_a < (hh + 1) * HEAD_DIM)
            mix = mix + _dot(w_tril[hh], jnp.where(mh, vblk, 0.0).astype(BF16))
        ya_parts.append(u[n * CHUNK:(n + 1) * CHUNK] * mix)
    ya = jnp.concatenate(ya_parts, axis=0)
    ycat_ref[:, 0:w_a] = _rms(ya, gout_ref[:, 0:w_a]).astype(BF16)
    project_chunks(1, 1)

    lane = lax.broadcasted_iota(jnp.int32, (1, LANES), 1)
    lo = lane < HEAD_DIM
    k = proj_ref[:, o_k:o_k + w_kv]
    vv = proj_ref[:, o_v:o_v + w_kv]
    kb_ref[WINDOW:, :] = k.astype(BF16)
    krb_ref[WINDOW:, :] = pltpu.roll(k, HEAD_DIM, 1).astype(BF16)
    vb_ref[WINDOW:, :] = vv.astype(BF16)
    vrb_ref[WINDOW:, :] = pltpu.roll(vv, HEAD_DIM, 1).astype(BF16)

    kj = lax.broadcasted_iota(jnp.int32, (WINDOW, 2 * WINDOW), 1)
    has_prev = jnp.full((WINDOW, 2 * WINDOW), j, jnp.int32) > 0
    first_ok = (kj >= WINDOW) | has_prev

    n_q_cols = w_b // LANES
    half = n_q_cols // 2
    yb_parts = []
    for n in range(nblk):
        rows = slice(n * WINDOW, (n + 1) * WINDOW)
        qcols = [proj_ref[rows, o_q + c * LANES:o_q + (c + 1) * LANES] * (SCALE * LOG2E)
                 for c in range(n_q_cols)]
        keep_a = [lo if c < half else ~lo for c in range(n_q_cols)]
        lhs_a = jnp.concatenate([jnp.where(keep_a[c], qcols[c], 0.0) for c in range(n_q_cols)],
                                axis=0).astype(BF16)
        lhs_b = jnp.concatenate([jnp.where(keep_a[c], 0.0, qcols[c]) for c in range(n_q_cols)],
                                axis=0).astype(BF16)
        band_rows = slice(n * WINDOW, (n + 2) * WINDOW)
        s_a = _dot_nt(lhs_a, kb_ref[band_rows, :])
        s_b = _dot_nt(lhs_b, krb_ref[band_rows, :])
        project_chunks(2 + n, 1)
        p_a, p_b, inv_a, inv_b = [], [], [], []
        for c in range(n_q_cols):
            head_a = 2 * c if c < half else 2 * c + 1
            head_b = 2 * c + 1 if c < half else 2 * c
            for s_all, head, acc, inv in ((s_a, head_a, p_a, inv_a), (s_b, head_b, p_b, inv_b)):
                s = s_all[c * WINDOW:(c + 1) * WINDOW] - abias_ref[head]
                if n == 0:
                    s = jnp.where(first_ok, s, NEG)
                sink = sinks_all_ref[layer, head] * LOG2E
                m = jnp.maximum(jnp.max(s, axis=-1, keepdims=True), sink)
                p = jnp.exp2(s - m)
                den = jnp.sum(p, axis=-1, keepdims=True) + jnp.exp2(sink - m)
                acc.append(p.astype(BF16))
                inv.append(1.0 / den)
        o_a = _dot(jnp.concatenate(p_a, axis=0), vb_ref[band_rows, :])
        o_b = _dot(jnp.concatenate(p_b, axis=0), vrb_ref[band_rows, :])
        cols = []
        for c in range(n_q_cols):
            ra = o_a[c * WINDOW:(c + 1) * WINDOW] * inv_a[c]
            rb = o_b[c * WINDOW:(c + 1) * WINDOW] * inv_b[c]
            cols.append(jnp.where(lo, ra, rb) if c < half else jnp.where(lo, rb, ra))
        yb_parts.append(jnp.concatenate(cols, axis=1))
    yb = jnp.concatenate(yb_parts, axis=0)
    ycat_ref[:, w_a:w_a + w_b] = _rms(yb, gout_ref[:, w_a:w_a + w_b]).astype(BF16)
    kb_ref[0:WINDOW, :] = kb_ref[t_rows:t_rows + WINDOW, :]
    krb_ref[0:WINDOW, :] = krb_ref[t_rows:t_rows + WINDOW, :]
    vb_ref[0:WINDOW, :] = vb_ref[t_rows:t_rows + WINDOW, :]
    vrb_ref[0:WINDOW, :] = vrb_ref[t_rows:t_rows + WINDOW, :]

    project_chunks(2 + nblk, 1)

    z = proj_ref[:, o_gc:o_gc + w_c] * proj_ref[:, o_hc:o_hc + w_c]
    zbuf_ref[8:8 + t_rows, :] = z
    conv = zbuf_ref[8 - (CONV_W - 1):8 - (CONV_W - 1) + t_rows, :] * wconv_ref[0:1, :]
    for jj in range(1, CONV_W):
        off = 8 - (CONV_W - 1) + jj
        conv = conv + zbuf_ref[off:off + t_rows, :] * wconv_ref[jj:jj + 1, :]
    yc = proj_ref[:, o_gb:o_gb + w_c] * conv
    ycat_ref[:, w_a + w_b:] = _rms(yc, gout_ref[:, w_a + w_b:]).astype(BF16)
    z_tail = zbuf_ref[8 + t_rows - (CONV_W - 1):8 + t_rows, :]
    zbuf_ref[8 - (CONV_W - 1):8, :] = z_tail

    y = _dot(ycat_ref[...], wout_ref[...])
    project_chunks(3 + nblk, n_pchunks - 3 - nblk)
    xo_ref[...] = x + _rms(y, g3_ref[...])

    @pl.when((j == tiles_per_seq - 1) & (step > 0))
    def _():
        ko_ref[0] = proj_ref[t_rows - WINDOW:, o_k:o_k + w_kv].T
        vo_ref[0] = proj_ref[t_rows - WINDOW:, o_v:o_v + w_kv].T
        co_ref[0] = z_tail


def _mixer_prompt(x_all, layer, batch, seq, norm_g, win, wout, gout_all, wsgu_all, bsgu_full_all,
                  gsgu_all, sinks_all, wconv_all, w_a, w_b, w_kv, w_c):
    d = x_all.shape[1]
    m = batch * seq
    t = MIX_T
    assert seq % t == 0 and t % WINDOW == 0
    nt = seq // t
    in_w = win.shape[2]
    n_heads_b = w_b // HEAD_DIM
    n_tiles = batch * nt
    const2 = lambda s, *_: (0, 0)
    const3 = lambda s, *_: (0, 0, 0)
    layer3 = lambda s, *_: (layer, 0, 0)
    layer4 = lambda s, *_: (layer, 0, 0, 0)
    row_next = lambda s, *_: (jnp.minimum(s, n_tiles - 1), 0)
    row = lambda s, *_: (jnp.maximum(s - 1, 0), 0)
    per_b = lambda s, *_: (jnp.maximum(s - 1, 0) // nt, 0, 0)
    kernel = functools.partial(_mixp_kernel, layer=layer, tiles_per_seq=nt,
                               slopes=_alibi_slopes(n_heads_b),
                               w_a=w_a, w_b=w_b, w_kv=w_kv, w_c=w_c)
    grid_spec = pltpu.PrefetchScalarGridSpec(
        num_scalar_prefetch=1,
        grid=(n_tiles + 1,),
        in_specs=[
            pl.BlockSpec((t, d), row_next),
            pl.BlockSpec((t, d), row),
            pl.BlockSpec(norm_g.shape, const3),
            pl.BlockSpec((None,) + win.shape[1:], layer3, pipeline_mode=pl.Buffered(1)),
            pl.BlockSpec((None,) + wout.shape[1:], layer3, pipeline_mode=pl.Buffered(1)),
            pl.BlockSpec(gout_all.shape, const2),
            pl.BlockSpec((None,) + wsgu_all.shape[1:], layer4),
            pl.BlockSpec((None,) + bsgu_full_all.shape[1:], layer3),
            pl.BlockSpec(gsgu_all.shape, const2),
            pl.BlockSpec(wconv_all.shape, const3),
        ],
        out_specs=[
            pl.BlockSpec((t, d), row),
            pl.BlockSpec((1, w_kv, WINDOW), per_b),
            pl.BlockSpec((1, w_kv, WINDOW), per_b),
            pl.BlockSpec((1, CONV_W - 1, w_c), per_b),
        ],
        scratch_shapes=[
            pltpu.VMEM((t, in_w), F32),
            pltpu.VMEM((t, in_w), F32),
            pltpu.VMEM((t + WINDOW, w_kv), BF16),
            pltpu.VMEM((t + WINDOW, w_kv), BF16),
            pltpu.VMEM((t + WINDOW, w_kv), BF16),
            pltpu.VMEM((t + WINDOW, w_kv), BF16),
            pltpu.VMEM((t + 8, w_c), F32),
            pltpu.VMEM((t, gout_all.shape[1]), BF16),
            pltpu.VMEM((n_heads_b, WINDOW, 2 * WINDOW), F32),
        ],
    )
    depth = win.shape[0]
    vmem = (win.size + wout.size) // depth * 2 + 6 * t * d * 4 + 2 * t * in_w * 4 + 24 * MIB
    return pl.pallas_call(
        kernel,
        grid_spec=grid_spec,
        out_shape=[
            jax.ShapeDtypeStruct((m, d), F32),
            jax.ShapeDtypeStruct((batch, w_kv, WINDOW), F32),
            jax.ShapeDtypeStruct((batch, w_kv, WINDOW), F32),
            jax.ShapeDtypeStruct((batch, CONV_W - 1, w_c), F32),
        ],
        compiler_params=pltpu.CompilerParams(
            dimension_semantics=("arbitrary",), vmem_limit_bytes=vmem),
        name="mixer_prompt",
    )(sinks_all, x_all, x_all, norm_g, win, wout, gout_all, wsgu_all, bsgu_full_all, gsgu_all,
      wconv_all)


def _mixs_kernel(x_ref, norm_ref, win_ref, wout_ref, gout_all_ref, coef_ref, bias_ref, gsgu_all_ref,
                 wconv_all_ref, cexp1_ref, cexp2_ref, sinkcol_ref, kc_ref, vc_ref,
                 xo_ref, vsgu_ref, z_ref, ko_ref, vo_ref,
                 proj_ref, qprep_ref, ybuf_ref, ycat_ref, knt_ref, vnt_ref, vtmp_ref,
                 *, layer, slopes, t_new, w_a, w_b, w_kv, w_c):
    g2_ref = norm_ref.at[layer, pl.ds(2, 1)]
    g3_ref = norm_ref.at[layer, pl.ds(3, 1)]
    gout_ref = gout_all_ref.at[pl.ds(layer, 1)]
    gsgu_ref = gsgu_all_ref.at[pl.ds(layer, 1)]
    wconv_ref = wconv_all_ref.at[layer]
    g = pl.program_id(0)
    rows_all = x_ref.shape[0]
    n_seq = rows_all // t_new
    gsz = kc_ref.shape[0]
    o_q = 2 * w_a
    o_k = o_q + w_b
    o_v = o_k + w_kv
    o_gb = o_v + w_kv
    o_gc = o_gb + w_c
    o_hc = o_gc + w_c
    n_heads = w_b // HEAD_DIM
    group = n_heads // KV_HEADS
    lane = lax.broadcasted_iota(jnp.int32, (1, LANES), 1)
    lo = lane < HEAD_DIM

    @pl.when(g == 0)
    def _():
        x = x_ref[...]
        h = _rms(x, g2_ref[...]).astype(BF16)
        proj_ref[...] = _dot(h, win_ref[...])
        tpos = lax.broadcasted_iota(jnp.int32, (rows_all, 1), 0) % t_new

        u = _gelu(proj_ref[:, 0:w_a])
        v = _head_layer_norm(_gelu(proj_ref[:, w_a:2 * w_a]), gsgu_ref[...])
        for c in range(w_a // LANES):
            vtmp_ref[c] = v[:, c * LANES:(c + 1) * LANES]
        for tt in range(t_new):
            for c in range(w_a // LANES):
                vsgu_ref[tt, c * LANES:(c + 1) * LANES, :] = (
                    vtmp_ref[c, pl.ds(tt, n_seq, stride=t_new), :].T)
        mix = bias_ref[...] + coef_ref[0] * v
        for dlt in range(1, t_new):
            mix = mix + coef_ref[dlt] * pltpu.roll(v, dlt, 0)
        ycat_ref[:, 0:w_a] = _rms(u * mix, gout_ref[:, 0:w_a]).astype(BF16)

        z = proj_ref[:, o_gc:o_gc + w_c] * proj_ref[:, o_hc:o_hc + w_c]
        z_ref[...] = z
        s2 = jnp.where(tpos >= 2, pltpu.roll(z, 2, 0), 0.0) + cexp2_ref[...]
        s1 = jnp.where(tpos >= 1, pltpu.roll(z, 1, 0), 0.0) + cexp1_ref[...]
        conv = s2 * wconv_ref[0:1, :] + s1 * wconv_ref[1:2, :] + z * wconv_ref[2:3, :]
        yc = proj_ref[:, o_gb:o_gb + w_c] * conv
        ycat_ref[:, w_a + w_b:] = _rms(yc, gout_ref[:, w_a + w_b:]).astype(BF16)

        for hd in range(n_heads):
            c, hf, kvh = hd // 2, hd % 2, hd // group
            piece = proj_ref[:, o_q + c * LANES:o_q + (c + 1) * LANES]
            if hf != kvh:
                piece = pltpu.roll(piece, HEAD_DIM, 1)
            qprep_ref[hd] = jnp.where(lo if kvh == 0 else ~lo, piece, 0.0)

        for c in range(rows_all // LANES):
            knt_ref[c] = proj_ref[c * LANES:(c + 1) * LANES, o_k:o_k + w_kv].T
            vnt_ref[c] = proj_ref[c * LANES:(c + 1) * LANES, o_v:o_v + w_kv].T

    pair_rows = 2 * t_new
    n_pairs = gsz // 2
    lrows = n_heads * pair_rows
    ri = lax.broadcasted_iota(jnp.int32, (lrows, 1), 0)
    r_in = ri % pair_rows
    seq_q = r_in // t_new
    t_q = r_in % t_new
    cj = lax.broadcasted_iota(jnp.int32, (1, 2 * WINDOW), 1)
    valid_c = (cj // WINDOW == seq_q) & (cj % WINDOW >= t_q + 1)
    dist_c = (WINDOW + t_q - cj % WINDOW).astype(F32)
    nj = lax.broadcasted_iota(jnp.int32, (1, LANES), 1)
    valid_n = (nj < pair_rows) & (nj // t_new == seq_q) & (nj % t_new <= t_q)
    dist_n = (t_q - nj % t_new).astype(F32)
    slope_col = jnp.zeros((lrows, 1), F32)
    for hd in range(n_heads):
        slope_col = jnp.where(ri // pair_rows == hd, slopes[hd], slope_col)
    sink = sinkcol_ref[...]
    bias_c = slope_col * dist_c
    bias_n = slope_col * dist_n
    zpad = jnp.zeros((LANES - pair_rows, w_kv), F32)

    r0 = pl.multiple_of(g * (n_pairs * pair_rows), n_pairs * pair_rows)
    for p in range(n_pairs):
        rows = pl.ds(r0 + p * pair_rows, pair_rows)
        lhs = jnp.concatenate([qprep_ref[hd, rows, :] for hd in range(n_heads)], axis=0).astype(BF16)
        kn = proj_ref[rows, o_k:o_k + w_kv]
        vn = proj_ref[rows, o_v:o_v + w_kv]
        kc = jnp.concatenate([kc_ref[2 * p], kc_ref[2 * p + 1]], axis=1)
        vc = jnp.concatenate([vc_ref[2 * p], vc_ref[2 * p + 1]], axis=1)
        s_c = _dot(lhs, kc.astype(BF16)) * SCALE - bias_c
        s_n = _dot_nt(lhs, jnp.concatenate([kn, zpad], axis=0).astype(BF16)) * SCALE - bias_n
        s_c = jnp.where(valid_c, s_c, NEG)
        s_n = jnp.where(valid_n, s_n, NEG)
        m = jnp.maximum(jnp.maximum(jnp.max(s_c, axis=-1, keepdims=True),
                                    jnp.max(s_n, axis=-1, keepdims=True)), sink)
        p_c = jnp.exp(s_c - m)
        p_n = jnp.exp(s_n - m)
        den = (jnp.sum(p_c, axis=-1, keepdims=True) + jnp.sum(p_n, axis=-1, keepdims=True)
               + jnp.exp(sink - m))
        inv = 1.0 / den
        o = (_dot_nt((p_c * inv).astype(BF16), vc.astype(BF16))
             + _dot((p_n * inv).astype(BF16), jnp.concatenate([vn, zpad], axis=0).astype(BF16)))
        cols = []
        for c in range(n_heads // 2):
            pieces = []
            for hf in range(2):
                hd = 2 * c + hf
                piece = o[hd * pair_rows:(hd + 1) * pair_rows]
                if hd // group != hf:
                    piece = pltpu.roll(piece, HEAD_DIM, 1)
                pieces.append(piece)
            cols.append(jnp.where(lo, pieces[0], pieces[1]))
        ybuf_ref[rows, :] = jnp.concatenate(cols, axis=1)

    keep = lane < WINDOW - t_new
    kn_t = knt_ref[g]
    vn_t = vnt_ref[g]
    for i in range(gsz):
        shift = (WINDOW - t_new - t_new * i) % LANES
        nk = pltpu.roll(kn_t, shift, 1) if shift else kn_t
        nv = pltpu.roll(vn_t, shift, 1) if shift else vn_t
        ko_ref[i] = jnp.where(keep, pltpu.roll(kc_ref[i], WINDOW - t_new, 1), nk)
        vo_ref[i] = jnp.where(keep, pltpu.roll(vc_ref[i], WINDOW - t_new, 1), nv)

    @pl.when(g == pl.num_programs(0) - 1)
    def _():
        ycat_ref[:, w_a:w_a + w_b] = _rms(ybuf_ref[...], gout_ref[:, w_a:w_a + w_b]).astype(BF16)
        y = _dot(ycat_ref[...], wout_ref[...])
        xo_ref[...] = x_ref[...] + _rms(y, g3_ref[...])


def _mixer_sample(x_all, layer, n_prompt_rows, n_seq, t_new, norm_g, win, wout, gout_all, coef_all,
                  bias_all, gsgu_all, wconv_all, cexp1_all, cexp2_all, sink_col_all, kc_all, vc_all,
                  w_a, w_b, w_kv, w_c):
    d = x_all.shape[1]
    m = n_seq * t_new
    assert n_prompt_rows % m == 0
    in_w = win.shape[2]
    n_heads = w_b // HEAD_DIM
    gsz = LANES // t_new
    assert n_seq % gsz == 0 and gsz % 2 == 0 and 2 * t_new == 8 and WINDOW == LANES
    ng = n_seq // gsz
    const2 = lambda g: (0, 0)
    const3 = lambda g: (0, 0, 0)
    layer3 = lambda g: (layer, 0, 0)
    layer4 = lambda g: (layer, 0, 0, 0)
    step_in = lambda g: (layer, g, 0, 0)
    step_out = lambda g: (g, 0, 0)
    kernel = functools.partial(_mixs_kernel, layer=layer, slopes=_alibi_slopes(n_heads),
                               t_new=t_new, w_a=w_a, w_b=w_b, w_kv=w_kv, w_c=w_c)
    coef_bytes = coef_all.size // coef_all.shape[0] * 4
    depth = win.shape[0]
    vmem = ((win.size + wout.size) // depth * 2 + 4 * m * d * 4 + m * in_w * 4
            + 8 * gsz * WINDOW * w_kv * 4 + coef_bytes * 2 + 24 * MIB)
    return pl.pallas_call(
        kernel,
        grid=(ng,),
        in_specs=[
            pl.BlockSpec((m, d), lambda g: (n_prompt_rows // m, 0)),
            pl.BlockSpec(norm_g.shape, const3),
            pl.BlockSpec((None,) + win.shape[1:], layer3, pipeline_mode=pl.Buffered(1)),
            pl.BlockSpec((None,) + wout.shape[1:], layer3, pipeline_mode=pl.Buffered(1)),
            pl.BlockSpec(gout_all.shape, const2),
            pl.BlockSpec((None,) + coef_all.shape[1:], layer4),
            pl.BlockSpec((None,) + bias_all.shape[1:], layer3),
            pl.BlockSpec(gsgu_all.shape, const2),
            pl.BlockSpec(wconv_all.shape, const3),
            pl.BlockSpec((None,) + cexp1_all.shape[1:], layer3),
            pl.BlockSpec((None,) + cexp2_all.shape[1:], layer3),
            pl.BlockSpec((None,) + sink_col_all.shape[1:], layer3),
            pl.BlockSpec((None, gsz, w_kv, WINDOW), step_in),
            pl.BlockSpec((None, gsz, w_kv, WINDOW), step_in),
        ],
        out_specs=[
            pl.BlockSpec((m, d), const2),
            pl.BlockSpec((t_new, w_a, n_seq), const3),
            pl.BlockSpec((m, w_c), const2),
            pl.BlockSpec((gsz, w_kv, WINDOW), step_out),
            pl.BlockSpec((gsz, w_kv, WINDOW), step_out),
        ],
        out_shape=[
            jax.ShapeDtypeStruct((m, d), F32),
            jax.ShapeDtypeStruct((t_new, w_a, n_seq), F32),
            jax.ShapeDtypeStruct((m, w_c), F32),
            jax.ShapeDtypeStruct((n_seq, w_kv, WINDOW), F32),
            jax.ShapeDtypeStruct((n_seq, w_kv, WINDOW), F32),
        ],
        scratch_shapes=[
            pltpu.VMEM((m, in_w), F32),
            pltpu.VMEM((n_heads, m, LANES), F32),
            pltpu.VMEM((m, w_b), F32),
            pltpu.VMEM((m, gout_all.shape[1]), BF16),
            pltpu.VMEM((ng, w_kv, LANES), F32),
            pltpu.VMEM((ng, w_kv, LANES), F32),
            pltpu.VMEM((w_a // LANES, m, LANES), F32),
        ],
        compiler_params=pltpu.CompilerParams(
            dimension_semantics=("arbitrary",), vmem_limit_bytes=vmem),
        name="mixer_sample",
    )(x_all, norm_g, win, wout, gout_all, coef_all, bias_all, gsgu_all, wconv_all, cexp1_all,
      cexp2_all, sink_col_all, kc_all, vc_all)


def kernel(x_prompt, x_sample, cache_swa_k, cache_swa_v, cache_conv, norm_g, w_ffn_gu, w_ffn_down,
           w_mix_in, w_mix_out, g_mix_out, w_sgu, b_sgu, g_sgu, attn_sinks, w_conv):
    batch, seq, d = x_prompt.shape
    n_seq, t_new, _ = x_sample.shape
    depth = norm_g.shape[0]
    w_a = g_sgu.shape[1]
    w_c = w_conv.shape[2]
    w_kv = KV_HEADS * HEAD_DIM
    w_b = g_mix_out.shape[1] - w_a - w_c
    n_heads_a = w_a // HEAD_DIM
    n_heads_b = w_b // HEAD_DIM
    assert t_new <= CHUNK and seq % CHUNK == 0

    xp = x_prompt.reshape(batch * seq, d)
    xs = x_sample.reshape(n_seq * t_new, d)
    to_dp = lambda c: jnp.transpose(c, (0, 1, 3, 4, 2)).reshape(depth, c.shape[1], w_kv, c.shape[2])
    from_dp = lambda c: jnp.transpose(
        c.reshape(depth, c.shape[1], KV_HEADS, HEAD_DIM, c.shape[3]), (0, 1, 4, 2, 3))
    kc_all = to_dp(cache_swa_k)
    vc_all = to_dp(cache_swa_v)

    win_all = w_mix_in.astype(BF16)
    wout_all = w_mix_out.astype(BF16)
    bsgu_full_all = jnp.repeat(jnp.swapaxes(b_sgu, 1, 2), HEAD_DIM, axis=2)
    w4 = jnp.tril(w_sgu[:, :, :t_new, :t_new])
    tt = np.arange(t_new)
    coef_all = jnp.stack([
        jnp.where((tt >= dlt)[None, None, :], w4[:, :, tt, np.maximum(tt - dlt, 0)], 0.0)
        for dlt in range(t_new)], axis=1)
    coef_all = jnp.repeat(coef_all.transpose(0, 1, 3, 2), HEAD_DIM, axis=3)
    coef_all = jnp.tile(coef_all, (1, 1, n_seq, 1))
    bias_all = jnp.tile(jnp.repeat(jnp.swapaxes(b_sgu[:, :, :t_new], 1, 2), HEAD_DIM, axis=2),
                        (1, n_seq, 1))
    pad_t = lambda c: jnp.pad(c, ((0, 0), (0, 0), (0, t_new - c.shape[2]), (0, 0))).reshape(
        depth, n_seq * t_new, w_c)
    cexp2_all = pad_t(cache_conv)
    cexp1_all = pad_t(cache_conv[:, :, 1:])
    sink_col_all = jnp.repeat(attn_sinks, 2 * t_new, axis=1)[:, :, None]

    n_prompt_rows = batch * seq
    outs = {k: [] for k in ("sgu", "kp", "vp", "ks", "vs", "cp", "cs")}
    x_in = (xp, xs)
    for l in range(depth):
        x_all = _ffn(x_in, norm_g, w_ffn_gu, w_ffn_down, l, 0, n_prompt_rows, False)
        xp, kp, vp, cp = _mixer_prompt(
            x_all, l, batch, seq, norm_g, win_all, wout_all, g_mix_out, w_sgu, bsgu_full_all,
            g_sgu, attn_sinks, w_conv, w_a, w_b, w_kv, w_c)
        xs, vsgu, z_s, ks, vs = _mixer_sample(
            x_all, l, n_prompt_rows, n_seq, t_new, norm_g, win_all, wout_all, g_mix_out,
            coef_all, bias_all, g_sgu, w_conv, cexp1_all, cexp2_all, sink_col_all, kc_all, vc_all,
            w_a, w_b, w_kv, w_c)
        last = l == depth - 1
        res = _ffn((xp, xs), norm_g, w_ffn_gu, w_ffn_down, l, 1, n_prompt_rows, last)
        if last:
            xp, xs = res
        else:
            x_in = (res,)

        outs["sgu"].append(vsgu)
        outs["kp"].append(kp)
        outs["vp"].append(vp)
        outs["ks"].append(ks)
        outs["vs"].append(vs)
        outs["cp"].append(cp)
        outs["cs"].append(z_s.reshape(n_seq, t_new, w_c)[:, t_new - (CONV_W - 1):])
    sgu = jnp.transpose(
        jnp.stack(outs["sgu"]).reshape(depth, t_new, n_heads_a, HEAD_DIM, n_seq), (0, 4, 1, 2, 3))
    return (xp.reshape(batch, seq, d), xs.reshape(n_seq, t_new, d), sgu,
            from_dp(jnp.stack(outs["kp"])), from_dp(jnp.stack(outs["vp"])),
            from_dp(jnp.stack(outs["ks"])), from_dp(jnp.stack(outs["vs"])),
            jnp.stack(outs["cp"]), jnp.stack(outs["cs"]))
```

```python
import functools

import numpy as np
import jax
import jax.numpy as jnp
from jax import lax
from jax.experimental import pallas as pl
from jax.experimental.pallas import tpu as pltpu

F32 = jnp.float32
BF16 = jnp.bfloat16

HEAD_DIM = 64
KV_HEADS = 2
WINDOW = 128
CHUNK = 128
CONV_W = 3
EPS = 1e-6
NEG = -1e30
SCALE = HEAD_DIM ** -0.5
LOG2E = float(np.log2(np.e))
LANES = 128

FFN_TM = 512
FFN_TF = 256
MIX_T = 512
MIB = 1024 * 1024


def _rms(x, g):
    return x * lax.rsqrt(jnp.mean(x * x, axis=-1, keepdims=True) + EPS) * g


def _gelu(x):
    return 0.5 * x * (1.0 + lax.erf(x * np.float32(np.sqrt(0.5))))


def _head_layer_norm(x, g):
    lane = lax.broadcasted_iota(jnp.int32, (1, x.shape[1]), 1)
    out = jnp.zeros_like(x)
    for hh in range(x.shape[1] // HEAD_DIM):
        m = (lane >= hh * HEAD_DIM) & (lane < (hh + 1) * HEAD_DIM)
        mu = jnp.sum(jnp.where(m, x, 0.0), axis=-1, keepdims=True) / HEAD_DIM
        d = jnp.where(m, x - mu, 0.0)
        var = jnp.sum(d * d, axis=-1, keepdims=True) / HEAD_DIM
        out = out + d * lax.rsqrt(var + EPS)
    return out * g


def _dot(a, b):
    return jnp.dot(a, b, preferred_element_type=F32)


def _dot_nt(a, b):
    return lax.dot_general(a, b, (((1,), (1,)), ((), ())), preferred_element_type=F32)


def _sink_softmax(s, sink):
    m = jnp.maximum(jnp.max(s, axis=-1, keepdims=True), sink)
    p = jnp.exp(s - m)
    den = jnp.sum(p, axis=-1, keepdims=True) + jnp.exp(sink - m)
    return p * (1.0 / den)


def _alibi_slopes(n_heads):
    return [float(2.0 ** (-8.0 * h / n_heads)) for h in range(1, n_heads + 1)]


def _ffn_kernel(*refs, layer, half, n_prompt_tiles, split_in, split_out, d_ff, tf, n_stream):
    refs = list(refs)
    xp_ref = refs.pop(0)
    xs_ref = refs.pop(0) if split_in else None
    norm_ref, wgu_hbm, wdn_hbm = refs[:3]
    refs = refs[3:]
    op_ref = refs.pop(0)
    os_ref = refs.pop(0) if split_out else None
    wgu16_ref, wdn16_ref, stage_gu_ref, stage_dn_ref, sem_ref, act_ref = refs
    i = pl.program_id(0)
    cw = stage_gu_ref.shape[2]
    rw = stage_dn_ref.shape[1]

    def gu_copy(c, slot):
        return pltpu.make_async_copy(wgu_hbm.at[layer, half, :, pl.ds(c * cw, cw)],
                                     stage_gu_ref.at[slot], sem_ref.at[0, slot])

    def dn_copy(c, slot):
        return pltpu.make_async_copy(wdn_hbm.at[layer, half, pl.ds(c * rw, rw), :],
                                     stage_dn_ref.at[slot], sem_ref.at[1, slot])

    @pl.when(i == 0)
    def _():
        gu_copy(0, 0).start()
        dn_copy(0, 0).start()
        for c in range(n_stream):
            slot = c % 2
            if c + 1 < n_stream:
                gu_copy(c + 1, 1 - slot).start()
                dn_copy(c + 1, 1 - slot).start()
            gu_copy(c, slot).wait()
            dn_copy(c, slot).wait()
            wgu16_ref[:, c * cw:(c + 1) * cw] = stage_gu_ref[slot].astype(BF16)
            wdn16_ref[c * rw:(c + 1) * rw, :] = stage_dn_ref[slot].astype(BF16)

    if split_in and split_out:
        x = jnp.where(i == 0, xs_ref[...], xp_ref[...])
    elif split_in:
        x = jnp.where(i < n_prompt_tiles, xp_ref[...], xs_ref[...])
    else:
        x = xp_ref[...]
    g_pre = norm_ref[layer, 4 * half:4 * half + 1, :]
    g_post = norm_ref[layer, 4 * half + 1:4 * half + 2, :]
    n_chunks = d_ff // tf

    def hidden_chunks(h, rows, first, count):
        for c in range(first, first + count):
            gate = _dot(h, wgu16_ref[:, c * tf:(c + 1) * tf])
            up = _dot(h, wgu16_ref[:, d_ff + c * tf:d_ff + (c + 1) * tf])
            act_ref[rows, c * tf:(c + 1) * tf] = (jax.nn.silu(gate) * up).astype(BF16)

    def store(rows, val):
        op_ref[rows, :] = val
        if split_out:
            os_ref[rows, :] = val

    hr = x.shape[0] // 2
    ra, rb = slice(0, hr), slice(hr, 2 * hr)
    xa, xb = x[ra], x[rb]
    ha = _rms(xa, g_pre).astype(BF16)
    hidden_chunks(ha, ra, 0, 2)
    hb = _rms(xb, g_pre).astype(BF16)
    hidden_chunks(ha, ra, 2, n_chunks - 2)
    ya = _dot(act_ref[ra, :], wdn16_ref[...])
    hidden_chunks(hb, rb, 0, 3)
    store(ra, xa + 0.5 * _rms(ya, g_post))
    hidden_chunks(hb, rb, 3, n_chunks - 3)
    yb = _dot(act_ref[rb, :], wdn16_ref[...])
    store(rb, xb + 0.5 * _rms(yb, g_post))


def _ffn(xs_in, norm_g, w_gu, w_down, layer, half, n_prompt_rows, split_out):
    split_in = len(xs_in) == 2
    d = xs_in[0].shape[1]
    d_ff = w_down.shape[2]
    tm = FFN_TM
    n_sample_rows = xs_in[1].shape[0] if split_in else xs_in[0].shape[0] - n_prompt_rows
    assert n_prompt_rows % tm == 0 and n_sample_rows == tm and d_ff % FFN_TF == 0
    npt = n_prompt_rows // tm
    n_stream = d_ff // FFN_TF
    cw = 2 * d_ff // n_stream
    rw = d_ff // n_stream
    assert cw % LANES == 0 and rw % 8 == 0
    assert split_in or not split_out
    if split_out:
        prompt_tile = lambda i: (jnp.maximum(i - 1, 0), 0)
    else:
        prompt_tile = lambda i: (jnp.minimum(i, npt - 1), 0)
    const = lambda i: (0, 0)
    in_specs = []
    if split_in:
        in_specs += [pl.BlockSpec((tm, d), prompt_tile),
                     pl.BlockSpec((tm, d), const, pipeline_mode=pl.Buffered(1))]
    else:
        in_specs += [pl.BlockSpec((tm, d), lambda i: (i, 0))]
    in_specs += [pl.BlockSpec(norm_g.shape, lambda i: (0, 0, 0)),
                 pl.BlockSpec(memory_space=pl.ANY),
                 pl.BlockSpec(memory_space=pl.ANY)]
    if split_out:
        out_specs = [pl.BlockSpec((tm, d), prompt_tile),
                     pl.BlockSpec((tm, d), lambda i: (jnp.minimum(i, 1), 0))]
        out_shape = [jax.ShapeDtypeStruct((n_prompt_rows, d), F32),
                     jax.ShapeDtypeStruct((2 * n_sample_rows, d), F32)]
    else:
        out_specs = pl.BlockSpec((tm, d), lambda i: (i, 0))
        out_shape = jax.ShapeDtypeStruct((n_prompt_rows + n_sample_rows, d), F32)
    weights16 = 3 * d * d_ff * 2
    staging = 2 * (d * cw + rw * d) * 4
    vmem = weights16 + staging + 7 * tm * d * 4 + tm * d_ff * 2 + 10 * MIB
    return pl.pallas_call(
        functools.partial(_ffn_kernel, layer=layer, half=half, n_prompt_tiles=npt,
                          split_in=split_in, split_out=split_out, d_ff=d_ff, tf=FFN_TF,
                          n_stream=n_stream),
        grid=(npt + 1,),
        in_specs=in_specs,
        out_specs=out_specs,
        out_shape=out_shape,
        scratch_shapes=[
            pltpu.VMEM((d, 2 * d_ff), BF16),
            pltpu.VMEM((d_ff, d), BF16),
            pltpu.VMEM((2, d, cw), F32),
            pltpu.VMEM((2, rw, d), F32),
            pltpu.SemaphoreType.DMA((2, 2)),
            pltpu.VMEM((tm, d_ff), BF16),
        ],
        compiler_params=pltpu.CompilerParams(
            dimension_semantics=("arbitrary",), vmem_limit_bytes=vmem),
        name="ffn_half_step",
    )(*xs_in, norm_g, w_gu, w_down)


def _mixp_kernel(*refs, **statics):
    parity = lax.rem(pl.program_id(0), 2)

    @pl.when(parity == 0)
    def _():
        _mixp_body(*refs, parity=0, **statics)

    @pl.when(parity == 1)
    def _():
        _mixp_body(*refs, parity=1, **statics)


def _mixp_body(sinks_all_ref, xnext_ref, x_ref, norm_ref, win_ref, wout_ref, gout_all_ref, wsgu_ref,
               bsgu_ref, gsgu_all_ref, wconv_all_ref,
               xo_ref, ko_ref, vo_ref, co_ref,
               proj_even_ref, proj_odd_ref, kb_ref, krb_ref, vb_ref, vrb_ref, zbuf_ref, ycat_ref,
               abias_ref,
               *, parity, layer, tiles_per_seq, slopes, w_a, w_b, w_kv, w_c):
    proj_next_ref, proj_ref = ((proj_even_ref, proj_odd_ref) if parity == 0
                               else (proj_odd_ref, proj_even_ref))
    g2_ref = norm_ref.at[layer, pl.ds(2, 1)]
    g3_ref = norm_ref.at[layer, pl.ds(3, 1)]
    gout_ref = gout_all_ref.at[pl.ds(layer, 1)]
    gsgu_ref = gsgu_all_ref.at[pl.ds(layer, 1)]
    wconv_ref = wconv_all_ref.at[layer]
    step = pl.program_id(0)
    tile = jnp.maximum(step - 1, 0)
    j = lax.rem(tile, tiles_per_seq)
    t_rows = x_ref.shape[0]
    nblk = t_rows // WINDOW
    o_q = 2 * w_a
    o_k = o_q + w_b
    o_v = o_k + w_kv
    o_gb = o_v + w_kv
    o_gc = o_gb + w_c
    o_hc = o_gc + w_c

    @pl.when(step == 0)
    def _():
        proj_ref[...] = jnp.zeros(proj_ref.shape, F32)
        qi = lax.broadcasted_iota(jnp.int32, (WINDOW, 2 * WINDOW), 0)
        kk = lax.broadcasted_iota(jnp.int32, (WINDOW, 2 * WINDOW), 1)
        dist_i = WINDOW + qi - kk
        band = (dist_i >= 0) & (dist_i < WINDOW)
        dist = dist_i.astype(F32)
        for hd in range(len(slopes)):
            abias_ref[hd] = jnp.where(band, (slopes[hd] * LOG2E) * dist, -NEG)

    @pl.when(j == 0)
    def _():
        kb_ref[0:WINDOW, :] = jnp.zeros((WINDOW, w_kv), BF16)
        krb_ref[0:WINDOW, :] = jnp.zeros((WINDOW, w_kv), BF16)
        vb_ref[0:WINDOW, :] = jnp.zeros((WINDOW, w_kv), BF16)
        vrb_ref[0:WINDOW, :] = jnp.zeros((WINDOW, w_kv), BF16)
        zbuf_ref[0:8, :] = jnp.zeros((8, w_c), F32)

    hn = _rms(xnext_ref[...], g2_ref[...]).astype(BF16)
    n_pchunks = 8
    pcw = proj_next_ref.shape[1] // n_pchunks

    def project_chunks(first, count):
        for c in range(first, first + count):
            proj_next_ref[:, c * pcw:(c + 1) * pcw] = _dot(hn, win_ref[:, c * pcw:(c + 1) * pcw])

    x = x_ref[...]
    project_chunks(0, 1)

    u = _gelu(proj_ref[:, 0:w_a])
    v = _head_layer_norm(_gelu(proj_ref[:, w_a:2 * w_a]), gsgu_ref[...])
    lane_a = lax.broadcasted_iota(jnp.int32, (1, w_a), 1)
    ri = lax.broadcasted_iota(jnp.int32, (CHUNK, CHUNK), 0)
    ci = lax.broadcasted_iota(jnp.int32, (CHUNK, CHUNK), 1)
    n_heads_a = w_a // HEAD_DIM
    w_tril = [jnp.where(ri >= ci, wsgu_ref[hh], 0.0).astype(BF16) for hh in range(n_heads_a)]
    bias = bsgu_ref[...]
    ya_parts = []
    for n in range(nblk):
        vblk = v[n * CHUNK:(n + 1) * CHUNK]
        mix = bias
        for hh in range(n_heads_a):
            mh = (lane_a >= hh * HEAD_DIM) & (lane_a < (hh + 1) * HEAD_DIM)
            mix = mix + _dot(w_tril[hh], jnp.where(mh, vblk, 0.0).astype(BF16))
        ya_parts.append(u[n * CHUNK:(n + 1) * CHUNK] * mix)
    ya = jnp.concatenate(ya_parts, axis=0)
    ycat_ref[:, 0:w_a] = _rms(ya, gout_ref[:, 0:w_a]).astype(BF16)
    project_chunks(1, 1)

    lane = lax.broadcasted_iota(jnp.int32, (1, LANES), 1)
    lo = lane < HEAD_DIM
    k = proj_ref[:, o_k:o_k + w_kv]
    vv = proj_ref[:, o_v:o_v + w_kv]
    kb_ref[WINDOW:, :] = k.astype(BF16)
    krb_ref[WINDOW:, :] = pltpu.roll(k, HEAD_DIM, 1).astype(BF16)
    vb_ref[WINDOW:, :] = vv.astype(BF16)
    vrb_ref[WINDOW:, :] = pltpu.roll(vv, HEAD_DIM, 1).astype(BF16)

    kj = lax.broadcasted_iota(jnp.int32, (WINDOW, 2 * WINDOW), 1)
    has_prev = jnp.full((WINDOW, 2 * WINDOW), j, jnp.int32) > 0
    first_ok = (kj >= WINDOW) | has_prev

    n_q_cols = w_b // LANES
    half = n_q_cols // 2
    yb_parts = []
    for n in range(nblk):
        rows = slice(n * WINDOW, (n + 1) * WINDOW)
        qcols = [proj_ref[rows, o_q + c * LANES:o_q + (c + 1) * LANES] * (SCALE * LOG2E)
                 for c in range(n_q_cols)]
        keep_a = [lo if c < half else ~lo for c in range(n_q_cols)]
        lhs_a = jnp.concatenate([jnp.where(keep_a[c], qcols[c], 0.0) for c in range(n_q_cols)],
                                axis=0).astype(BF16)
        lhs_b = jnp.concatenate([jnp.where(keep_a[c], 0.0, qcols[c]) for c in range(n_q_cols)],
                                axis=0).astype(BF16)
        band_rows = slice(n * WINDOW, (n + 2) * WINDOW)
        s_a = _dot_nt(lhs_a, kb_ref[band_rows, :])
        s_b = _dot_nt(lhs_b, krb_ref[band_rows, :])
        project_chunks(2 + n, 1)
        p_a, p_b, inv_a, inv_b = [], [], [], []
        for c in range(n_q_cols):
            head_a = 2 * c if c < half else 2 * c + 1
            head_b = 2 * c + 1 if c < half else 2 * c
            for s_all, head, acc, inv in ((s_a, head_a, p_a, inv_a), (s_b, head_b, p_b, inv_b)):
                s = s_all[c * WINDOW:(c + 1) * WINDOW] - abias_ref[head]
                if n == 0:
                    s = jnp.where(first_ok, s, NEG)
                sink = sinks_all_ref[layer, head] * LOG2E
                m = jnp.maximum(jnp.max(s, axis=-1, keepdims=True), sink)
                p = jnp.exp2(s - m)
                den = jnp.sum(p, axis=-1, keepdims=True) + jnp.exp2(sink - m)
                acc.append(p.astype(BF16))
                inv.append(1.0 / den)
        o_a = _dot(jnp.concatenate(p_a, axis=0), vb_ref[band_rows, :])
        o_b = _dot(jnp.concatenate(p_b, axis=0), vrb_ref[band_rows, :])
        cols = []
        for c in range(n_q_cols):
            ra = o_a[c * WINDOW:(c + 1) * WINDOW] * inv_a[c]
            rb = o_b[c * WINDOW:(c + 1) * WINDOW] * inv_b[c]
            cols.append(jnp.where(lo, ra, rb) if c < half else jnp.where(lo, rb, ra))
        yb_parts.append(jnp.concatenate(cols, axis=1))
    yb = jnp.concatenate(yb_parts, axis=0)
    ycat_ref[:, w_a:w_a + w_b] = _rms(yb, gout_ref[:, w_a:w_a + w_b]).astype(BF16)
    kb_ref[0:WINDOW, :] = kb_ref[t_rows:t_rows + WINDOW, :]
    krb_ref[0:WINDOW, :] = krb_ref[t_rows:t_rows + WINDOW, :]
    vb_ref[0:WINDOW, :] = vb_ref[t_rows:t_rows + WINDOW, :]
    vrb_ref[0:WINDOW, :] = vrb_ref[t_rows:t_rows + WINDOW, :]

    project_chunks(2 + nblk, 1)

    z = proj_ref[:, o_gc:o_gc + w_c] * proj_ref[:, o_hc:o_hc + w_c]
    zbuf_ref[8:8 + t_rows, :] = z
    conv = zbuf_ref[8 - (CONV_W - 1):8 - (CONV_W - 1) + t_rows, :] * wconv_ref[0:1, :]
    for jj in range(1, CONV_W):
        off = 8 - (CONV_W - 1) + jj
        conv = conv + zbuf_ref[off:off + t_rows, :] * wconv_ref[jj:jj + 1, :]
    yc = proj_ref[:, o_gb:o_gb + w_c] * conv
    ycat_ref[:, w_a + w_b:] = _rms(yc, gout_ref[:, w_a + w_b:]).astype(BF16)
    z_tail = zbuf_ref[8 + t_rows - (CONV_W - 1):8 + t_rows, :]
    zbuf_ref[8 - (CONV_W - 1):8, :] = z_tail

    y = _dot(ycat_ref[...], wout_ref[...])
    project_chunks(3 + nblk, n_pchunks - 3 - nblk)
    xo_ref[...] = x + _rms(y, g3_ref[...])

    @pl.when((j == tiles_per_seq - 1) & (step > 0))
    def _():
        ko_ref[0] = proj_ref[t_rows - WINDOW:, o_k:o_k + w_kv].T
        vo_ref[0] = proj_ref[t_rows - WINDOW:, o_v:o_v + w_kv].T
        co_ref[0] = z_tail


def _mixer_prompt(x_all, layer, batch, seq, norm_g, win, wout, gout_all, wsgu_all, bsgu_full_all,
                  gsgu_all, sinks_all, wconv_all, w_a, w_b, w_kv, w_c):
    d = x_all.shape[1]
    m = batch * seq
    t = MIX_T
    assert seq % t == 0 and t % WINDOW == 0
    nt = seq // t
    in_w = win.shape[2]
    n_heads_b = w_b // HEAD_DIM
    n_tiles = batch * nt
    const2 = lambda s, *_: (0, 0)
    const3 = lambda s, *_: (0, 0, 0)
    layer3 = lambda s, *_: (layer, 0, 0)
    layer4 = lambda s, *_: (layer, 0, 0, 0)
    row_next = lambda s, *_: (jnp.minimum(s, n_tiles - 1), 0)
    row = lambda s, *_: (jnp.maximum(s - 1, 0), 0)
    per_b = lambda s, *_: (jnp.maximum(s - 1, 0) // nt, 0, 0)
    kernel = functools.partial(_mixp_kernel, layer=layer, tiles_per_seq=nt,
                               slopes=_alibi_slopes(n_heads_b),
                               w_a=w_a, w_b=w_b, w_kv=w_kv, w_c=w_c)
    grid_spec = pltpu.PrefetchScalarGridSpec(
        num_scalar_prefetch=1,
        grid=(n_tiles + 1,),
        in_specs=[
            pl.BlockSpec((t, d), row_next),
            pl.BlockSpec((t, d), row),
            pl.BlockSpec(norm_g.shape, const3),
            pl.BlockSpec((None,) + win.shape[1:], layer3, pipeline_mode=pl.Buffered(1)),
            pl.BlockSpec((None,) + wout.shape[1:], layer3, pipeline_mode=pl.Buffered(1)),
            pl.BlockSpec(gout_all.shape, const2),
            pl.BlockSpec((None,) + wsgu_all.shape[1:], layer4),
            pl.BlockSpec((None,) + bsgu_full_all.shape[1:], layer3),
            pl.BlockSpec(gsgu_all.shape, const2),
            pl.BlockSpec(wconv_all.shape, const3),
        ],
        out_specs=[
            pl.BlockSpec((t, d), row),
            pl.BlockSpec((1, w_kv, WINDOW), per_b),
            pl.BlockSpec((1, w_kv, WINDOW), per_b),
            pl.BlockSpec((1, CONV_W - 1, w_c), per_b),
        ],
        scratch_shapes=[
            pltpu.VMEM((t, in_w), F32),
            pltpu.VMEM((t, in_w), F32),
            pltpu.VMEM((t + WINDOW, w_kv), BF16),
            pltpu.VMEM((t + WINDOW, w_kv), BF16),
            pltpu.VMEM((t + WINDOW, w_kv), BF16),
            pltpu.VMEM((t + WINDOW, w_kv), BF16),
            pltpu.VMEM((t + 8, w_c), F32),
            pltpu.VMEM((t, gout_all.shape[1]), BF16),
            pltpu.VMEM((n_heads_b, WINDOW, 2 * WINDOW), F32),
        ],
    )
    depth = win.shape[0]
    vmem = (win.size + wout.size) // depth * 2 + 6 * t * d * 4 + 2 * t * in_w * 4 + 24 * MIB
    return pl.pallas_call(
        kernel,
        grid_spec=grid_spec,
        out_shape=[
            jax.ShapeDtypeStruct((m, d), F32),
            jax.ShapeDtypeStruct((batch, w_kv, WINDOW), F32),
            jax.ShapeDtypeStruct((batch, w_kv, WINDOW), F32),
            jax.ShapeDtypeStruct((batch, CONV_W - 1, w_c), F32),
        ],
        compiler_params=pltpu.CompilerParams(
            dimension_semantics=("arbitrary",), vmem_limit_bytes=vmem),
        name="mixer_prompt",
    )(sinks_all, x_all, x_all, norm_g, win, wout, gout_all, wsgu_all, bsgu_full_all, gsgu_all,
      wconv_all)


def _mixs_kernel(x_ref, norm_ref, win_ref, wout_ref, gout_all_ref, coef_ref, bias_ref, gsgu_all_ref,
                 wconv_all_ref, cexp1_ref, cexp2_ref, sinkcol_ref, kc_ref, vc_ref,
                 xo_ref, vsgu_ref, z_ref, ko_ref, vo_ref,
                 proj_ref, qprep_ref, ybuf_ref, ycat_ref, knt_ref, vnt_ref, vtmp_ref,
                 *, layer, slopes, t_new, w_a, w_b, w_kv, w_c):
    g2_ref = norm_ref.at[layer, pl.ds(2, 1)]
    g3_ref = norm_ref.at[layer, pl.ds(3, 1)]
    gout_ref = gout_all_ref.at[pl.ds(layer, 1)]
    gsgu_ref = gsgu_all_ref.at[pl.ds(layer, 1)]
    wconv_ref = wconv_all_ref.at[layer]
    g = pl.program_id(0)
    rows_all = x_ref.shape[0]
    n_seq = rows_all // t_new
    gsz = kc_ref.shape[0]
    o_q = 2 * w_a
    o_k = o_q + w_b
    o_v = o_k + w_kv
    o_gb = o_v + w_kv
    o_gc = o_gb + w_c
    o_hc = o_gc + w_c
    n_heads = w_b // HEAD_DIM
    group = n_heads // KV_HEADS
    lane = lax.broadcasted_iota(jnp.int32, (1, LANES), 1)
    lo = lane < HEAD_DIM

    @pl.when(g == 0)
    def _():
        x = x_ref[...]
        h = _rms(x, g2_ref[...]).astype(BF16)
        proj_ref[...] = _dot(h, win_ref[...])
        tpos = lax.broadcasted_iota(jnp.int32, (rows_all, 1), 0) % t_new

        u = _gelu(proj_ref[:, 0:w_a])
        v = _head_layer_norm(_gelu(proj_ref[:, w_a:2 * w_a]), gsgu_ref[...])
        for c in range(w_a // LANES):
            vtmp_ref[c] = v[:, c * LANES:(c + 1) * LANES]
        for tt in range(t_new):
            for c in range(w_a // LANES):
                vsgu_ref[tt, c * LANES:(c + 1) * LANES, :] = (
                    vtmp_ref[c, pl.ds(tt, n_seq, stride=t_new), :].T)
        mix = bias_ref[...] + coef_ref[0] * v
        for dlt in range(1, t_new):
            mix = mix + coef_ref[dlt] * pltpu.roll(v, dlt, 0)
        ycat_ref[:, 0:w_a] = _rms(u * mix, gout_ref[:, 0:w_a]).astype(BF16)

        z = proj_ref[:, o_gc:o_gc + w_c] * proj_ref[:, o_hc:o_hc + w_c]
        z_ref[...] = z
        s2 = jnp.where(tpos >= 2, pltpu.roll(z, 2, 0), 0.0) + cexp2_ref[...]
        s1 = jnp.where(tpos >= 1, pltpu.roll(z, 1, 0), 0.0) + cexp1_ref[...]
        conv = s2 * wconv_ref[0:1, :] + s1 * wconv_ref[1:2, :] + z * wconv_ref[2:3, :]
        yc = proj_ref[:, o_gb:o_gb + w_c] * conv
        ycat_ref[:, w_a + w_b:] = _rms(yc, gout_ref[:, w_a + w_b:]).astype(BF16)

        for hd in range(n_heads):
            c, hf, kvh = hd // 2, hd % 2, hd // group
            piece = proj_ref[:, o_q + c * LANES:o_q + (c + 1) * LANES]
            if hf != kvh:
                piece = pltpu.roll(piece, HEAD_DIM, 1)
            qprep_ref[hd] = jnp.where(lo if kvh == 0 else ~lo, piece, 0.0)

        for c in range(rows_all // LANES):
            knt_ref[c] = proj_ref[c * LANES:(c + 1) * LANES, o_k:o_k + w_kv].T
            vnt_ref[c] = proj_ref[c * LANES:(c + 1) * LANES, o_v:o_v + w_kv].T

    pair_rows = 2 * t_new
    n_pairs = gsz // 2
    lrows = n_heads * pair_rows
    ri = lax.broadcasted_iota(jnp.int32, (lrows, 1), 0)
    r_in = ri % pair_rows
    seq_q = r_in // t_new
    t_q = r_in % t_new
    cj = lax.broadcasted_iota(jnp.int32, (1, 2 * WINDOW), 1)
    valid_c = (cj // WINDOW == seq_q) & (cj % WINDOW >= t_q + 1)
    dist_c = (WINDOW + t_q - cj % WINDOW).astype(F32)
    nj = lax.broadcasted_iota(jnp.int32, (1, LANES), 1)
    valid_n = (nj < pair_rows) & (nj // t_new == seq_q) & (nj % t_new <= t_q)
    dist_n = (t_q - nj % t_new).astype(F32)
    slope_col = jnp.zeros((lrows, 1), F32)
    for hd in range(n_heads):
        slope_col = jnp.where(ri // pair_rows == hd, slopes[hd], slope_col)
    sink = sinkcol_ref[...]
    bias_c = slope_col * dist_c
    bias_n = slope_col * dist_n
    zpad = jnp.zeros((LANES - pair_rows, w_kv), F32)

    r0 = pl.multiple_of(g * (n_pairs * pair_rows), n_pairs * pair_rows)
    for p in range(n_pairs):
        rows = pl.ds(r0 + p * pair_rows, pair_rows)
        lhs = jnp.concatenate([qprep_ref[hd, rows, :] for hd in range(n_heads)], axis=0).astype(BF16)
        kn = proj_ref[rows, o_k:o_k + w_kv]
        vn = proj_ref[rows, o_v:o_v + w_kv]
        kc = jnp.concatenate([kc_ref[2 * p], kc_ref[2 * p + 1]], axis=1)
        vc = jnp.concatenate([vc_ref[2 * p], vc_ref[2 * p + 1]], axis=1)
        s_c = _dot(lhs, kc.astype(BF16)) * SCALE - bias_c
        s_n = _dot_nt(lhs, jnp.concatenate([kn, zpad], axis=0).astype(BF16)) * SCALE - bias_n
        s_c = jnp.where(valid_c, s_c, NEG)
        s_n = jnp.where(valid_n, s_n, NEG)
        m = jnp.maximum(jnp.maximum(jnp.max(s_c, axis=-1, keepdims=True),
                                    jnp.max(s_n, axis=-1, keepdims=True)), sink)
        p_c = jnp.exp(s_c - m)
        p_n = jnp.exp(s_n - m)
        den = (jnp.sum(p_c, axis=-1, keepdims=True) + jnp.sum(p_n, axis=-1, keepdims=True)
               + jnp.exp(sink - m))
        inv = 1.0 / den
        o = (_dot_nt((p_c * inv).astype(BF16), vc.astype(BF16))
             + _dot((p_n * inv).astype(BF16), jnp.concatenate([vn, zpad], axis=0).astype(BF16)))
        cols = []
        for c in range(n_heads // 2):
            pieces = []
            for hf in range(2):
                hd = 2 * c + hf
                piece = o[hd * pair_rows:(hd + 1) * pair_rows]
                if hd // group != hf:
                    piece = pltpu.roll(piece, HEAD_DIM, 1)
                pieces.append(piece)
            cols.append(jnp.where(lo, pieces[0], pieces[1]))
        ybuf_ref[rows, :] = jnp.concatenate(cols, axis=1)

    keep = lane < WINDOW - t_new
    kn_t = knt_ref[g]
    vn_t = vnt_ref[g]
    for i in range(gsz):
        shift = (WINDOW - t_new - t_new * i) % LANES
        nk = pltpu.roll(kn_t, shift, 1) if shift else kn_t
        nv = pltpu.roll(vn_t, shift, 1) if shift else vn_t
        ko_ref[i] = jnp.where(keep, pltpu.roll(kc_ref[i], WINDOW - t_new, 1), nk)
        vo_ref[i] = jnp.where(keep, pltpu.roll(vc_ref[i], WINDOW - t_new, 1), nv)

    @pl.when(g == pl.num_programs(0) - 1)
    def _():
        ycat_ref[:, w_a:w_a + w_b] = _rms(ybuf_ref[...], gout_ref[:, w_a:w_a + w_b]).astype(BF16)
        y = _dot(ycat_ref[...], wout_ref[...])
        xo_ref[...] = x_ref[...] + _rms(y, g3_ref[...])


def _mixer_sample(x_all, layer, n_prompt_rows, n_seq, t_new, norm_g, win, wout, gout_all, coef_all,
                  bias_all, gsgu_all, wconv_all, cexp1_all, cexp2_all, sink_col_all, kc_all, vc_all,
                  w_a, w_b, w_kv, w_c):
    d = x_all.shape[1]
    m = n_seq * t_new
    assert n_prompt_rows % m == 0
    in_w = win.shape[2]
    n_heads = w_b // HEAD_DIM
    gsz = LANES // t_new
    assert n_seq % gsz == 0 and gsz % 2 == 0 and 2 * t_new == 8 and WINDOW == LANES
    ng = n_seq // gsz
    const2 = lambda g: (0, 0)
    const3 = lambda g: (0, 0, 0)
    layer3 = lambda g: (layer, 0, 0)
    layer4 = lambda g: (layer, 0, 0, 0)
    step_in = lambda g: (layer, g, 0, 0)
    step_out = lambda g: (g, 0, 0)
    kernel = functools.partial(_mixs_kernel, layer=layer, slopes=_alibi_slopes(n_heads),
                               t_new=t_new, w_a=w_a, w_b=w_b, w_kv=w_kv, w_c=w_c)
    coef_bytes = coef_all.size // coef_all.shape[0] * 4
    depth = win.shape[0]
    vmem = ((win.size + wout.size) // depth * 2 + 4 * m * d * 4 + m * in_w * 4
            + 8 * gsz * WINDOW * w_kv * 4 + coef_bytes * 2 + 24 * MIB)
    return pl.pallas_call(
        kernel,
        grid=(ng,),
        in_specs=[
            pl.BlockSpec((m, d), lambda g: (n_prompt_rows // m, 0)),
            pl.BlockSpec(norm_g.shape, const3),
            pl.BlockSpec((None,) + win.shape[1:], layer3, pipeline_mode=pl.Buffered(1)),
            pl.BlockSpec((None,) + wout.shape[1:], layer3, pipeline_mode=pl.Buffered(1)),
            pl.BlockSpec(gout_all.shape, const2),
            pl.BlockSpec((None,) + coef_all.shape[1:], layer4),
            pl.BlockSpec((None,) + bias_all.shape[1:], layer3),
            pl.BlockSpec(gsgu_all.shape, const2),
            pl.BlockSpec(wconv_all.shape, const3),
            pl.BlockSpec((None,) + cexp1_all.shape[1:], layer3),
            pl.BlockSpec((None,) + cexp2_all.shape[1:], layer3),
            pl.BlockSpec((None,) + sink_col_all.shape[1:], layer3),
            pl.BlockSpec((None, gsz, w_kv, WINDOW), step_in),
            pl.BlockSpec((None, gsz, w_kv, WINDOW), step_in),
        ],
        out_specs=[
            pl.BlockSpec((m, d), const2),
            pl.BlockSpec((t_new, w_a, n_seq), const3),
            pl.BlockSpec((m, w_c), const2),
            pl.BlockSpec((gsz, w_kv, WINDOW), step_out),
            pl.BlockSpec((gsz, w_kv, WINDOW), step_out),
        ],
        out_shape=[
            jax.ShapeDtypeStruct((m, d), F32),
            jax.ShapeDtypeStruct((t_new, w_a, n_seq), F32),
            jax.ShapeDtypeStruct((m, w_c), F32),
            jax.ShapeDtypeStruct((n_seq, w_kv, WINDOW), F32),
            jax.ShapeDtypeStruct((n_seq, w_kv, WINDOW), F32),
        ],
        scratch_shapes=[
            pltpu.VMEM((m, in_w), F32),
            pltpu.VMEM((n_heads, m, LANES), F32),
            pltpu.VMEM((m, w_b), F32),
            pltpu.VMEM((m, gout_all.shape[1]), BF16),
            pltpu.VMEM((ng, w_kv, LANES), F32),
            pltpu.VMEM((ng, w_kv, LANES), F32),
            pltpu.VMEM((w_a // LANES, m, LANES), F32),
        ],
        compiler_params=pltpu.CompilerParams(
            dimension_semantics=("arbitrary",), vmem_limit_bytes=vmem),
        name="mixer_sample",
    )(x_all, norm_g, win, wout, gout_all, coef_all, bias_all, gsgu_all, wconv_all, cexp1_all,
      cexp2_all, sink_col_all, kc_all, vc_all)


def kernel(x_prompt, x_sample, cache_swa_k, cache_swa_v, cache_conv, norm_g, w_ffn_gu, w_ffn_down,
           w_mix_in, w_mix_out, g_mix_out, w_sgu, b_sgu, g_sgu, attn_sinks, w_conv):
    batch, seq, d = x_prompt.shape
    n_seq, t_new, _ = x_sample.shape
    depth = norm_g.shape[0]
    w_a = g_sgu.shape[1]
    w_c = w_conv.shape[2]
    w_kv = KV_HEADS * HEAD_DIM
    w_b = g_mix_out.shape[1] - w_a - w_c
    n_heads_a = w_a // HEAD_DIM
    n_heads_b = w_b // HEAD_DIM
    assert t_new <= CHUNK and seq % CHUNK == 0

    xp = x_prompt.reshape(batch * seq, d)
    xs = x_sample.reshape(n_seq * t_new, d)
    to_dp = lambda c: jnp.transpose(c, (0, 1, 3, 4, 2)).reshape(depth, c.shape[1], w_kv, c.shape[2])
    from_dp = lambda c: jnp.transpose(
        c.reshape(depth, c.shape[1], KV_HEADS, HEAD_DIM, c.shape[3]), (0, 1, 4, 2, 3))
    kc_all = to_dp(cache_swa_k)
    vc_all = to_dp(cache_swa_v)

    win_all = w_mix_in.astype(BF16)
    wout_all = w_mix_out.astype(BF16)
    bsgu_full_all = jnp.repeat(jnp.swapaxes(b_sgu, 1, 2), HEAD_DIM, axis=2)
    w4 = jnp.tril(w_sgu[:, :, :t_new, :t_new])
    tt = np.arange(t_new)
    coef_all = jnp.stack([
        jnp.where((tt >= dlt)[None, None, :], w4[:, :, tt, np.maximum(tt - dlt, 0)], 0.0)
        for dlt in range(t_new)], axis=1)
    coef_all = jnp.repeat(coef_all.transpose(0, 1, 3, 2), HEAD_DIM, axis=3)
    coef_all = jnp.tile(coef_all, (1, 1, n_seq, 1))
    bias_all = jnp.tile(jnp.repeat(jnp.swapaxes(b_sgu[:, :, :t_new], 1, 2), HEAD_DIM, axis=2),
                        (1, n_seq, 1))
    pad_t = lambda c: jnp.pad(c, ((0, 0), (0, 0), (0, t_new - c.shape[2]), (0, 0))).reshape(
        depth, n_seq * t_new, w_c)
    cexp2_all = pad_t(cache_conv)
    cexp1_all = pad_t(cache_conv[:, :, 1:])
    sink_col_all = jnp.repeat(attn_sinks, 2 * t_new, axis=1)[:, :, None]

    n_prompt_rows = batch * seq
    outs = {k: [] for k in ("sgu", "kp", "vp", "ks", "vs", "cp", "cs")}
    x_in = (xp, xs)
    for l in range(depth):
        x_all = _ffn(x_in, norm_g, w_ffn_gu, w_ffn_down, l, 0, n_prompt_rows, False)
        xp, kp, vp, cp = _mixer_prompt(
            x_all, l, batch, seq, norm_g, win_all, wout_all, g_mix_out, w_sgu, bsgu_full_all,
            g_sgu, attn_sinks, w_conv, w_a, w_b, w_kv, w_c)
        xs, vsgu, z_s, ks, vs = _mixer_sample(
            x_all, l, n_prompt_rows, n_seq, t_new, norm_g, win_all, wout_all, g_mix_out,
            coef_all, bias_all, g_sgu, w_conv, cexp1_all, cexp2_all, sink_col_all, kc_all, vc_all,
            w_a, w_b, w_kv, w_c)
        last = l == depth - 1
        res = _ffn((xp, xs), norm_g, w_ffn_gu, w_ffn_down, l, 1, n_prompt_rows, last)
        if last:
            xp, xs = res[0], res[1][:n_seq * t_new]
        else:
            x_in = (res,)

        outs["sgu"].append(vsgu)
        outs["kp"].append(kp)
        outs["vp"].append(vp)
        outs["ks"].append(ks)
        outs["vs"].append(vs)
        outs["cp"].append(cp)
        outs["cs"].append(z_s.reshape(n_seq, t_new, w_c)[:, t_new - (CONV_W - 1):])
    sgu = jnp.transpose(
        jnp.stack(outs["sgu"]).reshape(depth, t_new, n_heads_a, HEAD_DIM, n_seq), (0, 4, 1, 2, 3))
    return (xp.reshape(batch, seq, d), xs.reshape(n_seq, t_new, d), sgu,
            from_dp(jnp.stack(outs["kp"])), from_dp(jnp.stack(outs["vp"])),
            from_dp(jnp.stack(outs["ks"])), from_dp(jnp.stack(outs["vs"])),
            jnp.stack(outs["cp"]), jnp.stack(outs["cs"]))
```

```python
import functools

import numpy as np
import jax
import jax.numpy as jnp
from jax import lax
from jax.experimental import pallas as pl
from jax.experimental.pallas import tpu as pltpu

F32 = jnp.float32
BF16 = jnp.bfloat16

HEAD_DIM = 64
KV_HEADS = 2
WINDOW = 128
CHUNK = 128
CONV_W = 3
EPS = 1e-6
NEG = -1e30
SCALE = HEAD_DIM ** -0.5
LOG2E = float(np.log2(np.e))
LANES = 128

FFN_TM = 512
FFN_TF = 256
MIX_T = 512
MIB = 1024 * 1024


def _rms(x, g):
    return x * lax.rsqrt(jnp.mean(x * x, axis=-1, keepdims=True) + EPS) * g


def _gelu(x):
    return 0.5 * x * (1.0 + lax.erf(x * np.float32(np.sqrt(0.5))))


def _head_layer_norm(x, g):
    lane = lax.broadcasted_iota(jnp.int32, (1, x.shape[1]), 1)
    out = jnp.zeros_like(x)
    for hh in range(x.shape[1] // HEAD_DIM):
        m = (lane >= hh * HEAD_DIM) & (lane < (hh + 1) * HEAD_DIM)
        mu = jnp.sum(jnp.where(m, x, 0.0), axis=-1, keepdims=True) / HEAD_DIM
        d = jnp.where(m, x - mu, 0.0)
        var = jnp.sum(d * d, axis=-1, keepdims=True) / HEAD_DIM
        out = out + d * lax.rsqrt(var + EPS)
    return out * g


def _dot(a, b):
    return jnp.dot(a, b, preferred_element_type=F32)


def _dot_nt(a, b):
    return lax.dot_general(a, b, (((1,), (1,)), ((), ())), preferred_element_type=F32)


def _sink_softmax(s, sink):
    m = jnp.maximum(jnp.max(s, axis=-1, keepdims=True), sink)
    p = jnp.exp(s - m)
    den = jnp.sum(p, axis=-1, keepdims=True) + jnp.exp(sink - m)
    return p * (1.0 / den)


def _alibi_slopes(n_heads):
    return [float(2.0 ** (-8.0 * h / n_heads)) for h in range(1, n_heads + 1)]


def _ffn_kernel(*refs, layer, half, n_prompt_tiles, split_in, split_out, d_ff, tf, n_stream):
    refs = list(refs)
    xp_ref = refs.pop(0)
    xs_ref = refs.pop(0) if split_in else None
    norm_ref, wgu_hbm, wdn_hbm = refs[:3]
    refs = refs[3:]
    op_ref = refs.pop(0)
    os_ref = refs.pop(0) if split_out else None
    wgu16_ref, wdn16_ref, stage_gu_ref, stage_dn_ref, sem_ref, act_ref = refs
    i = pl.program_id(0)
    cw = stage_gu_ref.shape[2]
    rw = stage_dn_ref.shape[1]

    def gu_copy(c, slot):
        return pltpu.make_async_copy(wgu_hbm.at[layer, half, :, pl.ds(c * cw, cw)],
                                     stage_gu_ref.at[slot], sem_ref.at[0, slot])

    def dn_copy(c, slot):
        return pltpu.make_async_copy(wdn_hbm.at[layer, half, pl.ds(c * rw, rw), :],
                                     stage_dn_ref.at[slot], sem_ref.at[1, slot])

    @pl.when(i == 0)
    def _():
        gu_copy(0, 0).start()
        dn_copy(0, 0).start()
        for c in range(n_stream):
            slot = c % 2
            if c + 1 < n_stream:
                gu_copy(c + 1, 1 - slot).start()
                dn_copy(c + 1, 1 - slot).start()
            gu_copy(c, slot).wait()
            dn_copy(c, slot).wait()
            wgu16_ref[:, c * cw:(c + 1) * cw] = stage_gu_ref[slot].astype(BF16)
            wdn16_ref[c * rw:(c + 1) * rw, :] = stage_dn_ref[slot].astype(BF16)

    if split_in and split_out:
        x = jnp.where(i == 0, xs_ref[...], xp_ref[...])
    elif split_in:
        x = jnp.where(i < n_prompt_tiles, xp_ref[...], xs_ref[...])
    else:
        x = xp_ref[...]
    g_pre = norm_ref[layer, 4 * half:4 * half + 1, :]
    g_post = norm_ref[layer, 4 * half + 1:4 * half + 2, :]
    n_chunks = d_ff // tf

    def hidden_chunks(h, rows, first, count):
        for c in range(first, first + count):
            gate = _dot(h, wgu16_ref[:, c * tf:(c + 1) * tf])
            up = _dot(h, wgu16_ref[:, d_ff + c * tf:d_ff + (c + 1) * tf])
            act_ref[rows, c * tf:(c + 1) * tf] = (jax.nn.silu(gate) * up).astype(BF16)

    def store(rows, val):
        op_ref[rows, :] = val
        if split_out:
            os_ref[rows, :] = val

    hr = x.shape[0] // 2
    ra, rb = slice(0, hr), slice(hr, 2 * hr)
    xa, xb = x[ra], x[rb]
    ha = _rms(xa, g_pre).astype(BF16)
    hidden_chunks(ha, ra, 0, 2)
    hb = _rms(xb, g_pre).astype(BF16)
    hidden_chunks(ha, ra, 2, n_chunks - 2)
    ya = _dot(act_ref[ra, :], wdn16_ref[...])
    hidden_chunks(hb, rb, 0, 3)
    store(ra, xa + 0.5 * _rms(ya, g_post))
    hidden_chunks(hb, rb, 3, n_chunks - 3)
    yb = _dot(act_ref[rb, :], wdn16_ref[...])
    store(rb, xb + 0.5 * _rms(yb, g_post))


def _ffn(xs_in, norm_g, w_gu, w_down, layer, half, n_prompt_rows, split_out):
    split_in = len(xs_in) == 2
    d = xs_in[0].shape[1]
    d_ff = w_down.shape[2]
    tm = FFN_TM
    n_sample_rows = xs_in[1].shape[0] if split_in else xs_in[0].shape[0] - n_prompt_rows
    assert n_prompt_rows % tm == 0 and n_sample_rows == tm and d_ff % FFN_TF == 0
    npt = n_prompt_rows // tm
    n_stream = d_ff // FFN_TF
    cw = 2 * d_ff // n_stream
    rw = d_ff // n_stream
    assert cw % LANES == 0 and rw % 8 == 0
    assert split_in or not split_out
    if split_out:
        prompt_tile = lambda i: (jnp.maximum(i - 1, 0), 0)
    else:
        prompt_tile = lambda i: (jnp.minimum(i, npt - 1), 0)
    const = lambda i: (0, 0)
    in_specs = []
    if split_in:
        in_specs += [pl.BlockSpec((tm, d), prompt_tile),
                     pl.BlockSpec((tm, d), const, pipeline_mode=pl.Buffered(1))]
    else:
        in_specs += [pl.BlockSpec((tm, d), lambda i: (i, 0))]
    in_specs += [pl.BlockSpec(norm_g.shape, lambda i: (0, 0, 0)),
                 pl.BlockSpec(memory_space=pl.ANY),
                 pl.BlockSpec(memory_space=pl.ANY)]
    if split_out:
        out_specs = [pl.BlockSpec((tm, d), prompt_tile),
                     pl.BlockSpec((tm, d), lambda i: (jnp.minimum(i, 1), 0))]
        out_shape = [jax.ShapeDtypeStruct((n_prompt_rows, d), F32),
                     jax.ShapeDtypeStruct((2 * n_sample_rows, d), F32)]
    else:
        out_specs = pl.BlockSpec((tm, d), lambda i: (i, 0))
        out_shape = jax.ShapeDtypeStruct((n_prompt_rows + n_sample_rows, d), F32)
    weights16 = 3 * d * d_ff * 2
    staging = 2 * (d * cw + rw * d) * 4
    vmem = weights16 + staging + 7 * tm * d * 4 + tm * d_ff * 2 + 10 * MIB
    return pl.pallas_call(
        functools.partial(_ffn_kernel, layer=layer, half=half, n_prompt_tiles=npt,
                          split_in=split_in, split_out=split_out, d_ff=d_ff, tf=FFN_TF,
                          n_stream=n_stream),
        grid=(npt + 1,),
        in_specs=in_specs,
        out_specs=out_specs,
        out_shape=out_shape,
        scratch_shapes=[
            pltpu.VMEM((d, 2 * d_ff), BF16),
            pltpu.VMEM((d_ff, d), BF16),
            pltpu.VMEM((2, d, cw), F32),
            pltpu.VMEM((2, rw, d), F32),
            pltpu.SemaphoreType.DMA((2, 2)),
            pltpu.VMEM((tm, d_ff), BF16),
        ],
        compiler_params=pltpu.CompilerParams(
            dimension_semantics=("arbitrary",), vmem_limit_bytes=vmem),
        name="ffn_half_step",
    )(*xs_in, norm_g, w_gu, w_down)


def _mixp_kernel(*refs, **statics):
    parity = lax.rem(pl.program_id(0), 2)

    @pl.when(parity == 0)
    def _():
        _mixp_body(*refs, parity=0, **statics)

    @pl.when(parity == 1)
    def _():
        _mixp_body(*refs, parity=1, **statics)


def _mixp_body(sinks_all_ref, xnext_ref, x_ref, norm_ref, win_ref, wout_ref, gout_all_ref, wsgu_ref,
               bsgu_ref, gsgu_all_ref, wconv_all_ref,
               xo_ref, ko_ref, vo_ref, co_ref,
               proj_even_ref, proj_odd_ref, kb_ref, krb_ref, vb_ref, vrb_ref, zbuf_ref, ycat_ref,
               abias_ref,
               *, parity, layer, tiles_per_seq, slopes, w_a, w_b, w_kv, w_c):
    proj_next_ref, proj_ref = ((proj_even_ref, proj_odd_ref) if parity == 0
                               else (proj_odd_ref, proj_even_ref))
    g2_ref = norm_ref.at[layer, pl.ds(2, 1)]
    g3_ref = norm_ref.at[layer, pl.ds(3, 1)]
    gout_ref = gout_all_ref.at[pl.ds(layer, 1)]
    gsgu_ref = gsgu_all_ref.at[pl.ds(layer, 1)]
    wconv_ref = wconv_all_ref.at[layer]
    step = pl.program_id(0)
    tile = jnp.maximum(step - 1, 0)
    j = lax.rem(tile, tiles_per_seq)
    t_rows = x_ref.shape[0]
    nblk = t_rows // WINDOW
    o_q = 2 * w_a
    o_k = o_q + w_b
    o_v = o_k + w_kv
    o_gb = o_v + w_kv
    o_gc = o_gb + w_c
    o_hc = o_gc + w_c

    @pl.when(step == 0)
    def _():
        proj_ref[...] = jnp.zeros(proj_ref.shape, F32)
        qi = lax.broadcasted_iota(jnp.int32, (WINDOW, 2 * WINDOW), 0)
        kk = lax.broadcasted_iota(jnp.int32, (WINDOW, 2 * WINDOW), 1)
        dist_i = WINDOW + qi - kk
        band = (dist_i >= 0) & (dist_i < WINDOW)
        dist = dist_i.astype(F32)
        for hd in range(len(slopes)):
            abias_ref[hd] = jnp.where(band, (slopes[hd] * LOG2E) * dist, -NEG)

    @pl.when(j == 0)
    def _():
        kb_ref[0:WINDOW, :] = jnp.zeros((WINDOW, w_kv), BF16)
        krb_ref[0:WINDOW, :] = jnp.zeros((WINDOW, w_kv), BF16)
        vb_ref[0:WINDOW, :] = jnp.zeros((WINDOW, w_kv), BF16)
        vrb_ref[0:WINDOW, :] = jnp.zeros((WINDOW, w_kv), BF16)
        zbuf_ref[0:8, :] = jnp.zeros((8, w_c), F32)

    hn = _rms(xnext_ref[...], g2_ref[...]).astype(BF16)
    n_pchunks = 8
    pcw = proj_next_ref.shape[1] // n_pchunks

    def project_chunks(first, count):
        for c in range(first, first + count):
            proj_next_ref[:, c * pcw:(c + 1) * pcw] = _dot(hn, win_ref[:, c * pcw:(c + 1) * pcw])

    x = x_ref[...]
    project_chunks(0, 1)

    u = _gelu(proj_ref[:, 0:w_a])
    v = _head_layer_norm(_gelu(proj_ref[:, w_a:2 * w_a]), gsgu_ref[...])
    lane_a = lax.broadcasted_iota(jnp.int32, (1, w_a), 1)
    ri = lax.broadcasted_iota(jnp.int32, (CHUNK, CHUNK), 0)
    ci = lax.broadcasted_iota(jnp.int32, (CHUNK, CHUNK), 1)
    n_heads_a = w_a // HEAD_DIM
    w_tril = [jnp.where(ri >= ci, wsgu_ref[hh], 0.0).astype(BF16) for hh in range(n_heads_a)]
    bias = bsgu_ref[...]
    ya_parts = []
    for n in range(nblk):
        vblk = v[n * CHUNK:(n + 1) * CHUNK]
        mix = bias
        for hh in range(n_heads_a):
            mh = (lane_a >= hh * HEAD_DIM) & (lane_a < (hh + 1) * HEAD_DIM)
            mix = mix + _dot(w_tril[hh], jnp.where(mh, vblk, 0.0).astype(BF16))
        ya_parts.append(u[n * CHUNK:(n + 1) * CHUNK] * mix)
    ya = jnp.concatenate(ya_parts, axis=0)
    ycat_ref[:, 0:w_a] = _rms(ya, gout_ref[:, 0:w_a]).astype(BF16)
    project_chunks(1, 1)

    lane = lax.broadcasted_iota(jnp.int32, (1, LANES), 1)
    lo = lane < HEAD_DIM
    k = proj_ref[:, o_k:o_k + w_kv]
    vv = proj_ref[:, o_v:o_v + w_kv]
    kb_ref[WINDOW:, :] = k.astype(BF16)
    krb_ref[WINDOW:, :] = pltpu.roll(k, HEAD_DIM, 1).astype(BF16)
    vb_ref[WINDOW:, :] = vv.astype(BF16)
    vrb_ref[WINDOW:, :] = pltpu.roll(vv, HEAD_DIM, 1).astype(BF16)

    kj = lax.broadcasted_iota(jnp.int32, (WINDOW, 2 * WINDOW), 1)
    has_prev = jnp.full((WINDOW, 2 * WINDOW), j, jnp.int32) > 0
    first_ok = (kj >= WINDOW) | has_prev

    n_q_cols = w_b // LANES
    half = n_q_cols // 2
    yb_parts = []
    for n in range(nblk):
        rows = slice(n * WINDOW, (n + 1) * WINDOW)
        qcols = [proj_ref[rows, o_q + c * LANES:o_q + (c + 1) * LANES] * (SCALE * LOG2E)
                 for c in range(n_q_cols)]
        keep_a = [lo if c < half else ~lo for c in range(n_q_cols)]
        lhs_a = jnp.concatenate([jnp.where(keep_a[c], qcols[c], 0.0) for c in range(n_q_cols)],
                                axis=0).astype(BF16)
        lhs_b = jnp.concatenate([jnp.where(keep_a[c], 0.0, qcols[c]) for c in range(n_q_cols)],
                                axis=0).astype(BF16)
        band_rows = slice(n * WINDOW, (n + 2) * WINDOW)
        s_a = _dot_nt(lhs_a, kb_ref[band_rows, :])
        s_b = _dot_nt(lhs_b, krb_ref[band_rows, :])
        project_chunks(2 + n, 1)
        p_a, p_b, inv_a, inv_b = [], [], [], []
        for c in range(n_q_cols):
            head_a = 2 * c if c < half else 2 * c + 1
            head_b = 2 * c + 1 if c < half else 2 * c
            for s_all, head, acc, inv in ((s_a, head_a, p_a, inv_a), (s_b, head_b, p_b, inv_b)):
                s = s_all[c * WINDOW:(c + 1) * WINDOW] - abias_ref[head]
                if n == 0:
                    s = jnp.where(first_ok, s, NEG)
                sink = sinks_all_ref[layer, head] * LOG2E
                m = jnp.maximum(jnp.max(s, axis=-1, keepdims=True), sink)
                p = jnp.exp2(s - m)
                den = jnp.sum(p, axis=-1, keepdims=True) + jnp.exp2(sink - m)
                acc.append(p.astype(BF16))
                inv.append(1.0 / den)
        o_a = _dot(jnp.concatenate(p_a, axis=0), vb_ref[band_rows, :])
        o_b = _dot(jnp.concatenate(p_b, axis=0), vrb_ref[band_rows, :])
        cols = []
        for c in range(n_q_cols):
            ra = o_a[c * WINDOW:(c + 1) * WINDOW] * inv_a[c]
            rb = o_b[c * WINDOW:(c + 1) * WINDOW] * inv_b[c]
            cols.append(jnp.where(lo, ra, rb) if c < half else jnp.where(lo, rb, ra))
        yb_parts.append(jnp.concatenate(cols, axis=1))
    yb = jnp.concatenate(yb_parts, axis=0)
    ycat_ref[:, w_a:w_a + w_b] = _rms(yb, gout_ref[:, w_a:w_a + w_b]).astype(BF16)
    kb_ref[0:WINDOW, :] = kb_ref[t_rows:t_rows + WINDOW, :]
    krb_ref[0:WINDOW, :] = krb_ref[t_rows:t_rows + WINDOW, :]
    vb_ref[0:WINDOW, :] = vb_ref[t_rows:t_rows + WINDOW, :]
    vrb_ref[0:WINDOW, :] = vrb_ref[t_rows:t_rows + WINDOW, :]

    project_chunks(2 + nblk, 1)

    z = proj_ref[:, o_gc:o_gc + w_c] * proj_ref[:, o_hc:o_hc + w_c]
    zbuf_ref[8:8 + t_rows, :] = z
    conv = zbuf_ref[8 - (CONV_W - 1):8 - (CONV_W - 1) + t_rows, :] * wconv_ref[0:1, :]
    for jj in range(1, CONV_W):
        off = 8 - (CONV_W - 1) + jj
        conv = conv + zbuf_ref[off:off + t_rows, :] * wconv_ref[jj:jj + 1, :]
    yc = proj_ref[:, o_gb:o_gb + w_c] * conv
    ycat_ref[:, w_a + w_b:] = _rms(yc, gout_ref[:, w_a + w_b:]).astype(BF16)
    z_tail = zbuf_ref[8 + t_rows - (CONV_W - 1):8 + t_rows, :]
    zbuf_ref[8 - (CONV_W - 1):8, :] = z_tail

    y = _dot(ycat_ref[...], wout_ref[...])
    project_chunks(3 + nblk, n_pchunks - 3 - nblk)
    xo_ref[...] = x + _rms(y, g3_ref[...])

    @pl.when((j == tiles_per_seq - 1) & (step > 0))
    def _():
        ko_ref[0] = proj_ref[t_rows - WINDOW:, o_k:o_k + w_kv].T
        vo_ref[0] = proj_ref[t_rows - WINDOW:, o_v:o_v + w_kv].T
        co_ref[0] = z_tail


def _mixer_prompt(x_all, layer, batch, seq, norm_g, win, wout, gout_all, wsgu_all, bsgu_full_all,
                  gsgu_all, sinks_all, wconv_all, w_a, w_b, w_kv, w_c):
    d = x_all.shape[1]
    m = batch * seq
    t = MIX_T
    assert seq % t == 0 and t % WINDOW == 0
    nt = seq // t
    in_w = win.shape[2]
    n_heads_b = w_b // HEAD_DIM
    n_tiles = batch * nt
    const2 = lambda s, *_: (0, 0)
    const3 = lambda s, *_: (0, 0, 0)
    layer3 = lambda s, *_: (layer, 0, 0)
    layer4 = lambda s, *_: (layer, 0, 0, 0)
    row_next = lambda s, *_: (jnp.minimum(s, n_tiles - 1), 0)
    row = lambda s, *_: (jnp.maximum(s - 1, 0), 0)
    per_b = lambda s, *_: (jnp.maximum(s - 1, 0) // nt, 0, 0)
    kernel = functools.partial(_mixp_kernel, layer=layer, tiles_per_seq=nt,
                               slopes=_alibi_slopes(n_heads_b),
                               w_a=w_a, w_b=w_b, w_kv=w_kv, w_c=w_c)
    grid_spec = pltpu.PrefetchScalarGridSpec(
        num_scalar_prefetch=1,
        grid=(n_tiles + 1,),
        in_specs=[
            pl.BlockSpec((t, d), row_next),
            pl.BlockSpec((t, d), row),
            pl.BlockSpec(norm_g.shape, const3),
            pl.BlockSpec((None,) + win.shape[1:], layer3, pipeline_mode=pl.Buffered(1)),
            pl.BlockSpec((None,) + wout.shape[1:], layer3, pipeline_mode=pl.Buffered(1)),
            pl.BlockSpec(gout_all.shape, const2),
            pl.BlockSpec((None,) + wsgu_all.shape[1:], layer4),
            pl.BlockSpec((None,) + bsgu_full_all.shape[1:], layer3),
            pl.BlockSpec(gsgu_all.shape, const2),
            pl.BlockSpec(wconv_all.shape, const3),
        ],
        out_specs=[
            pl.BlockSpec((t, d), row),
            pl.BlockSpec((1, w_kv, WINDOW), per_b),
            pl.BlockSpec((1, w_kv, WINDOW), per_b),
            pl.BlockSpec((1, CONV_W - 1, w_c), per_b),
        ],
        scratch_shapes=[
            pltpu.VMEM((t, in_w), F32),
            pltpu.VMEM((t, in_w), F32),
            pltpu.VMEM((t + WINDOW, w_kv), BF16),
            pltpu.VMEM((t + WINDOW, w_kv), BF16),
            pltpu.VMEM((t + WINDOW, w_kv), BF16),
            pltpu.VMEM((t + WINDOW, w_kv), BF16),
            pltpu.VMEM((t + 8, w_c), F32),
            pltpu.VMEM((t, gout_all.shape[1]), BF16),
            pltpu.VMEM((n_heads_b, WINDOW, 2 * WINDOW), F32),
        ],
    )
    depth = win.shape[0]
    vmem = (win.size + wout.size) // depth * 2 + 6 * t * d * 4 + 2 * t * in_w * 4 + 24 * MIB
    return pl.pallas_call(
        kernel,
        grid_spec=grid_spec,
        out_shape=[
            jax.ShapeDtypeStruct((m, d), F32),
            jax.ShapeDtypeStruct((batch, w_kv, WINDOW), F32),
            jax.ShapeDtypeStruct((batch, w_kv, WINDOW), F32),
            jax.ShapeDtypeStruct((batch, CONV_W - 1, w_c), F32),
        ],
        compiler_params=pltpu.CompilerParams(
            dimension_semantics=("arbitrary",), vmem_limit_bytes=vmem),
        name="mixer_prompt",
    )(sinks_all, x_all, x_all, norm_g, win, wout, gout_all, wsgu_all, bsgu_full_all, gsgu_all,
      wconv_all)


def _mixs_kernel(x_ref, norm_ref, win_ref, wout_ref, gout_all_ref, coef_ref, bias_ref, gsgu_all_ref,
                 wconv_all_ref, cexp1_ref, cexp2_ref, sinkcol_ref, kc_ref, vc_ref,
                 xo_ref, vsgu_ref, z_ref, ko_ref, vo_ref,
                 proj_ref, qprep_ref, ybuf_ref, ycat_ref, knt_ref, vnt_ref, vtmp_ref,
                 *, layer, slopes, t_new, w_a, w_b, w_kv, w_c):
    g2_ref = norm_ref.at[layer, pl.ds(2, 1)]
    g3_ref = norm_ref.at[layer, pl.ds(3, 1)]
    gout_ref = gout_all_ref.at[pl.ds(layer, 1)]
    gsgu_ref = gsgu_all_ref.at[pl.ds(layer, 1)]
    wconv_ref = wconv_all_ref.at[layer]
    g = pl.program_id(0)
    rows_all = x_ref.shape[0]
    n_seq = rows_all // t_new
    gsz = kc_ref.shape[0]
    o_q = 2 * w_a
    o_k = o_q + w_b
    o_v = o_k + w_kv
    o_gb = o_v + w_kv
    o_gc = o_gb + w_c
    o_hc = o_gc + w_c
    n_heads = w_b // HEAD_DIM
    group = n_heads // KV_HEADS
    lane = lax.broadcasted_iota(jnp.int32, (1, LANES), 1)
    lo = lane < HEAD_DIM

    @pl.when(g == 0)
    def _():
        x = x_ref[...]
        h = _rms(x, g2_ref[...]).astype(BF16)
        proj_ref[...] = _dot(h, win_ref[...])
        tpos = lax.broadcasted_iota(jnp.int32, (rows_all, 1), 0) % t_new

        u = _gelu(proj_ref[:, 0:w_a])
        v = _head_layer_norm(_gelu(proj_ref[:, w_a:2 * w_a]), gsgu_ref[...])
        for c in range(w_a // LANES):
            vtmp_ref[c] = v[:, c * LANES:(c + 1) * LANES]
        for tt in range(t_new):
            for c in range(w_a // LANES):
                vsgu_ref[tt, c * LANES:(c + 1) * LANES, :] = (
                    vtmp_ref[c, pl.ds(tt, n_seq, stride=t_new), :].T)
        mix = bias_ref[...] + coef_ref[0] * v
        for dlt in range(1, t_new):
            mix = mix + coef_ref[dlt] * pltpu.roll(v, dlt, 0)
        ycat_ref[:, 0:w_a] = _rms(u * mix, gout_ref[:, 0:w_a]).astype(BF16)

        z = proj_ref[:, o_gc:o_gc + w_c] * proj_ref[:, o_hc:o_hc + w_c]
        z_ref[...] = z
        s2 = jnp.where(tpos >= 2, pltpu.roll(z, 2, 0), 0.0) + cexp2_ref[...]
        s1 = jnp.where(tpos >= 1, pltpu.roll(z, 1, 0), 0.0) + cexp1_ref[...]
        conv = s2 * wconv_ref[0:1, :] + s1 * wconv_ref[1:2, :] + z * wconv_ref[2:3, :]
        yc = proj_ref[:, o_gb:o_gb + w_c] * conv
        ycat_ref[:, w_a + w_b:] = _rms(yc, gout_ref[:, w_a + w_b:]).astype(BF16)

        for hd in range(n_heads):
            c, hf, kvh = hd // 2, hd % 2, hd // group
            piece = proj_ref[:, o_q + c * LANES:o_q + (c + 1) * LANES]
            if hf != kvh:
                piece = pltpu.roll(piece, HEAD_DIM, 1)
            qprep_ref[hd] = jnp.where(lo if kvh == 0 else ~lo, piece * (SCALE * LOG2E), 0.0)

        for c in range(rows_all // LANES):
            knt_ref[c] = proj_ref[c * LANES:(c + 1) * LANES, o_k:o_k + w_kv].T
            vnt_ref[c] = proj_ref[c * LANES:(c + 1) * LANES, o_v:o_v + w_kv].T

    pair_rows = 2 * t_new
    n_pairs = gsz // 2
    lrows = n_heads * pair_rows
    ri = lax.broadcasted_iota(jnp.int32, (lrows, 1), 0)
    r_in = ri % pair_rows
    seq_q = r_in // t_new
    t_q = r_in % t_new
    cj = lax.broadcasted_iota(jnp.int32, (1, 2 * WINDOW), 1)
    valid_c = (cj // WINDOW == seq_q) & (cj % WINDOW >= t_q + 1)
    dist_c = (WINDOW + t_q - cj % WINDOW).astype(F32)
    nj = lax.broadcasted_iota(jnp.int32, (1, LANES), 1)
    valid_n = (nj < pair_rows) & (nj // t_new == seq_q) & (nj % t_new <= t_q)
    dist_n = (t_q - nj % t_new).astype(F32)
    slope_col = jnp.zeros((lrows, 1), F32)
    for hd in range(n_heads):
        slope_col = jnp.where(ri // pair_rows == hd, slopes[hd], slope_col)
    sink = sinkcol_ref[...] * LOG2E
    bias_c = jnp.where(valid_c, (slope_col * LOG2E) * dist_c, -NEG)
    bias_n = jnp.where(valid_n, (slope_col * LOG2E) * dist_n, -NEG)
    zpad = jnp.zeros((LANES - pair_rows, w_kv), F32)

    keep = lane < WINDOW - t_new
    kn_t = knt_ref[g]
    vn_t = vnt_ref[g]

    def slide(src_ref, dst_ref, new_t, i):
        shift = (WINDOW - t_new - t_new * i) % LANES
        new = pltpu.roll(new_t, shift, 1) if shift else new_t
        dst_ref[i] = jnp.where(keep, pltpu.roll(src_ref[i], WINDOW - t_new, 1), new)

    r0 = pl.multiple_of(g * (n_pairs * pair_rows), n_pairs * pair_rows)
    sc_parts, sn_parts = [], []
    for p in range(n_pairs):
        rows = pl.ds(r0 + p * pair_rows, pair_rows)
        lhs = jnp.concatenate([qprep_ref[hd, rows, :] for hd in range(n_heads)], axis=0).astype(BF16)
        kn = proj_ref[rows, o_k:o_k + w_kv]
        kc = jnp.concatenate([kc_ref[2 * p], kc_ref[2 * p + 1]], axis=1)
        sc_parts.append(_dot(lhs, kc.astype(BF16)))
        sn_parts.append(_dot_nt(lhs, jnp.concatenate([kn, zpad], axis=0).astype(BF16)))
        slide(kc_ref, ko_ref, kn_t, 2 * p)
        slide(kc_ref, ko_ref, kn_t, 2 * p + 1)
    s_c = jnp.stack(sc_parts) - bias_c[None]
    s_n = jnp.stack(sn_parts) - bias_n[None]
    m = jnp.maximum(jnp.maximum(jnp.max(s_c, axis=-1, keepdims=True),
                                jnp.max(s_n, axis=-1, keepdims=True)), sink[None])
    p_c = jnp.exp2(s_c - m)
    p_n = jnp.exp2(s_n - m)
    den = (jnp.sum(p_c, axis=-1, keepdims=True) + jnp.sum(p_n, axis=-1, keepdims=True)
           + jnp.exp2(sink[None] - m))
    inv = 1.0 / den
    p_c = p_c.astype(BF16)
    p_n = p_n.astype(BF16)
    for p in range(n_pairs):
        rows = pl.ds(r0 + p * pair_rows, pair_rows)
        vn = proj_ref[rows, o_v:o_v + w_kv]
        vc = jnp.concatenate([vc_ref[2 * p], vc_ref[2 * p + 1]], axis=1)
        slide(vc_ref, vo_ref, vn_t, 2 * p)
        slide(vc_ref, vo_ref, vn_t, 2 * p + 1)
        o = (_dot_nt(p_c[p], vc.astype(BF16))
             + _dot(p_n[p], jnp.concatenate([vn, zpad], axis=0).astype(BF16))) * inv[p]
        cols = []
        for c in range(n_heads // 2):
            pieces = []
            for hf in range(2):
                hd = 2 * c + hf
                piece = o[hd * pair_rows:(hd + 1) * pair_rows]
                if hd // group != hf:
                    piece = pltpu.roll(piece, HEAD_DIM, 1)
                pieces.append(piece)
            cols.append(jnp.where(lo, pieces[0], pieces[1]))
        ybuf_ref[rows, :] = jnp.concatenate(cols, axis=1)

    @pl.when(g == pl.num_programs(0) - 1)
    def _():
        ycat_ref[:, w_a:w_a + w_b] = _rms(ybuf_ref[...], gout_ref[:, w_a:w_a + w_b]).astype(BF16)
        y = _dot(ycat_ref[...], wout_ref[...])
        xo_ref[...] = x_ref[...] + _rms(y, g3_ref[...])


def _mixer_sample(x_all, layer, n_prompt_rows, n_seq, t_new, norm_g, win, wout, gout_all, coef_all,
                  bias_all, gsgu_all, wconv_all, cexp1_all, cexp2_all, sink_col_all, kc_all, vc_all,
                  w_a, w_b, w_kv, w_c):
    d = x_all.shape[1]
    m = n_seq * t_new
    assert n_prompt_rows % m == 0
    in_w = win.shape[2]
    n_heads = w_b // HEAD_DIM
    gsz = LANES // t_new
    assert n_seq % gsz == 0 and gsz % 2 == 0 and 2 * t_new == 8 and WINDOW == LANES
    ng = n_seq // gsz
    const2 = lambda g: (0, 0)
    const3 = lambda g: (0, 0, 0)
    layer3 = lambda g: (layer, 0, 0)
    layer4 = lambda g: (layer, 0, 0, 0)
    step_in = lambda g: (layer, g, 0, 0)
    step_out = lambda g: (g, 0, 0)
    kernel = functools.partial(_mixs_kernel, layer=layer, slopes=_alibi_slopes(n_heads),
                               t_new=t_new, w_a=w_a, w_b=w_b, w_kv=w_kv, w_c=w_c)
    coef_bytes = coef_all.size // coef_all.shape[0] * 4
    depth = win.shape[0]
    vmem = ((win.size + wout.size) // depth * 2 + 4 * m * d * 4 + m * in_w * 4
            + 8 * gsz * WINDOW * w_kv * 4 + coef_bytes * 2 + 24 * MIB)
    return pl.pallas_call(
        kernel,
        grid=(ng,),
        in_specs=[
            pl.BlockSpec((m, d), lambda g: (n_prompt_rows // m, 0)),
            pl.BlockSpec(norm_g.shape, const3),
            pl.BlockSpec((None,) + win.shape[1:], layer3, pipeline_mode=pl.Buffered(1)),
            pl.BlockSpec((None,) + wout.shape[1:], layer3, pipeline_mode=pl.Buffered(1)),
            pl.BlockSpec(gout_all.shape, const2),
            pl.BlockSpec((None,) + coef_all.shape[1:], layer4),
            pl.BlockSpec((None,) + bias_all.shape[1:], layer3),
            pl.BlockSpec(gsgu_all.shape, const2),
            pl.BlockSpec(wconv_all.shape, const3),
            pl.BlockSpec((None,) + cexp1_all.shape[1:], layer3),
            pl.BlockSpec((None,) + cexp2_all.shape[1:], layer3),
            pl.BlockSpec((None,) + sink_col_all.shape[1:], layer3),
            pl.BlockSpec((None, gsz, w_kv, WINDOW), step_in),
            pl.BlockSpec((None, gsz, w_kv, WINDOW), step_in),
        ],
        out_specs=[
            pl.BlockSpec((m, d), const2),
            pl.BlockSpec((t_new, w_a, n_seq), const3),
            pl.BlockSpec((m, w_c), const2),
            pl.BlockSpec((gsz, w_kv, WINDOW), step_out),
            pl.BlockSpec((gsz, w_kv, WINDOW), step_out),
        ],
        out_shape=[
            jax.ShapeDtypeStruct((m, d), F32),
            jax.ShapeDtypeStruct((t_new, w_a, n_seq), F32),
            jax.ShapeDtypeStruct((m, w_c), F32),
            jax.ShapeDtypeStruct((n_seq, w_kv, WINDOW), F32),
            jax.ShapeDtypeStruct((n_seq, w_kv, WINDOW), F32),
        ],
        scratch_shapes=[
            pltpu.VMEM((m, in_w), F32),
            pltpu.VMEM((n_heads, m, LANES), F32),
            pltpu.VMEM((m, w_b), F32),
            pltpu.VMEM((m, gout_all.shape[1]), BF16),
            pltpu.VMEM((ng, w_kv, LANES), F32),
            pltpu.VMEM((ng, w_kv, LANES), F32),
            pltpu.VMEM((w_a // LANES, m, LANES), F32),
        ],
        compiler_params=pltpu.CompilerParams(
            dimension_semantics=("arbitrary",), vmem_limit_bytes=vmem),
        name="mixer_sample",
    )(x_all, norm_g, win, wout, gout_all, coef_all, bias_all, gsgu_all, wconv_all, cexp1_all,
      cexp2_all, sink_col_all, kc_all, vc_all)


def kernel(x_prompt, x_sample, cache_swa_k, cache_swa_v, cache_conv, norm_g, w_ffn_gu, w_ffn_down,
           w_mix_in, w_mix_out, g_mix_out, w_sgu, b_sgu, g_sgu, attn_sinks, w_conv):
    batch, seq, d = x_prompt.shape
    n_seq, t_new, _ = x_sample.shape
    depth = norm_g.shape[0]
    w_a = g_sgu.shape[1]
    w_c = w_conv.shape[2]
    w_kv = KV_HEADS * HEAD_DIM
    w_b = g_mix_out.shape[1] - w_a - w_c
    n_heads_a = w_a // HEAD_DIM
    n_heads_b = w_b // HEAD_DIM
    assert t_new <= CHUNK and seq % CHUNK == 0

    xp = x_prompt.reshape(batch * seq, d)
    xs = x_sample.reshape(n_seq * t_new, d)
    to_dp = lambda c: jnp.transpose(c, (0, 1, 3, 4, 2)).reshape(depth, c.shape[1], w_kv, c.shape[2])
    from_dp = lambda c: jnp.transpose(
        c.reshape(depth, c.shape[1], KV_HEADS, HEAD_DIM, c.shape[3]), (0, 1, 4, 2, 3))
    kc_all = to_dp(cache_swa_k)
    vc_all = to_dp(cache_swa_v)

    win_all = w_mix_in.astype(BF16)
    wout_all = w_mix_out.astype(BF16)
    bsgu_full_all = jnp.repeat(jnp.swapaxes(b_sgu, 1, 2), HEAD_DIM, axis=2)
    w4 = jnp.tril(w_sgu[:, :, :t_new, :t_new])
    tt = np.arange(t_new)
    coef_all = jnp.stack([
        jnp.where((tt >= dlt)[None, None, :], w4[:, :, tt, np.maximum(tt - dlt, 0)], 0.0)
        for dlt in range(t_new)], axis=1)
    coef_all = jnp.repeat(coef_all.transpose(0, 1, 3, 2), HEAD_DIM, axis=3)
    coef_all = jnp.tile(coef_all, (1, 1, n_seq, 1))
    bias_all = jnp.tile(jnp.repeat(jnp.swapaxes(b_sgu[:, :, :t_new], 1, 2), HEAD_DIM, axis=2),
                        (1, n_seq, 1))
    pad_t = lambda c: jnp.pad(c, ((0, 0), (0, 0), (0, t_new - c.shape[2]), (0, 0))).reshape(
        depth, n_seq * t_new, w_c)
    cexp2_all = pad_t(cache_conv)
    cexp1_all = pad_t(cache_conv[:, :, 1:])
    sink_col_all = jnp.repeat(attn_sinks, 2 * t_new, axis=1)[:, :, None]

    n_prompt_rows = batch * seq
    outs = {k: [] for k in ("sgu", "kp", "vp", "ks", "vs", "cp", "cs")}
    x_in = (xp, xs)
    for l in range(depth):
        x_all = _ffn(x_in, norm_g, w_ffn_gu, w_ffn_down, l, 0, n_prompt_rows, False)
        xp, kp, vp, cp = _mixer_prompt(
            x_all, l, batch, seq, norm_g, win_all, wout_all, g_mix_out, w_sgu, bsgu_full_all,
            g_sgu, attn_sinks, w_conv, w_a, w_b, w_kv, w_c)
        xs, vsgu, z_s, ks, vs = _mixer_sample(
            x_all, l, n_prompt_rows, n_seq, t_new, norm_g, win_all, wout_all, g_mix_out,
            coef_all, bias_all, g_sgu, w_conv, cexp1_all, cexp2_all, sink_col_all, kc_all, vc_all,
            w_a, w_b, w_kv, w_c)
        last = l == depth - 1
        res = _ffn((xp, xs), norm_g, w_ffn_gu, w_ffn_down, l, 1, n_prompt_rows, last)
        if last:
            xp, xs = res[0], res[1][:n_seq * t_new]
        else:
            x_in = (res,)

        outs["sgu"].append(vsgu)
        outs["kp"].append(kp)
        outs["vp"].append(vp)
        outs["ks"].append(ks)
        outs["vs"].append(vs)
        outs["cp"].append(cp)
        outs["cs"].append(z_s.reshape(n_seq, t_new, w_c)[:, t_new - (CONV_W - 1):])
    sgu = jnp.transpose(
        jnp.stack(outs["sgu"]).reshape(depth, t_new, n_heads_a, HEAD_DIM, n_seq), (0, 4, 1, 2, 3))
    return (xp.reshape(batch, seq, d), xs.reshape(n_seq, t_new, d), sgu,
            from_dp(jnp.stack(outs["kp"])), from_dp(jnp.stack(outs["vp"])),
            from_dp(jnp.stack(outs["ks"])), from_dp(jnp.stack(outs["vs"])),
            jnp.stack(outs["cp"]), jnp.stack(outs["cs"]))
```

```python
import functools

import numpy as np
import jax
import jax.numpy as jnp
from jax import lax
from jax.experimental import pallas as pl
from jax.experimental.pallas import tpu as pltpu

F32 = jnp.float32
BF16 = jnp.bfloat16

HEAD_DIM = 64
KV_HEADS = 2
WINDOW = 128
CHUNK = 128
CONV_W = 3
EPS = 1e-6
NEG = -1e30
SCALE = HEAD_DIM ** -0.5
LOG2E = float(np.log2(np.e))
LANES = 128

FFN_TM = 512
FFN_TF = 256
MIX_T = 512
MIB = 1024 * 1024


def _rms(x, g):
    return x * lax.rsqrt(jnp.mean(x * x, axis=-1, keepdims=True) + EPS) * g


def _gelu(x):
    return 0.5 * x * (1.0 + lax.erf(x * np.float32(np.sqrt(0.5))))


def _head_layer_norm(x, g):
    lane = lax.broadcasted_iota(jnp.int32, (1, x.shape[1]), 1)
    out = jnp.zeros_like(x)
    for hh in range(x.shape[1] // HEAD_DIM):
        m = (lane >= hh * HEAD_DIM) & (lane < (hh + 1) * HEAD_DIM)
        mu = jnp.sum(jnp.where(m, x, 0.0), axis=-1, keepdims=True) / HEAD_DIM
        d = jnp.where(m, x - mu, 0.0)
        var = jnp.sum(d * d, axis=-1, keepdims=True) / HEAD_DIM
        out = out + d * lax.rsqrt(var + EPS)
    return out * g


def _dot(a, b):
    return jnp.dot(a, b, preferred_element_type=F32)


def _dot_nt(a, b):
    return lax.dot_general(a, b, (((1,), (1,)), ((), ())), preferred_element_type=F32)


def _sink_softmax(s, sink):
    m = jnp.maximum(jnp.max(s, axis=-1, keepdims=True), sink)
    p = jnp.exp(s - m)
    den = jnp.sum(p, axis=-1, keepdims=True) + jnp.exp(sink - m)
    return p * (1.0 / den)


def _alibi_slopes(n_heads):
    return [float(2.0 ** (-8.0 * h / n_heads)) for h in range(1, n_heads + 1)]


def _ffn_kernel(*refs, layer, half, n_prompt_tiles, split_in, split_out, d_ff, tf):
    refs = list(refs)
    xp_ref = refs.pop(0)
    xs_ref = refs.pop(0) if split_in else None
    norm_ref, wgu_hbm, wdn_hbm = refs[:3]
    refs = refs[3:]
    op_ref = refs.pop(0)
    os_ref = refs.pop(0) if split_out else None
    wgu16_ref, wdn16_ref, stage_gu_ref, stage_dn_ref, sem_ref, act_ref = refs
    i = pl.program_id(0)
    n_chunks = d_ff // tf

    def chunk_copies(c, slot):
        return [
            pltpu.make_async_copy(wgu_hbm.at[layer, half, :, pl.ds(c * tf, tf)],
                                  stage_gu_ref.at[slot, 0], sem_ref.at[slot, 0]),
            pltpu.make_async_copy(wgu_hbm.at[layer, half, :, pl.ds(d_ff + c * tf, tf)],
                                  stage_gu_ref.at[slot, 1], sem_ref.at[slot, 1]),
            pltpu.make_async_copy(wdn_hbm.at[layer, half, pl.ds(c * tf, tf), :],
                                  stage_dn_ref.at[slot], sem_ref.at[slot, 2]),
        ]

    def fetch(c):
        for cp in chunk_copies(c, c % 2):
            cp.start()

    def land(c):
        slot = c % 2
        for cp in chunk_copies(c, slot):
            cp.wait()
        wgu16_ref[:, c * tf:(c + 1) * tf] = stage_gu_ref[slot, 0].astype(BF16)
        wgu16_ref[:, d_ff + c * tf:d_ff + (c + 1) * tf] = stage_gu_ref[slot, 1].astype(BF16)
        wdn16_ref[c * tf:(c + 1) * tf, :] = stage_dn_ref[slot].astype(BF16)

    if split_in and split_out:
        x = jnp.where(i == 0, xs_ref[...], xp_ref[...])
    elif split_in:
        x = jnp.where(i < n_prompt_tiles, xp_ref[...], xs_ref[...])
    else:
        x = xp_ref[...]
    g_pre = norm_ref[layer, 4 * half:4 * half + 1, :]
    g_post = norm_ref[layer, 4 * half + 1:4 * half + 2, :]

    def hidden_chunks(h, rows, first, count):
        for c in range(first, first + count):
            gate = _dot(h, wgu16_ref[:, c * tf:(c + 1) * tf])
            up = _dot(h, wgu16_ref[:, d_ff + c * tf:d_ff + (c + 1) * tf])
            act_ref[rows, c * tf:(c + 1) * tf] = (jax.nn.silu(gate) * up).astype(BF16)

    def store(rows, val):
        op_ref[rows, :] = val
        if split_out:
            os_ref[rows, :] = val

    hr = x.shape[0] // 2
    ra, rb = slice(0, hr), slice(hr, 2 * hr)
    xa, xb = x[ra], x[rb]

    @pl.when(i == 0)
    def _():
        fetch(0)
        fetch(1)
        ha = _rms(xa, g_pre).astype(BF16)
        hb = _rms(xb, g_pre).astype(BF16)
        for c in range(n_chunks):
            land(c)
            if c + 2 < n_chunks:
                fetch(c + 2)
            hidden_chunks(ha, ra, c, 1)
            hidden_chunks(hb, rb, c, 1)
        ya = _dot(act_ref[ra, :], wdn16_ref[...])
        store(ra, xa + 0.5 * _rms(ya, g_post))
        yb = _dot(act_ref[rb, :], wdn16_ref[...])
        store(rb, xb + 0.5 * _rms(yb, g_post))

    @pl.when(i > 0)
    def _():
        ha = _rms(xa, g_pre).astype(BF16)
        hidden_chunks(ha, ra, 0, 2)
        hb = _rms(xb, g_pre).astype(BF16)
        hidden_chunks(ha, ra, 2, n_chunks - 2)
        ya = _dot(act_ref[ra, :], wdn16_ref[...])
        hidden_chunks(hb, rb, 0, 3)
        store(ra, xa + 0.5 * _rms(ya, g_post))
        hidden_chunks(hb, rb, 3, n_chunks - 3)
        yb = _dot(act_ref[rb, :], wdn16_ref[...])
        store(rb, xb + 0.5 * _rms(yb, g_post))


def _ffn(xs_in, norm_g, w_gu, w_down, layer, half, n_prompt_rows, split_out):
    split_in = len(xs_in) == 2
    d = xs_in[0].shape[1]
    d_ff = w_down.shape[2]
    tm = FFN_TM
    n_sample_rows = xs_in[1].shape[0] if split_in else xs_in[0].shape[0] - n_prompt_rows
    assert n_prompt_rows % tm == 0 and n_sample_rows == tm and d_ff % FFN_TF == 0
    npt = n_prompt_rows // tm
    tf = FFN_TF
    assert tf % LANES == 0
    assert split_in or not split_out
    if split_out:
        prompt_tile = lambda i: (jnp.maximum(i - 1, 0), 0)
    else:
        prompt_tile = lambda i: (jnp.minimum(i, npt - 1), 0)
    const = lambda i: (0, 0)
    in_specs = []
    if split_in:
        in_specs += [pl.BlockSpec((tm, d), prompt_tile),
                     pl.BlockSpec((tm, d), const, pipeline_mode=pl.Buffered(1))]
    else:
        in_specs += [pl.BlockSpec((tm, d), lambda i: (i, 0))]
    in_specs += [pl.BlockSpec(norm_g.shape, lambda i: (0, 0, 0)),
                 pl.BlockSpec(memory_space=pl.ANY),
                 pl.BlockSpec(memory_space=pl.ANY)]
    if split_out:
        out_specs = [pl.BlockSpec((tm, d), prompt_tile),
                     pl.BlockSpec((tm, d), lambda i: (jnp.minimum(i, 1), 0))]
        out_shape = [jax.ShapeDtypeStruct((n_prompt_rows, d), F32),
                     jax.ShapeDtypeStruct((2 * n_sample_rows, d), F32)]
    else:
        out_specs = pl.BlockSpec((tm, d), lambda i: (i, 0))
        out_shape = jax.ShapeDtypeStruct((n_prompt_rows + n_sample_rows, d), F32)
    weights16 = 3 * d * d_ff * 2
    staging = 2 * 3 * d * tf * 4
    vmem = weights16 + staging + 7 * tm * d * 4 + tm * d_ff * 2 + 10 * MIB
    return pl.pallas_call(
        functools.partial(_ffn_kernel, layer=layer, half=half, n_prompt_tiles=npt,
                          split_in=split_in, split_out=split_out, d_ff=d_ff, tf=tf),
        grid=(npt + 1,),
        in_specs=in_specs,
        out_specs=out_specs,
        out_shape=out_shape,
        scratch_shapes=[
            pltpu.VMEM((d, 2 * d_ff), BF16),
            pltpu.VMEM((d_ff, d), BF16),
            pltpu.VMEM((2, 2, d, tf), F32),
            pltpu.VMEM((2, tf, d), F32),
            pltpu.SemaphoreType.DMA((2, 3)),
            pltpu.VMEM((tm, d_ff), BF16),
        ],
        compiler_params=pltpu.CompilerParams(
            dimension_semantics=("arbitrary",), vmem_limit_bytes=vmem),
        name="ffn_half_step",
    )(*xs_in, norm_g, w_gu, w_down)


def _mixp_kernel(*refs, **statics):
    parity = lax.rem(pl.program_id(0), 2)

    @pl.when(parity == 0)
    def _():
        _mixp_body(*refs, parity=0, **statics)

    @pl.when(parity == 1)
    def _():
        _mixp_body(*refs, parity=1, **statics)


def _mixp_body(sinks_all_ref, xnext_ref, x_ref, norm_ref, win_ref, wout_ref, gout_all_ref, wsgu_ref,
               bsgu_ref, gsgu_all_ref, wconv_all_ref,
               xo_ref, ko_ref, vo_ref, co_ref,
               proj_even_ref, proj_odd_ref, kb_ref, krb_ref, vb_ref, vrb_ref, zbuf_ref, ycat_ref,
               abias_ref,
               *, parity, layer, tiles_per_seq, slopes, w_a, w_b, w_kv, w_c):
    proj_next_ref, proj_ref = ((proj_even_ref, proj_odd_ref) if parity == 0
                               else (proj_odd_ref, proj_even_ref))
    g2_ref = norm_ref.at[layer, pl.ds(2, 1)]
    g3_ref = norm_ref.at[layer, pl.ds(3, 1)]
    gout_ref = gout_all_ref.at[pl.ds(layer, 1)]
    gsgu_ref = gsgu_all_ref.at[pl.ds(layer, 1)]
    wconv_ref = wconv_all_ref.at[layer]
    step = pl.program_id(0)
    tile = jnp.maximum(step - 1, 0)
    j = lax.rem(tile, tiles_per_seq)
    t_rows = x_ref.shape[0]
    nblk = t_rows // WINDOW
    o_q = 2 * w_a
    o_k = o_q + w_b
    o_v = o_k + w_kv
    o_gb = o_v + w_kv
    o_gc = o_gb + w_c
    o_hc = o_gc + w_c

    @pl.when(step == 0)
    def _():
        proj_ref[...] = jnp.zeros(proj_ref.shape, F32)
        qi = lax.broadcasted_iota(jnp.int32, (WINDOW, 2 * WINDOW), 0)
        kk = lax.broadcasted_iota(jnp.int32, (WINDOW, 2 * WINDOW), 1)
        dist_i = WINDOW + qi - kk
        band = (dist_i >= 0) & (dist_i < WINDOW)
        dist = dist_i.astype(F32)
        for hd in range(len(slopes)):
            abias_ref[hd] = jnp.where(band, (slopes[hd] * LOG2E) * dist, -NEG)

    @pl.when(j == 0)
    def _():
        kb_ref[0:WINDOW, :] = jnp.zeros((WINDOW, w_kv), BF16)
        krb_ref[0:WINDOW, :] = jnp.zeros((WINDOW, w_kv), BF16)
        vb_ref[0:WINDOW, :] = jnp.zeros((WINDOW, w_kv), BF16)
        vrb_ref[0:WINDOW, :] = jnp.zeros((WINDOW, w_kv), BF16)
        zbuf_ref[0:8, :] = jnp.zeros((8, w_c), F32)

    hn = _rms(xnext_ref[...], g2_ref[...]).astype(BF16)
    n_pchunks = 8
    pcw = proj_next_ref.shape[1] // n_pchunks

    def project_chunks(first, count):
        for c in range(first, first + count):
            proj_next_ref[:, c * pcw:(c + 1) * pcw] = _dot(hn, win_ref[:, c * pcw:(c + 1) * pcw])

    x = x_ref[...]
    project_chunks(0, 1)

    u = _gelu(proj_ref[:, 0:w_a])
    v = _head_layer_norm(_gelu(proj_ref[:, w_a:2 * w_a]), gsgu_ref[...])
    lane_a = lax.broadcasted_iota(jnp.int32, (1, w_a), 1)
    ri = lax.broadcasted_iota(jnp.int32, (CHUNK, CHUNK), 0)
    ci = lax.broadcasted_iota(jnp.int32, (CHUNK, CHUNK), 1)
    n_heads_a = w_a // HEAD_DIM
    w_tril = [jnp.where(ri >= ci, wsgu_ref[hh], 0.0).astype(BF16) for hh in range(n_heads_a)]
    bias = bsgu_ref[...]
    ya_parts = []
    for n in range(nblk):
        vblk = v[n * CHUNK:(n + 1) * CHUNK]
        mix = bias
        for hh in range(n_heads_a):
            mh = (lane_a >= hh * HEAD_DIM) & (lane_a < (hh + 1) * HEAD_DIM)
            mix = mix + _dot(w_tril[hh], jnp.where(mh, vblk, 0.0).astype(BF16))
        ya_parts.append(u[n * CHUNK:(n + 1) * CHUNK] * mix)
    ya = jnp.concatenate(ya_parts, axis=0)
    ycat_ref[:, 0:w_a] = _rms(ya, gout_ref[:, 0:w_a]).astype(BF16)
    project_chunks(1, 1)

    lane = lax.broadcasted_iota(jnp.int32, (1, LANES), 1)
    lo = lane < HEAD_DIM
    k = proj_ref[:, o_k:o_k + w_kv]
    vv = proj_ref[:, o_v:o_v + w_kv]
    kb_ref[WINDOW:, :] = k.astype(BF16)
    krb_ref[WINDOW:, :] = pltpu.roll(k, HEAD_DIM, 1).astype(BF16)
    vb_ref[WINDOW:, :] = vv.astype(BF16)
    vrb_ref[WINDOW:, :] = pltpu.roll(vv, HEAD_DIM, 1).astype(BF16)

    kj = lax.broadcasted_iota(jnp.int32, (WINDOW, 2 * WINDOW), 1)
    has_prev = jnp.full((WINDOW, 2 * WINDOW), j, jnp.int32) > 0
    first_ok = (kj >= WINDOW) | has_prev

    n_q_cols = w_b // LANES
    half = n_q_cols // 2
    yb_parts = []
    for n in range(nblk):
        rows = slice(n * WINDOW, (n + 1) * WINDOW)
        qcols = [proj_ref[rows, o_q + c * LANES:o_q + (c + 1) * LANES] * (SCALE * LOG2E)
                 for c in range(n_q_cols)]
        keep_a = [lo if c < half else ~lo for c in range(n_q_cols)]
        lhs_a = jnp.concatenate([jnp.where(keep_a[c], qcols[c], 0.0) for c in range(n_q_cols)],
                                axis=0).astype(BF16)
        lhs_b = jnp.concatenate([jnp.where(keep_a[c], 0.0, qcols[c]) for c in range(n_q_cols)],
                                axis=0).astype(BF16)
        band_rows = slice(n * WINDOW, (n + 2) * WINDOW)
        s_a = _dot_nt(lhs_a, kb_ref[band_rows, :])
        s_b = _dot_nt(lhs_b, krb_ref[band_rows, :])
        project_chunks(2 + n, 1)
        p_a, p_b, inv_a, inv_b = [], [], [], []
        for c in range(n_q_cols):
            head_a = 2 * c if c < half else 2 * c + 1
            head_b = 2 * c + 1 if c < half else 2 * c
            for s_all, head, acc, inv in ((s_a, head_a, p_a, inv_a), (s_b, head_b, p_b, inv_b)):
                s = s_all[c * WINDOW:(c + 1) * WINDOW] - abias_ref[head]
                if n == 0:
                    s = jnp.where(first_ok, s, NEG)
                sink = sinks_all_ref[layer, head] * LOG2E
                m = jnp.maximum(jnp.max(s, axis=-1, keepdims=True), sink)
                p = jnp.exp2(s - m)
                den = jnp.sum(p, axis=-1, keepdims=True) + jnp.exp2(sink - m)
                acc.append(p.astype(BF16))
                inv.append(1.0 / den)
        o_a = _dot(jnp.concatenate(p_a, axis=0), vb_ref[band_rows, :])
        o_b = _dot(jnp.concatenate(p_b, axis=0), vrb_ref[band_rows, :])
        cols = []
        for c in range(n_q_cols):
            ra = o_a[c * WINDOW:(c + 1) * WINDOW] * inv_a[c]
            rb = o_b[c * WINDOW:(c + 1) * WINDOW] * inv_b[c]
            cols.append(jnp.where(lo, ra, rb) if c < half else jnp.where(lo, rb, ra))
        yb_parts.append(jnp.concatenate(cols, axis=1))
    yb = jnp.concatenate(yb_parts, axis=0)
    ycat_ref[:, w_a:w_a + w_b] = _rms(yb, gout_ref[:, w_a:w_a + w_b]).astype(BF16)
    kb_ref[0:WINDOW, :] = kb_ref[t_rows:t_rows + WINDOW, :]
    krb_ref[0:WINDOW, :] = krb_ref[t_rows:t_rows + WINDOW, :]
    vb_ref[0:WINDOW, :] = vb_ref[t_rows:t_rows + WINDOW, :]
    vrb_ref[0:WINDOW, :] = vrb_ref[t_rows:t_rows + WINDOW, :]

    project_chunks(2 + nblk, 1)

    z = proj_ref[:, o_gc:o_gc + w_c] * proj_ref[:, o_hc:o_hc + w_c]
    zbuf_ref[8:8 + t_rows, :] = z
    conv = zbuf_ref[8 - (CONV_W - 1):8 - (CONV_W - 1) + t_rows, :] * wconv_ref[0:1, :]
    for jj in range(1, CONV_W):
        off = 8 - (CONV_W - 1) + jj
        conv = conv + zbuf_ref[off:off + t_rows, :] * wconv_ref[jj:jj + 1, :]
    yc = proj_ref[:, o_gb:o_gb + w_c] * conv
    ycat_ref[:, w_a + w_b:] = _rms(yc, gout_ref[:, w_a + w_b:]).astype(BF16)
    z_tail = zbuf_ref[8 + t_rows - (CONV_W - 1):8 + t_rows, :]
    zbuf_ref[8 - (CONV_W - 1):8, :] = z_tail

    y = _dot(ycat_ref[...], wout_ref[...])
    project_chunks(3 + nblk, n_pchunks - 3 - nblk)
    xo_ref[...] = x + _rms(y, g3_ref[...])

    @pl.when((j == tiles_per_seq - 1) & (step > 0))
    def _():
        ko_ref[0] = proj_ref[t_rows - WINDOW:, o_k:o_k + w_kv].T
        vo_ref[0] = proj_ref[t_rows - WINDOW:, o_v:o_v + w_kv].T
        co_ref[0] = z_tail


def _mixer_prompt(x_all, layer, batch, seq, norm_g, win, wout, gout_all, wsgu_all, bsgu_full_all,
                  gsgu_all, sinks_all, wconv_all, w_a, w_b, w_kv, w_c):
    d = x_all.shape[1]
    m = batch * seq
    t = MIX_T
    assert seq % t == 0 and t % WINDOW == 0
    nt = seq // t
    in_w = win.shape[2]
    n_heads_b = w_b // HEAD_DIM
    n_tiles = batch * nt
    const2 = lambda s, *_: (0, 0)
    const3 = lambda s, *_: (0, 0, 0)
    layer3 = lambda s, *_: (layer, 0, 0)
    layer4 = lambda s, *_: (layer, 0, 0, 0)
    row_next = lambda s, *_: (jnp.minimum(s, n_tiles - 1), 0)
    row = lambda s, *_: (jnp.maximum(s - 1, 0), 0)
    per_b = lambda s, *_: (jnp.maximum(s - 1, 0) // nt, 0, 0)
    kernel = functools.partial(_mixp_kernel, layer=layer, tiles_per_seq=nt,
                               slopes=_alibi_slopes(n_heads_b),
                               w_a=w_a, w_b=w_b, w_kv=w_kv, w_c=w_c)
    grid_spec = pltpu.PrefetchScalarGridSpec(
        num_scalar_prefetch=1,
        grid=(n_tiles + 1,),
        in_specs=[
            pl.BlockSpec((t, d), row_next),
            pl.BlockSpec((t, d), row),
            pl.BlockSpec(norm_g.shape, const3),
            pl.BlockSpec((None,) + win.shape[1:], layer3, pipeline_mode=pl.Buffered(1)),
            pl.BlockSpec((None,) + wout.shape[1:], layer3, pipeline_mode=pl.Buffered(1)),
            pl.BlockSpec(gout_all.shape, const2),
            pl.BlockSpec((None,) + wsgu_all.shape[1:], layer4),
            pl.BlockSpec((None,) + bsgu_full_all.shape[1:], layer3),
            pl.BlockSpec(gsgu_all.shape, const2),
            pl.BlockSpec(wconv_all.shape, const3),
        ],
        out_specs=[
            pl.BlockSpec((t, d), row),
            pl.BlockSpec((1, w_kv, WINDOW), per_b),
            pl.BlockSpec((1, w_kv, WINDOW), per_b),
            pl.BlockSpec((1, CONV_W - 1, w_c), per_b),
        ],
        scratch_shapes=[
            pltpu.VMEM((t, in_w), F32),
            pltpu.VMEM((t, in_w), F32),
            pltpu.VMEM((t + WINDOW, w_kv), BF16),
            pltpu.VMEM((t + WINDOW, w_kv), BF16),
            pltpu.VMEM((t + WINDOW, w_kv), BF16),
            pltpu.VMEM((t + WINDOW, w_kv), BF16),
            pltpu.VMEM((t + 8, w_c), F32),
            pltpu.VMEM((t, gout_all.shape[1]), BF16),
            pltpu.VMEM((n_heads_b, WINDOW, 2 * WINDOW), F32),
        ],
    )
    depth = win.shape[0]
    vmem = (win.size + wout.size) // depth * 2 + 6 * t * d * 4 + 2 * t * in_w * 4 + 24 * MIB
    return pl.pallas_call(
        kernel,
        grid_spec=grid_spec,
        out_shape=[
            jax.ShapeDtypeStruct((m, d), F32),
            jax.ShapeDtypeStruct((batch, w_kv, WINDOW), F32),
            jax.ShapeDtypeStruct((batch, w_kv, WINDOW), F32),
            jax.ShapeDtypeStruct((batch, CONV_W - 1, w_c), F32),
        ],
        compiler_params=pltpu.CompilerParams(
            dimension_semantics=("arbitrary",), vmem_limit_bytes=vmem),
        name="mixer_prompt",
    )(sinks_all, x_all, x_all, norm_g, win, wout, gout_all, wsgu_all, bsgu_full_all, gsgu_all,
      wconv_all)


def _mixs_kernel(x_ref, norm_ref, win_ref, wout_ref, gout_all_ref, coef_ref, bias_ref, gsgu_all_ref,
                 wconv_all_ref, cexp1_ref, cexp2_ref, sinkcol_ref, kc_ref, vc_ref,
                 xo_ref, vsgu_ref, z_ref, ko_ref, vo_ref,
                 proj_ref, qprep_ref, ybuf_ref, ycat_ref, knt_ref, vnt_ref, vtmp_ref,
                 *, layer, slopes, t_new, w_a, w_b, w_kv, w_c):
    g2_ref = norm_ref.at[layer, pl.ds(2, 1)]
    g3_ref = norm_ref.at[layer, pl.ds(3, 1)]
    gout_ref = gout_all_ref.at[pl.ds(layer, 1)]
    gsgu_ref = gsgu_all_ref.at[pl.ds(layer, 1)]
    wconv_ref = wconv_all_ref.at[layer]
    g = pl.program_id(0)
    rows_all = x_ref.shape[0]
    n_seq = rows_all // t_new
    gsz = kc_ref.shape[0]
    o_q = 2 * w_a
    o_k = o_q + w_b
    o_v = o_k + w_kv
    o_gb = o_v + w_kv
    o_gc = o_gb + w_c
    o_hc = o_gc + w_c
    n_heads = w_b // HEAD_DIM
    group = n_heads // KV_HEADS
    lane = lax.broadcasted_iota(jnp.int32, (1, LANES), 1)
    lo = lane < HEAD_DIM

    @pl.when(g == 0)
    def _():
        x = x_ref[...]
        h = _rms(x, g2_ref[...]).astype(BF16)
        proj_ref[...] = _dot(h, win_ref[...])
        tpos = lax.broadcasted_iota(jnp.int32, (rows_all, 1), 0) % t_new

        u = _gelu(proj_ref[:, 0:w_a])
        v = _head_layer_norm(_gelu(proj_ref[:, w_a:2 * w_a]), gsgu_ref[...])
        for c in range(w_a // LANES):
            vtmp_ref[c] = v[:, c * LANES:(c + 1) * LANES]
        for tt in range(t_new):
            for c in range(w_a // LANES):
                vsgu_ref[tt, c * LANES:(c + 1) * LANES, :] = (
                    vtmp_ref[c, pl.ds(tt, n_seq, stride=t_new), :].T)
        mix = bias_ref[...] + coef_ref[0] * v
        for dlt in range(1, t_new):
            mix = mix + coef_ref[dlt] * pltpu.roll(v, dlt, 0)
        ycat_ref[:, 0:w_a] = _rms(u * mix, gout_ref[:, 0:w_a]).astype(BF16)

        z = proj_ref[:, o_gc:o_gc + w_c] * proj_ref[:, o_hc:o_hc + w_c]
        z_ref[...] = z
        s2 = jnp.where(tpos >= 2, pltpu.roll(z, 2, 0), 0.0) + cexp2_ref[...]
        s1 = jnp.where(tpos >= 1, pltpu.roll(z, 1, 0), 0.0) + cexp1_ref[...]
        conv = s2 * wconv_ref[0:1, :] + s1 * wconv_ref[1:2, :] + z * wconv_ref[2:3, :]
        yc = proj_ref[:, o_gb:o_gb + w_c] * conv
        ycat_ref[:, w_a + w_b:] = _rms(yc, gout_ref[:, w_a + w_b:]).astype(BF16)

        for hd in range(n_heads):
            c, hf, kvh = hd // 2, hd % 2, hd // group
            piece = proj_ref[:, o_q + c * LANES:o_q + (c + 1) * LANES]
            if hf != kvh:
                piece = pltpu.roll(piece, HEAD_DIM, 1)
            qprep_ref[hd] = jnp.where(lo if kvh == 0 else ~lo, piece * (SCALE * LOG2E), 0.0)

        for c in range(rows_all // LANES):
            knt_ref[c] = proj_ref[c * LANES:(c + 1) * LANES, o_k:o_k + w_kv].T
            vnt_ref[c] = proj_ref[c * LANES:(c + 1) * LANES, o_v:o_v + w_kv].T

    pair_rows = 2 * t_new
    n_pairs = gsz // 2
    lrows = n_heads * pair_rows
    ri = lax.broadcasted_iota(jnp.int32, (lrows, 1), 0)
    r_in = ri % pair_rows
    seq_q = r_in // t_new
    t_q = r_in % t_new
    cj = lax.broadcasted_iota(jnp.int32, (1, 2 * WINDOW), 1)
    valid_c = (cj // WINDOW == seq_q) & (cj % WINDOW >= t_q + 1)
    dist_c = (WINDOW + t_q - cj % WINDOW).astype(F32)
    nj = lax.broadcasted_iota(jnp.int32, (1, LANES), 1)
    valid_n = (nj < pair_rows) & (nj // t_new == seq_q) & (nj % t_new <= t_q)
    dist_n = (t_q - nj % t_new).astype(F32)
    slope_col = jnp.zeros((lrows, 1), F32)
    for hd in range(n_heads):
        slope_col = jnp.where(ri // pair_rows == hd, slopes[hd], slope_col)
    sink = sinkcol_ref[...] * LOG2E
    bias_c = jnp.where(valid_c, (slope_col * LOG2E) * dist_c, -NEG)
    bias_n = jnp.where(valid_n, (slope_col * LOG2E) * dist_n, -NEG)
    zpad = jnp.zeros((LANES - pair_rows, w_kv), F32)

    keep = lane < WINDOW - t_new
    kn_t = knt_ref[g]
    vn_t = vnt_ref[g]

    def slide(src_ref, dst_ref, new_t, i):
        shift = (WINDOW - t_new - t_new * i) % LANES
        new = pltpu.roll(new_t, shift, 1) if shift else new_t
        dst_ref[i] = jnp.where(keep, pltpu.roll(src_ref[i], WINDOW - t_new, 1), new)

    r0 = pl.multiple_of(g * (n_pairs * pair_rows), n_pairs * pair_rows)
    sc_parts, sn_parts = [], []
    for p in range(n_pairs):
        rows = pl.ds(r0 + p * pair_rows, pair_rows)
        lhs = jnp.concatenate([qprep_ref[hd, rows, :] for hd in range(n_heads)], axis=0).astype(BF16)
        kn = proj_ref[rows, o_k:o_k + w_kv]
        kc = jnp.concatenate([kc_ref[2 * p], kc_ref[2 * p + 1]], axis=1)
        sc_parts.append(_dot(lhs, kc.astype(BF16)))
        sn_parts.append(_dot_nt(lhs, jnp.concatenate([kn, zpad], axis=0).astype(BF16)))
        slide(kc_ref, ko_ref, kn_t, 2 * p)
        slide(kc_ref, ko_ref, kn_t, 2 * p + 1)
    s_c = jnp.stack(sc_parts) - bias_c[None]
    s_n = jnp.stack(sn_parts) - bias_n[None]
    m = jnp.maximum(jnp.maximum(jnp.max(s_c, axis=-1, keepdims=True),
                                jnp.max(s_n, axis=-1, keepdims=True)), sink[None])
    p_c = jnp.exp2(s_c - m)
    p_n = jnp.exp2(s_n - m)
    den = (jnp.sum(p_c, axis=-1, keepdims=True) + jnp.sum(p_n, axis=-1, keepdims=True)
           + jnp.exp2(sink[None] - m))
    inv = 1.0 / den
    p_c = p_c.astype(BF16)
    p_n = p_n.astype(BF16)
    for p in range(n_pairs):
        rows = pl.ds(r0 + p * pair_rows, pair_rows)
        vn = proj_ref[rows, o_v:o_v + w_kv]
        vc = jnp.concatenate([vc_ref[2 * p], vc_ref[2 * p + 1]], axis=1)
        slide(vc_ref, vo_ref, vn_t, 2 * p)
        slide(vc_ref, vo_ref, vn_t, 2 * p + 1)
        o = (_dot_nt(p_c[p], vc.astype(BF16))
             + _dot(p_n[p], jnp.concatenate([vn, zpad], axis=0).astype(BF16))) * inv[p]
        cols = []
        for c in range(n_heads // 2):
            pieces = []
            for hf in range(2):
                hd = 2 * c + hf
                piece = o[hd * pair_rows:(hd + 1) * pair_rows]
                if hd // group != hf:
                    piece = pltpu.roll(piece, HEAD_DIM, 1)
                pieces.append(piece)
            cols.append(jnp.where(lo, pieces[0], pieces[1]))
        ybuf_ref[rows, :] = jnp.concatenate(cols, axis=1)

    @pl.when(g == pl.num_programs(0) - 1)
    def _():
        ycat_ref[:, w_a:w_a + w_b] = _rms(ybuf_ref[...], gout_ref[:, w_a:w_a + w_b]).astype(BF16)
        y = _dot(ycat_ref[...], wout_ref[...])
        xo_ref[...] = x_ref[...] + _rms(y, g3_ref[...])


def _mixer_sample(x_all, layer, n_prompt_rows, n_seq, t_new, norm_g, win, wout, gout_all, coef_all,
                  bias_all, gsgu_all, wconv_all, cexp1_all, cexp2_all, sink_col_all, kc_all, vc_all,
                  w_a, w_b, w_kv, w_c):
    d = x_all.shape[1]
    m = n_seq * t_new
    assert n_prompt_rows % m == 0
    in_w = win.shape[2]
    n_heads = w_b // HEAD_DIM
    gsz = LANES // t_new
    assert n_seq % gsz == 0 and gsz % 2 == 0 and 2 * t_new == 8 and WINDOW == LANES
    ng = n_seq // gsz
    const2 = lambda g: (0, 0)
    const3 = lambda g: (0, 0, 0)
    layer3 = lambda g: (layer, 0, 0)
    layer4 = lambda g: (layer, 0, 0, 0)
    step_in = lambda g: (layer, g, 0, 0)
    step_out = lambda g: (g, 0, 0)
    kernel = functools.partial(_mixs_kernel, layer=layer, slopes=_alibi_slopes(n_heads),
                               t_new=t_new, w_a=w_a, w_b=w_b, w_kv=w_kv, w_c=w_c)
    coef_bytes = coef_all.size // coef_all.shape[0] * 4
    depth = win.shape[0]
    vmem = ((win.size + wout.size) // depth * 2 + 4 * m * d * 4 + m * in_w * 4
            + 8 * gsz * WINDOW * w_kv * 4 + coef_bytes * 2 + 24 * MIB)
    return pl.pallas_call(
        kernel,
        grid=(ng,),
        in_specs=[
            pl.BlockSpec((m, d), lambda g: (n_prompt_rows // m, 0)),
            pl.BlockSpec(norm_g.shape, const3),
            pl.BlockSpec((None,) + win.shape[1:], layer3, pipeline_mode=pl.Buffered(1)),
            pl.BlockSpec((None,) + wout.shape[1:], layer3, pipeline_mode=pl.Buffered(1)),
            pl.BlockSpec(gout_all.shape, const2),
            pl.BlockSpec((None,) + coef_all.shape[1:], layer4),
            pl.BlockSpec((None,) + bias_all.shape[1:], layer3),
            pl.BlockSpec(gsgu_all.shape, const2),
            pl.BlockSpec(wconv_all.shape, const3),
            pl.BlockSpec((None,) + cexp1_all.shape[1:], layer3),
            pl.BlockSpec((None,) + cexp2_all.shape[1:], layer3),
            pl.BlockSpec((None,) + sink_col_all.shape[1:], layer3),
            pl.BlockSpec((None, gsz, w_kv, WINDOW), step_in),
            pl.BlockSpec((None, gsz, w_kv, WINDOW), step_in),
        ],
        out_specs=[
            pl.BlockSpec((m, d), const2),
            pl.BlockSpec((t_new, w_a, n_seq), const3),
            pl.BlockSpec((m, w_c), const2),
            pl.BlockSpec((gsz, w_kv, WINDOW), step_out),
            pl.BlockSpec((gsz, w_kv, WINDOW), step_out),
        ],
        out_shape=[
            jax.ShapeDtypeStruct((m, d), F32),
            jax.ShapeDtypeStruct((t_new, w_a, n_seq), F32),
            jax.ShapeDtypeStruct((m, w_c), F32),
            jax.ShapeDtypeStruct((n_seq, w_kv, WINDOW), F32),
            jax.ShapeDtypeStruct((n_seq, w_kv, WINDOW), F32),
        ],
        scratch_shapes=[
            pltpu.VMEM((m, in_w), F32),
            pltpu.VMEM((n_heads, m, LANES), F32),
            pltpu.VMEM((m, w_b), F32),
            pltpu.VMEM((m, gout_all.shape[1]), BF16),
            pltpu.VMEM((ng, w_kv, LANES), F32),
            pltpu.VMEM((ng, w_kv, LANES), F32),
            pltpu.VMEM((w_a // LANES, m, LANES), F32),
        ],
        compiler_params=pltpu.CompilerParams(
            dimension_semantics=("arbitrary",), vmem_limit_bytes=vmem),
        name="mixer_sample",
    )(x_all, norm_g, win, wout, gout_all, coef_all, bias_all, gsgu_all, wconv_all, cexp1_all,
      cexp2_all, sink_col_all, kc_all, vc_all)


def kernel(x_prompt, x_sample, cache_swa_k, cache_swa_v, cache_conv, norm_g, w_ffn_gu, w_ffn_down,
           w_mix_in, w_mix_out, g_mix_out, w_sgu, b_sgu, g_sgu, attn_sinks, w_conv):
    batch, seq, d = x_prompt.shape
    n_seq, t_new, _ = x_sample.shape
    depth = norm_g.shape[0]
    w_a = g_sgu.shape[1]
    w_c = w_conv.shape[2]
    w_kv = KV_HEADS * HEAD_DIM
    w_b = g_mix_out.shape[1] - w_a - w_c
    n_heads_a = w_a // HEAD_DIM
    n_heads_b = w_b // HEAD_DIM
    assert t_new <= CHUNK and seq % CHUNK == 0

    xp = x_prompt.reshape(batch * seq, d)
    xs = x_sample.reshape(n_seq * t_new, d)
    to_dp = lambda c: jnp.transpose(c, (0, 1, 3, 4, 2)).reshape(depth, c.shape[1], w_kv, c.shape[2])
    from_dp = lambda c: jnp.transpose(
        c.reshape(depth, c.shape[1], KV_HEADS, HEAD_DIM, c.shape[3]), (0, 1, 4, 2, 3))
    kc_all = to_dp(cache_swa_k)
    vc_all = to_dp(cache_swa_v)

    win_all = w_mix_in.astype(BF16)
    wout_all = w_mix_out.astype(BF16)
    bsgu_full_all = jnp.repeat(jnp.swapaxes(b_sgu, 1, 2), HEAD_DIM, axis=2)
    w4 = jnp.tril(w_sgu[:, :, :t_new, :t_new])
    tt = np.arange(t_new)
    coef_all = jnp.stack([
        jnp.where((tt >= dlt)[None, None, :], w4[:, :, tt, np.maximum(tt - dlt, 0)], 0.0)
        for dlt in range(t_new)], axis=1)
    coef_all = jnp.repeat(coef_all.transpose(0, 1, 3, 2), HEAD_DIM, axis=3)
    coef_all = jnp.tile(coef_all, (1, 1, n_seq, 1))
    bias_all = jnp.tile(jnp.repeat(jnp.swapaxes(b_sgu[:, :, :t_new], 1, 2), HEAD_DIM, axis=2),
                        (1, n_seq, 1))
    pad_t = lambda c: jnp.pad(c, ((0, 0), (0, 0), (0, t_new - c.shape[2]), (0, 0))).reshape(
        depth, n_seq * t_new, w_c)
    cexp2_all = pad_t(cache_conv)
    cexp1_all = pad_t(cache_conv[:, :, 1:])
    sink_col_all = jnp.repeat(attn_sinks, 2 * t_new, axis=1)[:, :, None]

    n_prompt_rows = batch * seq
    outs = {k: [] for k in ("sgu", "kp", "vp", "ks", "vs", "cp", "cs")}
    x_in = (xp, xs)
    for l in range(depth):
        x_all = _ffn(x_in, norm_g, w_ffn_gu, w_ffn_down, l, 0, n_prompt_rows, False)
        xp, kp, vp, cp = _mixer_prompt(
            x_all, l, batch, seq, norm_g, win_all, wout_all, g_mix_out, w_sgu, bsgu_full_all,
            g_sgu, attn_sinks, w_conv, w_a, w_b, w_kv, w_c)
        xs, vsgu, z_s, ks, vs = _mixer_sample(
            x_all, l, n_prompt_rows, n_seq, t_new, norm_g, win_all, wout_all, g_mix_out,
            coef_all, bias_all, g_sgu, w_conv, cexp1_all, cexp2_all, sink_col_all, kc_all, vc_all,
            w_a, w_b, w_kv, w_c)
        last = l == depth - 1
        res = _ffn((xp, xs), norm_g, w_ffn_gu, w_ffn_down, l, 1, n_prompt_rows, last)
        if last:
            xp, xs = res[0], res[1][:n_seq * t_new]
        else:
            x_in = (res,)

        outs["sgu"].append(vsgu)
        outs["kp"].append(kp)
        outs["vp"].append(vp)
        outs["ks"].append(ks)
        outs["vs"].append(vs)
        outs["cp"].append(cp)
        outs["cs"].append(z_s.reshape(n_seq, t_new, w_c)[:, t_new - (CONV_W - 1):])
    sgu = jnp.transpose(
        jnp.stack(outs["sgu"]).reshape(depth, t_new, n_heads_a, HEAD_DIM, n_seq), (0, 4, 1, 2, 3))
    return (xp.reshape(batch, seq, d), xs.reshape(n_seq, t_new, d), sgu,
            from_dp(jnp.stack(outs["kp"])), from_dp(jnp.stack(outs["vp"])),
            from_dp(jnp.stack(outs["ks"])), from_dp(jnp.stack(outs["vs"])),
            jnp.stack(outs["cp"]), jnp.stack(outs["cs"]))
```

```python
import functools

import numpy as np
import jax
import jax.numpy as jnp
from jax import lax
from jax.experimental import pallas as pl
from jax.experimental.pallas import tpu as pltpu

F32 = jnp.float32
BF16 = jnp.bfloat16

HEAD_DIM = 64
KV_HEADS = 2
WINDOW = 128
CHUNK = 128
CONV_W = 3
EPS = 1e-6
NEG = -1e30
SCALE = HEAD_DIM ** -0.5
LOG2E = float(np.log2(np.e))
LANES = 128

FFN_TM = 512
FFN_TF = 256
FFN_SUBTILES = 2
MIX_T = 512
MIB = 1024 * 1024


def _rms(x, g):
    return x * lax.rsqrt(jnp.mean(x * x, axis=-1, keepdims=True) + EPS) * g


def _gelu(x):
    return 0.5 * x * (1.0 + lax.erf(x * np.float32(np.sqrt(0.5))))


def _head_layer_norm(x, g):
    lane = lax.broadcasted_iota(jnp.int32, (1, x.shape[1]), 1)
    out = jnp.zeros_like(x)
    for hh in range(x.shape[1] // HEAD_DIM):
        m = (lane >= hh * HEAD_DIM) & (lane < (hh + 1) * HEAD_DIM)
        mu = jnp.sum(jnp.where(m, x, 0.0), axis=-1, keepdims=True) / HEAD_DIM
        d = jnp.where(m, x - mu, 0.0)
        var = jnp.sum(d * d, axis=-1, keepdims=True) / HEAD_DIM
        out = out + d * lax.rsqrt(var + EPS)
    return out * g


def _dot(a, b):
    return jnp.dot(a, b, preferred_element_type=F32)


def _dot_nt(a, b):
    return lax.dot_general(a, b, (((1,), (1,)), ((), ())), preferred_element_type=F32)


def _sink_softmax(s, sink):
    m = jnp.maximum(jnp.max(s, axis=-1, keepdims=True), sink)
    p = jnp.exp(s - m)
    den = jnp.sum(p, axis=-1, keepdims=True) + jnp.exp(sink - m)
    return p * (1.0 / den)


def _alibi_slopes(n_heads):
    return [float(2.0 ** (-8.0 * h / n_heads)) for h in range(1, n_heads + 1)]


def _ffn_kernel(*refs, layer, half, n_prompt_tiles, split_in, split_out, d_ff, tf, n_sub):
    refs = list(refs)
    xp_ref = refs.pop(0)
    xs_ref = refs.pop(0) if split_in else None
    norm_ref, wgu_hbm, wdn_hbm = refs[:3]
    refs = refs[3:]
    op_ref = refs.pop(0)
    os_ref = refs.pop(0) if split_out else None
    wgu16_ref, wdn16_ref, stage_gu_ref, stage_dn_ref, sem_ref, act_ref = refs
    i = pl.program_id(0)
    n_chunks = d_ff // tf

    def chunk_copies(c, slot):
        return [
            pltpu.make_async_copy(wgu_hbm.at[layer, half, :, pl.ds(c * tf, tf)],
                                  stage_gu_ref.at[slot, 0], sem_ref.at[slot, 0]),
            pltpu.make_async_copy(wgu_hbm.at[layer, half, :, pl.ds(d_ff + c * tf, tf)],
                                  stage_gu_ref.at[slot, 1], sem_ref.at[slot, 1]),
            pltpu.make_async_copy(wdn_hbm.at[layer, half, pl.ds(c * tf, tf), :],
                                  stage_dn_ref.at[slot], sem_ref.at[slot, 2]),
        ]

    def fetch(c):
        for cp in chunk_copies(c, c % 2):
            cp.start()

    def land(c):
        slot = c % 2
        for cp in chunk_copies(c, slot):
            cp.wait()
        wgu16_ref[:, c * tf:(c + 1) * tf] = stage_gu_ref[slot, 0].astype(BF16)
        wgu16_ref[:, d_ff + c * tf:d_ff + (c + 1) * tf] = stage_gu_ref[slot, 1].astype(BF16)
        wdn16_ref[c * tf:(c + 1) * tf, :] = stage_dn_ref[slot].astype(BF16)

    if split_in and split_out:
        x = jnp.where(i == 0, xs_ref[...], xp_ref[...])
    elif split_in:
        x = jnp.where(i < n_prompt_tiles, xp_ref[...], xs_ref[...])
    else:
        x = xp_ref[...]
    g_pre = norm_ref[layer, 4 * half:4 * half + 1, :]
    g_post = norm_ref[layer, 4 * half + 1:4 * half + 2, :]

    def hidden_chunks(h, rows, first, count):
        for c in range(first, first + count):
            gate = _dot(h, wgu16_ref[:, c * tf:(c + 1) * tf])
            up = _dot(h, wgu16_ref[:, d_ff + c * tf:d_ff + (c + 1) * tf])
            act_ref[rows, c * tf:(c + 1) * tf] = (jax.nn.silu(gate) * up).astype(BF16)

    def store(rows, val):
        op_ref[rows, :] = val
        if split_out:
            os_ref[rows, :] = val

    @pl.when(i == 0)
    def _():
        fetch(0)
        for c in range(n_chunks):
            if c + 1 < n_chunks:
                fetch(c + 1)
            land(c)

    sr = x.shape[0] // n_sub
    rows = [slice(k * sr, (k + 1) * sr) for k in range(n_sub)]
    xk = [x[r] for r in rows]
    h_next = _rms(xk[0], g_pre).astype(BF16)
    y_prev = None
    for k in range(n_sub):
        h = h_next
        hidden_chunks(h, rows[k], 0, 2)
        if k + 1 < n_sub:
            h_next = _rms(xk[k + 1], g_pre).astype(BF16)
        hidden_chunks(h, rows[k], 2, 1)
        if k > 0:
            store(rows[k - 1], xk[k - 1] + 0.5 * _rms(y_prev, g_post))
        hidden_chunks(h, rows[k], 3, n_chunks - 3)
        y_prev = _dot(act_ref[rows[k], :], wdn16_ref[...])
    store(rows[-1], xk[-1] + 0.5 * _rms(y_prev, g_post))


def _ffn(xs_in, norm_g, w_gu, w_down, layer, half, n_prompt_rows, split_out):
    split_in = len(xs_in) == 2
    d = xs_in[0].shape[1]
    d_ff = w_down.shape[2]
    tm = FFN_TM
    n_sample_rows = xs_in[1].shape[0] if split_in else xs_in[0].shape[0] - n_prompt_rows
    assert n_prompt_rows % tm == 0 and n_sample_rows == tm and d_ff % FFN_TF == 0
    npt = n_prompt_rows // tm
    tf = FFN_TF
    assert tf % LANES == 0
    assert split_in or not split_out
    if split_out:
        prompt_tile = lambda i: (jnp.maximum(i - 1, 0), 0)
    else:
        prompt_tile = lambda i: (jnp.minimum(i, npt - 1), 0)
    const = lambda i: (0, 0)
    in_specs = []
    if split_in:
        in_specs += [pl.BlockSpec((tm, d), prompt_tile),
                     pl.BlockSpec((tm, d), const, pipeline_mode=pl.Buffered(1))]
    else:
        in_specs += [pl.BlockSpec((tm, d), lambda i: (i, 0))]
    in_specs += [pl.BlockSpec(norm_g.shape, lambda i: (0, 0, 0)),
                 pl.BlockSpec(memory_space=pl.ANY),
                 pl.BlockSpec(memory_space=pl.ANY)]
    if split_out:
        out_specs = [pl.BlockSpec((tm, d), prompt_tile),
                     pl.BlockSpec((tm, d), lambda i: (jnp.minimum(i, 1), 0))]
        out_shape = [jax.ShapeDtypeStruct((n_prompt_rows, d), F32),
                     jax.ShapeDtypeStruct((2 * n_sample_rows, d), F32)]
    else:
        out_specs = pl.BlockSpec((tm, d), lambda i: (i, 0))
        out_shape = jax.ShapeDtypeStruct((n_prompt_rows + n_sample_rows, d), F32)
    weights16 = 3 * d * d_ff * 2
    staging = 2 * 3 * d * tf * 4
    vmem = weights16 + staging + 7 * tm * d * 4 + tm * d_ff * 2 + 10 * MIB
    return pl.pallas_call(
        functools.partial(_ffn_kernel, layer=layer, half=half, n_prompt_tiles=npt,
                          split_in=split_in, split_out=split_out, d_ff=d_ff, tf=tf,
                          n_sub=FFN_SUBTILES),
        grid=(npt + 1,),
        in_specs=in_specs,
        out_specs=out_specs,
        out_shape=out_shape,
        scratch_shapes=[
            pltpu.VMEM((d, 2 * d_ff), BF16),
            pltpu.VMEM((d_ff, d), BF16),
            pltpu.VMEM((2, 2, d, tf), F32),
            pltpu.VMEM((2, tf, d), F32),
            pltpu.SemaphoreType.DMA((2, 3)),
            pltpu.VMEM((tm, d_ff), BF16),
        ],
        compiler_params=pltpu.CompilerParams(
            dimension_semantics=("arbitrary",), vmem_limit_bytes=vmem),
        name="ffn_half_step",
    )(*xs_in, norm_g, w_gu, w_down)


def _mixp_kernel(*refs, **statics):
    parity = lax.rem(pl.program_id(0), 2)

    @pl.when(parity == 0)
    def _():
        _mixp_body(*refs, parity=0, **statics)

    @pl.when(parity == 1)
    def _():
        _mixp_body(*refs, parity=1, **statics)


def _mixp_body(sinks_all_ref, xnext_ref, x_ref, norm_ref, win_ref, wint_ref, wout_ref, gout_all_ref,
               goutb_ref, wsgu_ref, bsgu_ref, gsgu_all_ref, wconv_all_ref,
               xo_ref, ko_ref, vo_ref, co_ref,
               proj_even_ref, proj_odd_ref, qvt_even_ref, qvt_odd_ref, kb_ref, vtb_ref, zbuf_ref,
               ycat_ref, ybt_ref, abias_ref,
               *, parity, layer, tiles_per_seq, slopes, w_a, w_b, w_kv, w_c):
    proj_next_ref, proj_ref = ((proj_even_ref, proj_odd_ref) if parity == 0
                               else (proj_odd_ref, proj_even_ref))
    qvt_next_ref, qvt_ref = ((qvt_even_ref, qvt_odd_ref) if parity == 0
                             else (qvt_odd_ref, qvt_even_ref))
    g2_ref = norm_ref.at[layer, pl.ds(2, 1)]
    g3_ref = norm_ref.at[layer, pl.ds(3, 1)]
    gout_ref = gout_all_ref.at[pl.ds(layer, 1)]
    gsgu_ref = gsgu_all_ref.at[pl.ds(layer, 1)]
    wconv_ref = wconv_all_ref.at[layer]
    step = pl.program_id(0)
    tile = jnp.maximum(step - 1, 0)
    j = lax.rem(tile, tiles_per_seq)
    t_rows = x_ref.shape[0]
    nblk = t_rows // WINDOW
    o_k = 2 * w_a
    o_gb = o_k + w_kv
    o_gc = o_gb + w_c
    o_hc = o_gc + w_c
    n_heads = w_b // HEAD_DIM
    group = n_heads // KV_HEADS

    @pl.when(step == 0)
    def _():
        proj_ref[...] = jnp.zeros(proj_ref.shape, F32)
        qvt_ref[...] = jnp.zeros(qvt_ref.shape, F32)
        ki = lax.broadcasted_iota(jnp.int32, (2 * WINDOW, WINDOW), 0)
        qi = lax.broadcasted_iota(jnp.int32, (2 * WINDOW, WINDOW), 1)
        dist_i = WINDOW + qi - ki
        band = (dist_i >= 0) & (dist_i < WINDOW)
        dist = dist_i.astype(F32)
        for hd in range(n_heads):
            abias_ref[hd] = jnp.where(band, (slopes[hd] * LOG2E) * dist, -NEG)

    @pl.when(j == 0)
    def _():
        kb_ref[0:WINDOW, :] = jnp.zeros((WINDOW, w_kv), BF16)
        vtb_ref[:, 0:WINDOW] = jnp.zeros((w_kv, WINDOW), BF16)
        zbuf_ref[0:8, :] = jnp.zeros((8, w_c), F32)

    hn = _rms(xnext_ref[...], g2_ref[...]).astype(BF16)
    row_w = proj_next_ref.shape[1]
    chunks = [("row", c0, min(c0 + 2 * LANES, row_w)) for c0 in range(0, row_w, 2 * LANES)]
    t_w = qvt_next_ref.shape[0]
    chunks += [("t", r0, min(r0 + 2 * LANES, t_w)) for r0 in range(0, t_w, 2 * LANES)]
    n_pchunks = len(chunks)

    def project_chunks(first, count):
        for kind, lo_, hi_ in chunks[first:first + count]:
            if kind == "row":
                proj_next_ref[:, lo_:hi_] = _dot(hn, win_ref[:, lo_:hi_])
            else:
                qvt_next_ref[lo_:hi_, :] = _dot_nt(wint_ref[lo_:hi_, :], hn)

    x = x_ref[...]
    project_chunks(0, 1)

    u = _gelu(proj_ref[:, 0:w_a])
    v = _head_layer_norm(_gelu(proj_ref[:, w_a:2 * w_a]), gsgu_ref[...])
    lane_a = lax.broadcasted_iota(jnp.int32, (1, w_a), 1)
    ri = lax.broadcasted_iota(jnp.int32, (CHUNK, CHUNK), 0)
    ci = lax.broadcasted_iota(jnp.int32, (CHUNK, CHUNK), 1)
    n_heads_a = w_a // HEAD_DIM
    w_tril = [jnp.where(ri >= ci, wsgu_ref[hh], 0.0).astype(BF16) for hh in range(n_heads_a)]
    bias = bsgu_ref[...]
    ya_parts = []
    for n in range(nblk):
        vblk = v[n * CHUNK:(n + 1) * CHUNK]
        mix = bias
        for hh in range(n_heads_a):
            mh = (lane_a >= hh * HEAD_DIM) & (lane_a < (hh + 1) * HEAD_DIM)
            mix = mix + _dot(w_tril[hh], jnp.where(mh, vblk, 0.0).astype(BF16))
        ya_parts.append(u[n * CHUNK:(n + 1) * CHUNK] * mix)
    ya = jnp.concatenate(ya_parts, axis=0)
    ycat_ref[:, 0:w_a] = _rms(ya, gout_ref[:, 0:w_a]).astype(BF16)
    project_chunks(1, 1)

    kb_ref[WINDOW:, :] = proj_ref[:, o_k:o_k + w_kv].astype(BF16)
    vtb_ref[:, WINDOW:] = qvt_ref[w_b:w_b + w_kv, :].astype(BF16)
    ki = lax.broadcasted_iota(jnp.int32, (2 * WINDOW, WINDOW), 0)
    has_prev = jnp.full((2 * WINDOW, WINDOW), j, jnp.int32) > 0
    first_ok = (ki >= WINDOW) | has_prev
    zrows = jnp.zeros((HEAD_DIM, group * WINDOW), F32)

    for n in range(nblk):
        qcols = slice(n * WINDOW, (n + 1) * WINDOW)
        band = slice(n * WINDOW, (n + 2) * WINDOW)
        for kv in range(KV_HEADS):
            qd = jnp.concatenate(
                [qvt_ref[(kv * group + g) * HEAD_DIM:(kv * group + g + 1) * HEAD_DIM, qcols]
                 for g in range(group)], axis=1) * (SCALE * LOG2E)
            rhs = jnp.concatenate([qd, zrows] if kv == 0 else [zrows, qd], axis=0).astype(BF16)
            s_t = _dot(kb_ref[band, :], rhs)
            if kv == 0:
                project_chunks(2 + n, 1)
            p_parts, inv_parts = [], []
            for g in range(group):
                head = kv * group + g
                s = s_t[:, g * WINDOW:(g + 1) * WINDOW] - abias_ref[head]
                if n == 0:
                    s = jnp.where(first_ok, s, NEG)
                sink = sinks_all_ref[layer, head] * LOG2E
                m = jnp.maximum(jnp.max(s, axis=0, keepdims=True), sink)
                p = jnp.exp2(s - m)
                den = jnp.sum(p, axis=0, keepdims=True) + jnp.exp2(sink - m)
                p_parts.append(p.astype(BF16))
                inv_parts.append(1.0 / den)
            o_t = _dot(vtb_ref[kv * HEAD_DIM:(kv + 1) * HEAD_DIM, band],
                       jnp.concatenate(p_parts, axis=1))
            o_t = o_t * jnp.concatenate(inv_parts, axis=1)
            for g in range(group):
                head = kv * group + g
                ybt_ref[head * HEAD_DIM:(head + 1) * HEAD_DIM, qcols] = (
                    o_t[:, g * WINDOW:(g + 1) * WINDOW])
    ybt = ybt_ref[...]
    ybn = (ybt * lax.rsqrt(jnp.mean(ybt * ybt, axis=0, keepdims=True) + EPS)
           * goutb_ref[...]).astype(BF16)
    kb_ref[0:WINDOW, :] = kb_ref[t_rows:t_rows + WINDOW, :]
    vtb_ref[:, 0:WINDOW] = vtb_ref[:, t_rows:t_rows + WINDOW]

    project_chunks(2 + nblk, 1)

    z = proj_ref[:, o_gc:o_gc + w_c] * proj_ref[:, o_hc:o_hc + w_c]
    zbuf_ref[8:8 + t_rows, :] = z
    conv = zbuf_ref[8 - (CONV_W - 1):8 - (CONV_W - 1) + t_rows, :] * wconv_ref[0:1, :]
    for jj in range(1, CONV_W):
        off = 8 - (CONV_W - 1) + jj
        conv = conv + zbuf_ref[off:off + t_rows, :] * wconv_ref[jj:jj + 1, :]
    yc = proj_ref[:, o_gb:o_gb + w_c] * conv
    ycat_ref[:, w_a:] = _rms(yc, gout_ref[:, w_a + w_b:]).astype(BF16)
    z_tail = zbuf_ref[8 + t_rows - (CONV_W - 1):8 + t_rows, :]
    zbuf_ref[8 - (CONV_W - 1):8, :] = z_tail

    y = (_dot(ycat_ref[:, 0:w_a], wout_ref[0:w_a, :])
         + lax.dot_general(ybn, wout_ref[w_a:w_a + w_b, :], (((0,), (0,)), ((), ())),
                           preferred_element_type=F32)
         + _dot(ycat_ref[:, w_a:], wout_ref[w_a + w_b:, :]))
    project_chunks(3 + nblk, n_pchunks - 3 - nblk)
    xo_ref[...] = x + _rms(y, g3_ref[...])

    @pl.when((j == tiles_per_seq - 1) & (step > 0))
    def _():
        ko_ref[0] = proj_ref[t_rows - WINDOW:, o_k:o_k + w_kv].T
        vo_ref[0] = qvt_ref[w_b:w_b + w_kv, t_rows - WINDOW:]
        co_ref[0] = z_tail


def _mixer_prompt(x_all, layer, batch, seq, norm_g, win, wint, wout, gout_all, goutb_all, wsgu_all,
                  bsgu_full_all, gsgu_all, sinks_all, wconv_all, w_a, w_b, w_kv, w_c):
    d = x_all.shape[1]
    m = batch * seq
    t = MIX_T
    assert seq % t == 0 and t % WINDOW == 0 and WINDOW == LANES
    nt = seq // t
    in_w = win.shape[2]
    t_w = wint.shape[1]
    assert in_w == 2 * w_a + w_kv + 3 * w_c and t_w == w_b + w_kv
    n_heads_b = w_b // HEAD_DIM
    n_tiles = batch * nt
    const2 = lambda s, *_: (0, 0)
    const3 = lambda s, *_: (0, 0, 0)
    layer3 = lambda s, *_: (layer, 0, 0)
    layer4 = lambda s, *_: (layer, 0, 0, 0)
    row_next = lambda s, *_: (jnp.minimum(s, n_tiles - 1), 0)
    row = lambda s, *_: (jnp.maximum(s - 1, 0), 0)
    per_b = lambda s, *_: (jnp.maximum(s - 1, 0) // nt, 0, 0)
    kernel = functools.partial(_mixp_kernel, layer=layer, tiles_per_seq=nt,
                               slopes=_alibi_slopes(n_heads_b),
                               w_a=w_a, w_b=w_b, w_kv=w_kv, w_c=w_c)
    grid_spec = pltpu.PrefetchScalarGridSpec(
        num_scalar_prefetch=1,
        grid=(n_tiles + 1,),
        in_specs=[
            pl.BlockSpec((t, d), row_next),
            pl.BlockSpec((t, d), row),
            pl.BlockSpec(norm_g.shape, const3),
            pl.BlockSpec((None,) + win.shape[1:], layer3, pipeline_mode=pl.Buffered(1)),
            pl.BlockSpec((None,) + wint.shape[1:], layer3, pipeline_mode=pl.Buffered(1)),
            pl.BlockSpec((None,) + wout.shape[1:], layer3, pipeline_mode=pl.Buffered(1)),
            pl.BlockSpec(gout_all.shape, const2),
            pl.BlockSpec((None,) + goutb_all.shape[1:], layer3),
            pl.BlockSpec((None,) + wsgu_all.shape[1:], layer4),
            pl.BlockSpec((None,) + bsgu_full_all.shape[1:], layer3),
            pl.BlockSpec(gsgu_all.shape, const2),
            pl.BlockSpec(wconv_all.shape, const3),
        ],
        out_specs=[
            pl.BlockSpec((t, d), row),
            pl.BlockSpec((1, w_kv, WINDOW), per_b),
            pl.BlockSpec((1, w_kv, WINDOW), per_b),
            pl.BlockSpec((1, CONV_W - 1, w_c), per_b),
        ],
        scratch_shapes=[
            pltpu.VMEM((t, in_w), F32),
            pltpu.VMEM((t, in_w), F32),
            pltpu.VMEM((t_w, t), F32),
            pltpu.VMEM((t_w, t), F32),
            pltpu.VMEM((t + WINDOW, w_kv), BF16),
            pltpu.VMEM((w_kv, t + WINDOW), BF16),
            pltpu.VMEM((t + 8, w_c), F32),
            pltpu.VMEM((t, w_a + w_c), BF16),
            pltpu.VMEM((w_b, t), F32),
            pltpu.VMEM((n_heads_b, 2 * WINDOW, WINDOW), F32),
        ],
    )
    depth = win.shape[0]
    vmem = ((win.size + wint.size + wout.size) // depth * 2 + 6 * t * d * 4
            + 2 * t * (in_w + t_w) * 4 + 24 * MIB)
    return pl.pallas_call(
        kernel,
        grid_spec=grid_spec,
        out_shape=[
            jax.ShapeDtypeStruct((m, d), F32),
            jax.ShapeDtypeStruct((batch, w_kv, WINDOW), F32),
            jax.ShapeDtypeStruct((batch, w_kv, WINDOW), F32),
            jax.ShapeDtypeStruct((batch, CONV_W - 1, w_c), F32),
        ],
        compiler_params=pltpu.CompilerParams(
            dimension_semantics=("arbitrary",), vmem_limit_bytes=vmem),
        name="mixer_prompt",
    )(sinks_all, x_all, x_all, norm_g, win, wint, wout, gout_all, goutb_all, wsgu_all,
      bsgu_full_all, gsgu_all, wconv_all)


def _mixs_kernel(x_ref, norm_ref, win_ref, wout_ref, gout_all_ref, coef_ref, bias_ref, gsgu_all_ref,
                 wconv_all_ref, cexp1_ref, cexp2_ref, sinkcol_ref, kc_ref, vc_ref,
                 xo_ref, vsgu_ref, z_ref, ko_ref, vo_ref,
                 proj_ref, qprep_ref, ybuf_ref, ycat_ref, knt_ref, vnt_ref, vtmp_ref,
                 *, layer, slopes, t_new, w_a, w_b, w_kv, w_c):
    g2_ref = norm_ref.at[layer, pl.ds(2, 1)]
    g3_ref = norm_ref.at[layer, pl.ds(3, 1)]
    gout_ref = gout_all_ref.at[pl.ds(layer, 1)]
    gsgu_ref = gsgu_all_ref.at[pl.ds(layer, 1)]
    wconv_ref = wconv_all_ref.at[layer]
    g = pl.program_id(0)
    rows_all = x_ref.shape[0]
    n_seq = rows_all // t_new
    gsz = kc_ref.shape[0]
    o_q = 2 * w_a
    o_k = o_q + w_b
    o_v = o_k + w_kv
    o_gb = o_v + w_kv
    o_gc = o_gb + w_c
    o_hc = o_gc + w_c
    n_heads = w_b // HEAD_DIM
    group = n_heads // KV_HEADS
    lane = lax.broadcasted_iota(jnp.int32, (1, LANES), 1)
    lo = lane < HEAD_DIM

    @pl.when(g == 0)
    def _():
        x = x_ref[...]
        h = _rms(x, g2_ref[...]).astype(BF16)
        proj_ref[...] = _dot(h, win_ref[...])
        tpos = lax.broadcasted_iota(jnp.int32, (rows_all, 1), 0) % t_new

        u = _gelu(proj_ref[:, 0:w_a])
        v = _head_layer_norm(_gelu(proj_ref[:, w_a:2 * w_a]), gsgu_ref[...])
        for c in range(w_a // LANES):
            vtmp_ref[c] = v[:, c * LANES:(c + 1) * LANES]
        for tt in range(t_new):
            for c in range(w_a // LANES):
                vsgu_ref[tt, c * LANES:(c + 1) * LANES, :] = (
                    vtmp_ref[c, pl.ds(tt, n_seq, stride=t_new), :].T)
        mix = bias_ref[...] + coef_ref[0] * v
        for dlt in range(1, t_new):
            mix = mix + coef_ref[dlt] * pltpu.roll(v, dlt, 0)
        ycat_ref[:, 0:w_a] = _rms(u * mix, gout_ref[:, 0:w_a]).astype(BF16)

        z = proj_ref[:, o_gc:o_gc + w_c] * proj_ref[:, o_hc:o_hc + w_c]
        z_ref[...] = z
        s2 = jnp.where(tpos >= 2, pltpu.roll(z, 2, 0), 0.0) + cexp2_ref[...]
        s1 = jnp.where(tpos >= 1, pltpu.roll(z, 1, 0), 0.0) + cexp1_ref[...]
        conv = s2 * wconv_ref[0:1, :] + s1 * wconv_ref[1:2, :] + z * wconv_ref[2:3, :]
        yc = proj_ref[:, o_gb:o_gb + w_c] * conv
        ycat_ref[:, w_a + w_b:] = _rms(yc, gout_ref[:, w_a + w_b:]).astype(BF16)

        for hd in range(n_heads):
            c, hf, kvh = hd // 2, hd % 2, hd // group
            piece = proj_ref[:, o_q + c * LANES:o_q + (c + 1) * LANES]
            if hf != kvh:
                piece = pltpu.roll(piece, HEAD_DIM, 1)
            qprep_ref[hd] = jnp.where(lo if kvh == 0 else ~lo, piece * (SCALE * LOG2E), 0.0)

        for c in range(rows_all // LANES):
            knt_ref[c] = proj_ref[c * LANES:(c + 1) * LANES, o_k:o_k + w_kv].T
            vnt_ref[c] = proj_ref[c * LANES:(c + 1) * LANES, o_v:o_v + w_kv].T

    pair_rows = 2 * t_new
    n_pairs = gsz // 2
    lrows = n_heads * pair_rows
    ri = lax.broadcasted_iota(jnp.int32, (lrows, 1), 0)
    r_in = ri % pair_rows
    seq_q = r_in // t_new
    t_q = r_in % t_new
    cj = lax.broadcasted_iota(jnp.int32, (1, 2 * WINDOW), 1)
    valid_c = (cj // WINDOW == seq_q) & (cj % WINDOW >= t_q + 1)
    dist_c = (WINDOW + t_q - cj % WINDOW).astype(F32)
    nj = lax.broadcasted_iota(jnp.int32, (1, LANES), 1)
    valid_n = (nj < pair_rows) & (nj // t_new == seq_q) & (nj % t_new <= t_q)
    dist_n = (t_q - nj % t_new).astype(F32)
    slope_col = jnp.zeros((lrows, 1), F32)
    for hd in range(n_heads):
        slope_col = jnp.where(ri // pair_rows == hd, slopes[hd], slope_col)
    sink = sinkcol_ref[...] * LOG2E
    bias_c = jnp.where(valid_c, (slope_col * LOG2E) * dist_c, -NEG)
    bias_n = jnp.where(valid_n, (slope_col * LOG2E) * dist_n, -NEG)
    zpad = jnp.zeros((LANES - pair_rows, w_kv), F32)

    keep = lane < WINDOW - t_new
    kn_t = knt_ref[g]
    vn_t = vnt_ref[g]

    def slide(src_ref, dst_ref, new_t, i):
        shift = (WINDOW - t_new - t_new * i) % LANES
        new = pltpu.roll(new_t, shift, 1) if shift else new_t
        dst_ref[i] = jnp.where(keep, pltpu.roll(src_ref[i], WINDOW - t_new, 1), new)

    r0 = pl.multiple_of(g * (n_pairs * pair_rows), n_pairs * pair_rows)
    sc_parts, sn_parts = [], []
    for p in range(n_pairs):
        rows = pl.ds(r0 + p * pair_rows, pair_rows)
        lhs = jnp.concatenate([qprep_ref[hd, rows, :] for hd in range(n_heads)], axis=0).astype(BF16)
        kn = proj_ref[rows, o_k:o_k + w_kv]
        kc = jnp.concatenate([kc_ref[2 * p], kc_ref[2 * p + 1]], axis=1)
        sc_parts.append(_dot(lhs, kc.astype(BF16)))
        sn_parts.append(_dot_nt(lhs, jnp.concatenate([kn, zpad], axis=0).astype(BF16)))
        slide(kc_ref, ko_ref, kn_t, 2 * p)
        slide(kc_ref, ko_ref, kn_t, 2 * p + 1)
    s_c = jnp.stack(sc_parts) - bias_c[None]
    s_n = jnp.stack(sn_parts) - bias_n[None]
    m = jnp.maximum(jnp.maximum(jnp.max(s_c, axis=-1, keepdims=True),
                                jnp.max(s_n, axis=-1, keepdims=True)), sink[None])
    p_c = jnp.exp2(s_c - m)
    p_n = jnp.exp2(s_n - m)
    den = (jnp.sum(p_c, axis=-1, keepdims=True) + jnp.sum(p_n, axis=-1, keepdims=True)
           + jnp.exp2(sink[None] - m))
    inv = 1.0 / den
    p_c = p_c.astype(BF16)
    p_n = p_n.astype(BF16)
    for p in range(n_pairs):
        rows = pl.ds(r0 + p * pair_rows, pair_rows)
        vn = proj_ref[rows, o_v:o_v + w_kv]
        vc = jnp.concatenate([vc_ref[2 * p], vc_ref[2 * p + 1]], axis=1)
        slide(vc_ref, vo_ref, vn_t, 2 * p)
        slide(vc_ref, vo_ref, vn_t, 2 * p + 1)
        o = (_dot_nt(p_c[p], vc.astype(BF16))
             + _dot(p_n[p], jnp.concatenate([vn, zpad], axis=0).astype(BF16))) * inv[p]
        cols = []
        for c in range(n_heads // 2):
            pieces = []
            for hf in range(2):
                hd = 2 * c + hf
                piece = o[hd * pair_rows:(hd + 1) * pair_rows]
                if hd // group != hf:
                    piece = pltpu.roll(piece, HEAD_DIM, 1)
                pieces.append(piece)
            cols.append(jnp.where(lo, pieces[0], pieces[1]))
        ybuf_ref[rows, :] = jnp.concatenate(cols, axis=1)

    @pl.when(g == pl.num_programs(0) - 1)
    def _():
        ycat_ref[:, w_a:w_a + w_b] = _rms(ybuf_ref[...], gout_ref[:, w_a:w_a + w_b]).astype(BF16)
        y = _dot(ycat_ref[...], wout_ref[...])
        xo_ref[...] = x_ref[...] + _rms(y, g3_ref[...])


def _mixer_sample(x_all, layer, n_prompt_rows, n_seq, t_new, norm_g, win, wout, gout_all, coef_all,
                  bias_all, gsgu_all, wconv_all, cexp1_all, cexp2_all, sink_col_all, kc_all, vc_all,
                  w_a, w_b, w_kv, w_c):
    d = x_all.shape[1]
    m = n_seq * t_new
    assert n_prompt_rows % m == 0
    in_w = win.shape[2]
    n_heads = w_b // HEAD_DIM
    gsz = LANES // t_new
    assert n_seq % gsz == 0 and gsz % 2 == 0 and 2 * t_new == 8 and WINDOW == LANES
    ng = n_seq // gsz
    const2 = lambda g: (0, 0)
    const3 = lambda g: (0, 0, 0)
    layer3 = lambda g: (layer, 0, 0)
    layer4 = lambda g: (layer, 0, 0, 0)
    step_in = lambda g: (layer, g, 0, 0)
    step_out = lambda g: (g, 0, 0)
    kernel = functools.partial(_mixs_kernel, layer=layer, slopes=_alibi_slopes(n_heads),
                               t_new=t_new, w_a=w_a, w_b=w_b, w_kv=w_kv, w_c=w_c)
    coef_bytes = coef_all.size // coef_all.shape[0] * 4
    depth = win.shape[0]
    vmem = ((win.size + wout.size) // depth * 2 + 4 * m * d * 4 + m * in_w * 4
            + 8 * gsz * WINDOW * w_kv * 4 + coef_bytes * 2 + 24 * MIB)
    return pl.pallas_call(
        kernel,
        grid=(ng,),
        in_specs=[
            pl.BlockSpec((m, d), lambda g: (n_prompt_rows // m, 0)),
            pl.BlockSpec(norm_g.shape, const3),
            pl.BlockSpec((None,) + win.shape[1:], layer3, pipeline_mode=pl.Buffered(1)),
            pl.BlockSpec((None,) + wout.shape[1:], layer3, pipeline_mode=pl.Buffered(1)),
            pl.BlockSpec(gout_all.shape, const2),
            pl.BlockSpec((None,) + coef_all.shape[1:], layer4),
            pl.BlockSpec((None,) + bias_all.shape[1:], layer3),
            pl.BlockSpec(gsgu_all.shape, const2),
            pl.BlockSpec(wconv_all.shape, const3),
            pl.BlockSpec((None,) + cexp1_all.shape[1:], layer3),
            pl.BlockSpec((None,) + cexp2_all.shape[1:], layer3),
            pl.BlockSpec((None,) + sink_col_all.shape[1:], layer3),
            pl.BlockSpec((None, gsz, w_kv, WINDOW), step_in),
            pl.BlockSpec((None, gsz, w_kv, WINDOW), step_in),
        ],
        out_specs=[
            pl.BlockSpec((m, d), const2),
            pl.BlockSpec((t_new, w_a, n_seq), const3),
            pl.BlockSpec((m, w_c), const2),
            pl.BlockSpec((gsz, w_kv, WINDOW), step_out),
            pl.BlockSpec((gsz, w_kv, WINDOW), step_out),
        ],
        out_shape=[
            jax.ShapeDtypeStruct((m, d), F32),
            jax.ShapeDtypeStruct((t_new, w_a, n_seq), F32),
            jax.ShapeDtypeStruct((m, w_c), F32),
            jax.ShapeDtypeStruct((n_seq, w_kv, WINDOW), F32),
            jax.ShapeDtypeStruct((n_seq, w_kv, WINDOW), F32),
        ],
        scratch_shapes=[
            pltpu.VMEM((m, in_w), F32),
            pltpu.VMEM((n_heads, m, LANES), F32),
            pltpu.VMEM((m, w_b), F32),
            pltpu.VMEM((m, gout_all.shape[1]), BF16),
            pltpu.VMEM((ng, w_kv, LANES), F32),
            pltpu.VMEM((ng, w_kv, LANES), F32),
            pltpu.VMEM((w_a // LANES, m, LANES), F32),
        ],
        compiler_params=pltpu.CompilerParams(
            dimension_semantics=("arbitrary",), vmem_limit_bytes=vmem),
        name="mixer_sample",
    )(x_all, norm_g, win, wout, gout_all, coef_all, bias_all, gsgu_all, wconv_all, cexp1_all,
      cexp2_all, sink_col_all, kc_all, vc_all)


def kernel(x_prompt, x_sample, cache_swa_k, cache_swa_v, cache_conv, norm_g, w_ffn_gu, w_ffn_down,
           w_mix_in, w_mix_out, g_mix_out, w_sgu, b_sgu, g_sgu, attn_sinks, w_conv):
    batch, seq, d = x_prompt.shape
    n_seq, t_new, _ = x_sample.shape
    depth = norm_g.shape[0]
    w_a = g_sgu.shape[1]
    w_c = w_conv.shape[2]
    w_kv = KV_HEADS * HEAD_DIM
    w_b = g_mix_out.shape[1] - w_a - w_c
    n_heads_a = w_a // HEAD_DIM
    n_heads_b = w_b // HEAD_DIM
    assert t_new <= CHUNK and seq % CHUNK == 0

    xp = x_prompt.reshape(batch * seq, d)
    xs = x_sample.reshape(n_seq * t_new, d)
    to_dp = lambda c: jnp.transpose(c, (0, 1, 3, 4, 2)).reshape(depth, c.shape[1], w_kv, c.shape[2])
    from_dp = lambda c: jnp.transpose(
        c.reshape(depth, c.shape[1], KV_HEADS, HEAD_DIM, c.shape[3]), (0, 1, 4, 2, 3))
    kc_all = to_dp(cache_swa_k)
    vc_all = to_dp(cache_swa_v)

    win_all = w_mix_in.astype(BF16)
    wout_all = w_mix_out.astype(BF16)
    o_q, o_k, o_v = 2 * w_a, 2 * w_a + w_b, 2 * w_a + w_b + w_kv
    win_row_all = jnp.concatenate(
        [win_all[:, :, :o_q], win_all[:, :, o_k:o_v], win_all[:, :, o_v + w_kv:]], axis=2)
    win_t_all = jnp.swapaxes(
        jnp.concatenate([win_all[:, :, o_q:o_k], win_all[:, :, o_v:o_v + w_kv]], axis=2), 1, 2)
    goutb_all = g_mix_out[:, w_a:w_a + w_b, None]
    bsgu_full_all = jnp.repeat(jnp.swapaxes(b_sgu, 1, 2), HEAD_DIM, axis=2)
    w4 = jnp.tril(w_sgu[:, :, :t_new, :t_new])
    tt = np.arange(t_new)
    coef_all = jnp.stack([
        jnp.where((tt >= dlt)[None, None, :], w4[:, :, tt, np.maximum(tt - dlt, 0)], 0.0)
        for dlt in range(t_new)], axis=1)
    coef_all = jnp.repeat(coef_all.transpose(0, 1, 3, 2), HEAD_DIM, axis=3)
    coef_all = jnp.tile(coef_all, (1, 1, n_seq, 1))
    bias_all = jnp.tile(jnp.repeat(jnp.swapaxes(b_sgu[:, :, :t_new], 1, 2), HEAD_DIM, axis=2),
                        (1, n_seq, 1))
    pad_t = lambda c: jnp.pad(c, ((0, 0), (0, 0), (0, t_new - c.shape[2]), (0, 0))).reshape(
        depth, n_seq * t_new, w_c)
    cexp2_all = pad_t(cache_conv)
    cexp1_all = pad_t(cache_conv[:, :, 1:])
    sink_col_all = jnp.repeat(attn_sinks, 2 * t_new, axis=1)[:, :, None]

    n_prompt_rows = batch * seq
    outs = {k: [] for k in ("sgu", "kp", "vp", "ks", "vs", "cp", "cs")}
    x_in = (xp, xs)
    for l in range(depth):
        x_all = _ffn(x_in, norm_g, w_ffn_gu, w_ffn_down, l, 0, n_prompt_rows, False)
        xp, kp, vp, cp = _mixer_prompt(
            x_all, l, batch, seq, norm_g, win_row_all, win_t_all, wout_all, g_mix_out, goutb_all,
            w_sgu, bsgu_full_all, g_sgu, attn_sinks, w_conv, w_a, w_b, w_kv, w_c)
        xs, vsgu, z_s, ks, vs = _mixer_sample(
            x_all, l, n_prompt_rows, n_seq, t_new, norm_g, win_all, wout_all, g_mix_out,
            coef_all, bias_all, g_sgu, w_conv, cexp1_all, cexp2_all, sink_col_all, kc_all, vc_all,
            w_a, w_b, w_kv, w_c)
        last = l == depth - 1
        res = _ffn((xp, xs), norm_g, w_ffn_gu, w_ffn_down, l, 1, n_prompt_rows, last)
        if last:
            xp, xs = res[0], res[1][:n_seq * t_new]
        else:
            x_in = (res,)

        outs["sgu"].append(vsgu)
        outs["kp"].append(kp)
        outs["vp"].append(vp)
        outs["ks"].append(ks)
        outs["vs"].append(vs)
        outs["cp"].append(cp)
        outs["cs"].append(z_s.reshape(n_seq, t_new, w_c)[:, t_new - (CONV_W - 1):])
    sgu = jnp.transpose(
        jnp.stack(outs["sgu"]).reshape(depth, t_new, n_heads_a, HEAD_DIM, n_seq), (0, 4, 1, 2, 3))
    return (xp.reshape(batch, seq, d), xs.reshape(n_seq, t_new, d), sgu,
            from_dp(jnp.stack(outs["kp"])), from_dp(jnp.stack(outs["vp"])),
            from_dp(jnp.stack(outs["ks"])), from_dp(jnp.stack(outs["vs"])),
            jnp.stack(outs["cp"]), jnp.stack(outs["cs"]))
```

```python
import functools

import numpy as np
import jax
import jax.numpy as jnp
from jax import lax
from jax.experimental import pallas as pl
from jax.experimental.pallas import tpu as pltpu

F32 = jnp.float32
BF16 = jnp.bfloat16

HEAD_DIM = 64
KV_HEADS = 2
WINDOW = 128
CHUNK = 128
CONV_W = 3
EPS = 1e-6
NEG = -1e30
SCALE = HEAD_DIM ** -0.5
LOG2E = float(np.log2(np.e))
LANES = 128

FFN_TM = 512
FFN_TF = 256
MIX_T = 512
MIB = 1024 * 1024


def _rms(x, g):
    return x * lax.rsqrt(jnp.mean(x * x, axis=-1, keepdims=True) + EPS) * g


def _gelu(x):
    return 0.5 * x * (1.0 + lax.erf(x * np.float32(np.sqrt(0.5))))


def _head_layer_norm(x, g):
    lane = lax.broadcasted_iota(jnp.int32, (1, x.shape[1]), 1)
    out = jnp.zeros_like(x)
    for hh in range(x.shape[1] // HEAD_DIM):
        m = (lane >= hh * HEAD_DIM) & (lane < (hh + 1) * HEAD_DIM)
        mu = jnp.sum(jnp.where(m, x, 0.0), axis=-1, keepdims=True) / HEAD_DIM
        d = jnp.where(m, x - mu, 0.0)
        var = jnp.sum(d * d, axis=-1, keepdims=True) / HEAD_DIM
        out = out + d * lax.rsqrt(var + EPS)
    return out * g


def _dot(a, b):
    return jnp.dot(a, b, preferred_element_type=F32)


def _dot_nt(a, b):
    return lax.dot_general(a, b, (((1,), (1,)), ((), ())), preferred_element_type=F32)


def _alibi_slopes(n_heads):
    return [float(2.0 ** (-8.0 * h / n_heads)) for h in range(1, n_heads + 1)]


def _ffn_kernel(*refs, layer, half, n_prompt_tiles, split_in, split_out, d_ff, tf, n_stream):
    refs = list(refs)
    xp_ref = refs.pop(0)
    xs_ref = refs.pop(0) if split_in else None
    norm_ref, wgu_hbm, wdn_hbm = refs[:3]
    refs = refs[3:]
    op_ref = refs.pop(0)
    os_ref = refs.pop(0) if split_out else None
    wgu16_ref, wdn16_ref, stage_gu_ref, stage_dn_ref, sem_ref, act_ref = refs
    i = pl.program_id(0)
    cw = stage_gu_ref.shape[2]
    rw = stage_dn_ref.shape[1]

    def gu_copy(c, slot):
        return pltpu.make_async_copy(wgu_hbm.at[layer, half, :, pl.ds(c * cw, cw)],
                                     stage_gu_ref.at[slot], sem_ref.at[0, slot])

    def dn_copy(c, slot):
        return pltpu.make_async_copy(wdn_hbm.at[layer, half, pl.ds(c * rw, rw), :],
                                     stage_dn_ref.at[slot], sem_ref.at[1, slot])

    @pl.when(i == 0)
    def _():
        gu_copy(0, 0).start()
        dn_copy(0, 0).start()
        for c in range(n_stream):
            slot = c % 2
            if c + 1 < n_stream:
                gu_copy(c + 1, 1 - slot).start()
                dn_copy(c + 1, 1 - slot).start()
            gu_copy(c, slot).wait()
            dn_copy(c, slot).wait()
            wgu16_ref[:, c * cw:(c + 1) * cw] = stage_gu_ref[slot].astype(BF16)
            wdn16_ref[c * rw:(c + 1) * rw, :] = stage_dn_ref[slot].astype(BF16)

    if split_in and split_out:
        x = jnp.where(i == 0, xs_ref[...], xp_ref[...])
    elif split_in:
        x = jnp.where(i < n_prompt_tiles, xp_ref[...], xs_ref[...])
    else:
        x = xp_ref[...]
    g_pre = norm_ref[layer, 4 * half:4 * half + 1, :]
    g_post = norm_ref[layer, 4 * half + 1:4 * half + 2, :]
    n_chunks = d_ff // tf

    def hidden_chunks(h, rows, first, count):
        for c in range(first, first + count):
            gate = _dot(h, wgu16_ref[:, c * tf:(c + 1) * tf])
            up = _dot(h, wgu16_ref[:, d_ff + c * tf:d_ff + (c + 1) * tf])
            act_ref[rows, c * tf:(c + 1) * tf] = (jax.nn.silu(gate) * up).astype(BF16)

    def store(rows, val):
        op_ref[rows, :] = val
        if split_out:
            os_ref[rows, :] = val

    hr = x.shape[0] // 2
    ra, rb = slice(0, hr), slice(hr, 2 * hr)
    xa, xb = x[ra], x[rb]
    ha = _rms(xa, g_pre).astype(BF16)
    hidden_chunks(ha, ra, 0, 2)
    hb = _rms(xb, g_pre).astype(BF16)
    hidden_chunks(ha, ra, 2, n_chunks - 2)
    ya = _dot(act_ref[ra, :], wdn16_ref[...])
    hidden_chunks(hb, rb, 0, 3)
    store(ra, xa + 0.5 * _rms(ya, g_post))
    hidden_chunks(hb, rb, 3, n_chunks - 3)
    yb = _dot(act_ref[rb, :], wdn16_ref[...])
    store(rb, xb + 0.5 * _rms(yb, g_post))


def _ffn(xs_in, norm_g, w_gu, w_down, layer, half, n_prompt_rows, split_out):
    split_in = len(xs_in) == 2
    d = xs_in[0].shape[1]
    d_ff = w_down.shape[2]
    tm = FFN_TM
    n_sample_rows = xs_in[1].shape[0] if split_in else xs_in[0].shape[0] - n_prompt_rows
    assert n_prompt_rows % tm == 0 and n_sample_rows == tm and d_ff % FFN_TF == 0
    npt = n_prompt_rows // tm
    n_stream = d_ff // FFN_TF
    cw = 2 * d_ff // n_stream
    rw = d_ff // n_stream
    assert cw % LANES == 0 and rw % 8 == 0
    assert split_in or not split_out
    if split_out:
        prompt_tile = lambda i: (jnp.maximum(i - 1, 0), 0)
    else:
        prompt_tile = lambda i: (jnp.minimum(i, npt - 1), 0)
    const = lambda i: (0, 0)
    in_specs = []
    if split_in:
        in_specs += [pl.BlockSpec((tm, d), prompt_tile),
                     pl.BlockSpec((tm, d), const, pipeline_mode=pl.Buffered(1))]
    else:
        in_specs += [pl.BlockSpec((tm, d), lambda i: (i, 0))]
    in_specs += [pl.BlockSpec(norm_g.shape, lambda i: (0, 0, 0)),
                 pl.BlockSpec(memory_space=pl.ANY),
                 pl.BlockSpec(memory_space=pl.ANY)]
    if split_out:
        out_specs = [pl.BlockSpec((tm, d), prompt_tile),
                     pl.BlockSpec((tm, d), lambda i: (jnp.minimum(i, 1), 0))]
        out_shape = [jax.ShapeDtypeStruct((n_prompt_rows, d), F32),
                     jax.ShapeDtypeStruct((2 * n_sample_rows, d), F32)]
    else:
        out_specs = pl.BlockSpec((tm, d), lambda i: (i, 0))
        out_shape = jax.ShapeDtypeStruct((n_prompt_rows + n_sample_rows, d), F32)
    weights16 = 3 * d * d_ff * 2
    staging = 2 * (d * cw + rw * d) * 4
    vmem = weights16 + staging + 7 * tm * d * 4 + tm * d_ff * 2 + 10 * MIB
    return pl.pallas_call(
        functools.partial(_ffn_kernel, layer=layer, half=half, n_prompt_tiles=npt,
                          split_in=split_in, split_out=split_out, d_ff=d_ff, tf=FFN_TF,
                          n_stream=n_stream),
        grid=(npt + 1,),
        in_specs=in_specs,
        out_specs=out_specs,
        out_shape=out_shape,
        scratch_shapes=[
            pltpu.VMEM((d, 2 * d_ff), BF16),
            pltpu.VMEM((d_ff, d), BF16),
            pltpu.VMEM((2, d, cw), F32),
            pltpu.VMEM((2, rw, d), F32),
            pltpu.SemaphoreType.DMA((2, 2)),
            pltpu.VMEM((tm, d_ff), BF16),
        ],
        compiler_params=pltpu.CompilerParams(
            dimension_semantics=("arbitrary",), vmem_limit_bytes=vmem),
        name="ffn_half_step",
    )(*xs_in, norm_g, w_gu, w_down)


def _mixp_kernel(*refs, **statics):
    parity = lax.rem(pl.program_id(0), 2)

    @pl.when(parity == 0)
    def _():
        _mixp_body(*refs, parity=0, **statics)

    @pl.when(parity == 1)
    def _():
        _mixp_body(*refs, parity=1, **statics)


def _mixp_body(sinks_all_ref, xnext_ref, x_ref, norm_ref, win_ref, wout_ref, gout_all_ref, wsgu_ref,
               bsgu_ref, gsgu_all_ref, wconv_all_ref,
               xo_ref, ko_ref, vo_ref, co_ref,
               proj_even_ref, proj_odd_ref, kb_ref, krb_ref, vb_ref, vrb_ref, zbuf_ref, ycat_ref,
               abias_ref,
               *, parity, layer, tiles_per_seq, slopes, w_a, w_b, w_kv, w_c):
    proj_next_ref, proj_ref = ((proj_even_ref, proj_odd_ref) if parity == 0
                               else (proj_odd_ref, proj_even_ref))
    g2_ref = norm_ref.at[layer, pl.ds(2, 1)]
    g3_ref = norm_ref.at[layer, pl.ds(3, 1)]
    gout_ref = gout_all_ref.at[pl.ds(layer, 1)]
    gsgu_ref = gsgu_all_ref.at[pl.ds(layer, 1)]
    wconv_ref = wconv_all_ref.at[layer]
    step = pl.program_id(0)
    tile = jnp.maximum(step - 1, 0)
    j = lax.rem(tile, tiles_per_seq)
    t_rows = x_ref.shape[0]
    nblk = t_rows // WINDOW
    o_q = 2 * w_a
    o_k = o_q + w_b
    o_v = o_k + w_kv
    o_gb = o_v + w_kv
    o_gc = o_gb + w_c
    o_hc = o_gc + w_c

    @pl.when(step == 0)
    def _():
        proj_ref[...] = jnp.zeros(proj_ref.shape, F32)
        qi = lax.broadcasted_iota(jnp.int32, (WINDOW, 2 * WINDOW), 0)
        kk = lax.broadcasted_iota(jnp.int32, (WINDOW, 2 * WINDOW), 1)
        dist_i = WINDOW + qi - kk
        band = (dist_i >= 0) & (dist_i < WINDOW)
        dist = dist_i.astype(F32)
        for hd in range(len(slopes)):
            abias_ref[hd] = jnp.where(band, (slopes[hd] * LOG2E) * dist, -NEG)

    @pl.when(j == 0)
    def _():
        kb_ref[0:WINDOW, :] = jnp.zeros((WINDOW, w_kv), BF16)
        krb_ref[0:WINDOW, :] = jnp.zeros((WINDOW, w_kv), BF16)
        vb_ref[0:WINDOW, :] = jnp.zeros((WINDOW, w_kv), BF16)
        vrb_ref[0:WINDOW, :] = jnp.zeros((WINDOW, w_kv), BF16)
        zbuf_ref[0:8, :] = jnp.zeros((8, w_c), F32)

    hn = _rms(xnext_ref[...], g2_ref[...]).astype(BF16)
    n_pchunks = 8
    pcw = proj_next_ref.shape[1] // n_pchunks

    def project_chunks(first, count):
        for c in range(first, first + count):
            proj_next_ref[:, c * pcw:(c + 1) * pcw] = _dot(hn, win_ref[:, c * pcw:(c + 1) * pcw])

    x = x_ref[...]
    project_chunks(0, 1)

    u = _gelu(proj_ref[:, 0:w_a])
    v = _head_layer_norm(_gelu(proj_ref[:, w_a:2 * w_a]), gsgu_ref[...])
    lane_a = lax.broadcasted_iota(jnp.int32, (1, w_a), 1)
    ri = lax.broadcasted_iota(jnp.int32, (CHUNK, CHUNK), 0)
    ci = lax.broadcasted_iota(jnp.int32, (CHUNK, CHUNK), 1)
    n_heads_a = w_a // HEAD_DIM
    w_tril = [jnp.where(ri >= ci, wsgu_ref[hh], 0.0).astype(BF16) for hh in range(n_heads_a)]
    bias = bsgu_ref[...]
    ya_parts = []
    for n in range(nblk):
        vblk = v[n * CHUNK:(n + 1) * CHUNK]
        mix = bias
        for hh in range(n_heads_a):
            mh = (lane_a >= hh * HEAD_DIM) & (lane_a < (hh + 1) * HEAD_DIM)
            mix = mix + _dot(w_tril[hh], jnp.where(mh, vblk, 0.0).astype(BF16))
        ya_parts.append(u[n * CHUNK:(n + 1) * CHUNK] * mix)
    ya = jnp.concatenate(ya_parts, axis=0)
    ycat_ref[:, 0:w_a] = _rms(ya, gout_ref[:, 0:w_a]).astype(BF16)
    project_chunks(1, 1)

    lane = lax.broadcasted_iota(jnp.int32, (1, LANES), 1)
    lo = lane < HEAD_DIM
    k = proj_ref[:, o_k:o_k + w_kv]
    vv = proj_ref[:, o_v:o_v + w_kv]
    kb_ref[WINDOW:, :] = k.astype(BF16)
    krb_ref[WINDOW:, :] = pltpu.roll(k, HEAD_DIM, 1).astype(BF16)
    vb_ref[WINDOW:, :] = vv.astype(BF16)
    vrb_ref[WINDOW:, :] = pltpu.roll(vv, HEAD_DIM, 1).astype(BF16)

    kj = lax.broadcasted_iota(jnp.int32, (WINDOW, 2 * WINDOW), 1)
    has_prev = jnp.full((WINDOW, 2 * WINDOW), j, jnp.int32) > 0
    first_ok = (kj >= WINDOW) | has_prev

    n_q_cols = w_b // LANES
    half = n_q_cols // 2
    yb_parts = []
    for n in range(nblk):
        rows = slice(n * WINDOW, (n + 1) * WINDOW)
        qcols = [proj_ref[rows, o_q + c * LANES:o_q + (c + 1) * LANES] * (SCALE * LOG2E)
                 for c in range(n_q_cols)]
        keep_a = [lo if c < half else ~lo for c in range(n_q_cols)]
        lhs_a = jnp.concatenate([jnp.where(keep_a[c], qcols[c], 0.0) for c in range(n_q_cols)],
                                axis=0).astype(BF16)
        lhs_b = jnp.concatenate([jnp.where(keep_a[c], 0.0, qcols[c]) for c in range(n_q_cols)],
                                axis=0).astype(BF16)
        band_rows = slice(n * WINDOW, (n + 2) * WINDOW)
        s_a = _dot_nt(lhs_a, kb_ref[band_rows, :])
        s_b = _dot_nt(lhs_b, krb_ref[band_rows, :])
        project_chunks(2 + n, 1)
        p_a, p_b, inv_a, inv_b = [], [], [], []
        for c in range(n_q_cols):
            head_a = 2 * c if c < half else 2 * c + 1
            head_b = 2 * c + 1 if c < half else 2 * c
            for s_all, head, acc, inv in ((s_a, head_a, p_a, inv_a), (s_b, head_b, p_b, inv_b)):
                s = s_all[c * WINDOW:(c + 1) * WINDOW] - abias_ref[head]
                if n == 0:
                    s = jnp.where(first_ok, s, NEG)
                sink = sinks_all_ref[layer, head] * LOG2E
                m = jnp.maximum(jnp.max(s, axis=-1, keepdims=True), sink)
                p = jnp.exp2(s - m)
                den = jnp.sum(p, axis=-1, keepdims=True) + jnp.exp2(sink - m)
                acc.append(p.astype(BF16))
                inv.append(1.0 / den)
        o_a = _dot(jnp.concatenate(p_a, axis=0), vb_ref[band_rows, :])
        o_b = _dot(jnp.concatenate(p_b, axis=0), vrb_ref[band_rows, :])
        cols = []
        for c in range(n_q_cols):
            ra = o_a[c * WINDOW:(c + 1) * WINDOW] * inv_a[c]
            rb = o_b[c * WINDOW:(c + 1) * WINDOW] * inv_b[c]
            cols.append(jnp.where(lo, ra, rb) if c < half else jnp.where(lo, rb, ra))
        yb_parts.append(jnp.concatenate(cols, axis=1))
    yb = jnp.concatenate(yb_parts, axis=0)
    ycat_ref[:, w_a:w_a + w_b] = _rms(yb, gout_ref[:, w_a:w_a + w_b]).astype(BF16)
    kb_ref[0:WINDOW, :] = kb_ref[t_rows:t_rows + WINDOW, :]
    krb_ref[0:WINDOW, :] = krb_ref[t_rows:t_rows + WINDOW, :]
    vb_ref[0:WINDOW, :] = vb_ref[t_rows:t_rows + WINDOW, :]
    vrb_ref[0:WINDOW, :] = vrb_ref[t_rows:t_rows + WINDOW, :]

    project_chunks(2 + nblk, 1)

    z = proj_ref[:, o_gc:o_gc + w_c] * proj_ref[:, o_hc:o_hc + w_c]
    zbuf_ref[8:8 + t_rows, :] = z
    conv = zbuf_ref[8 - (CONV_W - 1):8 - (CONV_W - 1) + t_rows, :] * wconv_ref[0:1, :]
    for jj in range(1, CONV_W):
        off = 8 - (CONV_W - 1) + jj
        conv = conv + zbuf_ref[off:off + t_rows, :] * wconv_ref[jj:jj + 1, :]
    yc = proj_ref[:, o_gb:o_gb + w_c] * conv
    ycat_ref[:, w_a + w_b:] = _rms(yc, gout_ref[:, w_a + w_b:]).astype(BF16)
    z_tail = zbuf_ref[8 + t_rows - (CONV_W - 1):8 + t_rows, :]
    zbuf_ref[8 - (CONV_W - 1):8, :] = z_tail

    y = _dot(ycat_ref[...], wout_ref[...])
    project_chunks(3 + nblk, n_pchunks - 3 - nblk)
    xo_ref[...] = x + _rms(y, g3_ref[...])

    @pl.when((j == tiles_per_seq - 1) & (step > 0))
    def _():
        ko_ref[0] = proj_ref[t_rows - WINDOW:, o_k:o_k + w_kv].T
        vo_ref[0] = proj_ref[t_rows - WINDOW:, o_v:o_v + w_kv].T
        co_ref[0] = z_tail


def _mixer_prompt(x_all, layer, batch, seq, norm_g, win, wout, gout_all, wsgu_all, bsgu_full_all,
                  gsgu_all, sinks_all, wconv_all, w_a, w_b, w_kv, w_c):
    d = x_all.shape[1]
    m = batch * seq
    t = MIX_T
    assert seq % t == 0 and t % WINDOW == 0
    nt = seq // t
    in_w = win.shape[2]
    n_heads_b = w_b // HEAD_DIM
    n_tiles = batch * nt
    const2 = lambda s, *_: (0, 0)
    const3 = lambda s, *_: (0, 0, 0)
    layer3 = lambda s, *_: (layer, 0, 0)
    layer4 = lambda s, *_: (layer, 0, 0, 0)
    row_next = lambda s, *_: (jnp.minimum(s, n_tiles - 1), 0)
    row = lambda s, *_: (jnp.maximum(s - 1, 0), 0)
    per_b = lambda s, *_: (jnp.maximum(s - 1, 0) // nt, 0, 0)
    kernel = functools.partial(_mixp_kernel, layer=layer, tiles_per_seq=nt,
                               slopes=_alibi_slopes(n_heads_b),
                               w_a=w_a, w_b=w_b, w_kv=w_kv, w_c=w_c)
    grid_spec = pltpu.PrefetchScalarGridSpec(
        num_scalar_prefetch=1,
        grid=(n_tiles + 1,),
        in_specs=[
            pl.BlockSpec((t, d), row_next),
            pl.BlockSpec((t, d), row),
            pl.BlockSpec(norm_g.shape, const3),
            pl.BlockSpec((None,) + win.shape[1:], layer3, pipeline_mode=pl.Buffered(1)),
            pl.BlockSpec((None,) + wout.shape[1:], layer3, pipeline_mode=pl.Buffered(1)),
            pl.BlockSpec(gout_all.shape, const2),
            pl.BlockSpec((None,) + wsgu_all.shape[1:], layer4),
            pl.BlockSpec((None,) + bsgu_full_all.shape[1:], layer3),
            pl.BlockSpec(gsgu_all.shape, const2),
            pl.BlockSpec(wconv_all.shape, const3),
        ],
        out_specs=[
            pl.BlockSpec((t, d), row),
            pl.BlockSpec((1, w_kv, WINDOW), per_b),
            pl.BlockSpec((1, w_kv, WINDOW), per_b),
            pl.BlockSpec((1, CONV_W - 1, w_c), per_b),
        ],
        scratch_shapes=[
            pltpu.VMEM((t, in_w), F32),
            pltpu.VMEM((t, in_w), F32),
            pltpu.VMEM((t + WINDOW, w_kv), BF16),
            pltpu.VMEM((t + WINDOW, w_kv), BF16),
            pltpu.VMEM((t + WINDOW, w_kv), BF16),
            pltpu.VMEM((t + WINDOW, w_kv), BF16),
            pltpu.VMEM((t + 8, w_c), F32),
            pltpu.VMEM((t, gout_all.shape[1]), BF16),
            pltpu.VMEM((n_heads_b, WINDOW, 2 * WINDOW), F32),
        ],
    )
    depth = win.shape[0]
    vmem = (win.size + wout.size) // depth * 2 + 6 * t * d * 4 + 2 * t * in_w * 4 + 24 * MIB
    return pl.pallas_call(
        kernel,
        grid_spec=grid_spec,
        out_shape=[
            jax.ShapeDtypeStruct((m, d), F32),
            jax.ShapeDtypeStruct((batch, w_kv, WINDOW), F32),
            jax.ShapeDtypeStruct((batch, w_kv, WINDOW), F32),
            jax.ShapeDtypeStruct((batch, CONV_W - 1, w_c), F32),
        ],
        compiler_params=pltpu.CompilerParams(
            dimension_semantics=("arbitrary",), vmem_limit_bytes=vmem),
        name="mixer_prompt",
    )(sinks_all, x_all, x_all, norm_g, win, wout, gout_all, wsgu_all, bsgu_full_all, gsgu_all,
      wconv_all)


def _mixs_kernel(x_ref, norm_ref, win_ref, wout_ref, gout_all_ref, coef_ref, bias_ref, gsgu_all_ref,
                 wconv_all_ref, cexp1_ref, cexp2_ref, sinkcol_ref, kc_ref, vc_ref,
                 ks_stack_ref, vs_stack_ref,
                 xo_ref, vsgu_ref, z_ref, ko_ref, vo_ref,
                 proj_ref, qprep_ref, ybuf_ref, ycat_ref, knt_ref, vnt_ref, vtmp_ref,
                 *, layer, slopes, t_new, w_a, w_b, w_kv, w_c):
    del ks_stack_ref, vs_stack_ref
    g2_ref = norm_ref.at[layer, pl.ds(2, 1)]
    g3_ref = norm_ref.at[layer, pl.ds(3, 1)]
    gout_ref = gout_all_ref.at[pl.ds(layer, 1)]
    gsgu_ref = gsgu_all_ref.at[pl.ds(layer, 1)]
    wconv_ref = wconv_all_ref.at[layer]
    g = pl.program_id(0)
    rows_all = x_ref.shape[0]
    n_seq = rows_all // t_new
    gsz = kc_ref.shape[0]
    o_q = 2 * w_a
    o_k = o_q + w_b
    o_v = o_k + w_kv
    o_gb = o_v + w_kv
    o_gc = o_gb + w_c
    o_hc = o_gc + w_c
    n_heads = w_b // HEAD_DIM
    group = n_heads // KV_HEADS
    lane = lax.broadcasted_iota(jnp.int32, (1, LANES), 1)
    lo = lane < HEAD_DIM

    @pl.when(g == 0)
    def _():
        x = x_ref[...]
        h = _rms(x, g2_ref[...]).astype(BF16)
        proj_ref[...] = _dot(h, win_ref[...])
        tpos = lax.broadcasted_iota(jnp.int32, (rows_all, 1), 0) % t_new

        u = _gelu(proj_ref[:, 0:w_a])
        v = _head_layer_norm(_gelu(proj_ref[:, w_a:2 * w_a]), gsgu_ref[...])
        for c in range(w_a // LANES):
            vtmp_ref[c] = v[:, c * LANES:(c + 1) * LANES]
        for tt in range(t_new):
            for c in range(w_a // LANES):
                vsgu_ref[tt, c * LANES:(c + 1) * LANES, :] = (
                    vtmp_ref[c, pl.ds(tt, n_seq, stride=t_new), :].T)
        mix = bias_ref[...] + coef_ref[0] * v
        for dlt in range(1, t_new):
            mix = mix + coef_ref[dlt] * pltpu.roll(v, dlt, 0)
        ycat_ref[:, 0:w_a] = _rms(u * mix, gout_ref[:, 0:w_a]).astype(BF16)

        z = proj_ref[:, o_gc:o_gc + w_c] * proj_ref[:, o_hc:o_hc + w_c]
        z_ref[...] = z
        s2 = jnp.where(tpos >= 2, pltpu.roll(z, 2, 0), 0.0) + cexp2_ref[...]
        s1 = jnp.where(tpos >= 1, pltpu.roll(z, 1, 0), 0.0) + cexp1_ref[...]
        conv = s2 * wconv_ref[0:1, :] + s1 * wconv_ref[1:2, :] + z * wconv_ref[2:3, :]
        yc = proj_ref[:, o_gb:o_gb + w_c] * conv
        ycat_ref[:, w_a + w_b:] = _rms(yc, gout_ref[:, w_a + w_b:]).astype(BF16)

        for hd in range(n_heads):
            c, hf, kvh = hd // 2, hd % 2, hd // group
            piece = proj_ref[:, o_q + c * LANES:o_q + (c + 1) * LANES]
            if hf != kvh:
                piece = pltpu.roll(piece, HEAD_DIM, 1)
            qprep_ref[hd] = jnp.where(lo if kvh == 0 else ~lo, piece * (SCALE * LOG2E), 0.0)

        for c in range(rows_all // LANES):
            knt_ref[c] = proj_ref[c * LANES:(c + 1) * LANES, o_k:o_k + w_kv].T
            vnt_ref[c] = proj_ref[c * LANES:(c + 1) * LANES, o_v:o_v + w_kv].T

    pair_rows = 2 * t_new
    n_pairs = gsz // 2
    lrows = n_heads * pair_rows
    ri = lax.broadcasted_iota(jnp.int32, (lrows, 1), 0)
    r_in = ri % pair_rows
    seq_q = r_in // t_new
    t_q = r_in % t_new
    cj = lax.broadcasted_iota(jnp.int32, (1, 2 * WINDOW), 1)
    valid_c = (cj // WINDOW == seq_q) & (cj % WINDOW >= t_q + 1)
    dist_c = (WINDOW + t_q - cj % WINDOW).astype(F32)
    nj = lax.broadcasted_iota(jnp.int32, (1, LANES), 1)
    valid_n = (nj < pair_rows) & (nj // t_new == seq_q) & (nj % t_new <= t_q)
    dist_n = (t_q - nj % t_new).astype(F32)
    slope_col = jnp.zeros((lrows, 1), F32)
    for hd in range(n_heads):
        slope_col = jnp.where(ri // pair_rows == hd, slopes[hd], slope_col)
    sink = sinkcol_ref[...] * LOG2E
    bias_c = jnp.where(valid_c, (slope_col * LOG2E) * dist_c, -NEG)
    bias_n = jnp.where(valid_n, (slope_col * LOG2E) * dist_n, -NEG)
    zpad = jnp.zeros((LANES - pair_rows, w_kv), F32)

    keep = lane < WINDOW - t_new
    kn_t = knt_ref[g]
    vn_t = vnt_ref[g]

    def slide(src_ref, dst_ref, new_t, i):
        shift = (WINDOW - t_new - t_new * i) % LANES
        new = pltpu.roll(new_t, shift, 1) if shift else new_t
        dst_ref[i] = jnp.where(keep, pltpu.roll(src_ref[i], WINDOW - t_new, 1), new)

    r0 = pl.multiple_of(g * (n_pairs * pair_rows), n_pairs * pair_rows)
    sc_parts, sn_parts = [], []
    for p in range(n_pairs):
        rows = pl.ds(r0 + p * pair_rows, pair_rows)
        lhs = jnp.concatenate([qprep_ref[hd, rows, :] for hd in range(n_heads)], axis=0).astype(BF16)
        kn = proj_ref[rows, o_k:o_k + w_kv]
        kc = jnp.concatenate([kc_ref[2 * p], kc_ref[2 * p + 1]], axis=1)
        sc_parts.append(_dot(lhs, kc.astype(BF16)))
        sn_parts.append(_dot_nt(lhs, jnp.concatenate([kn, zpad], axis=0).astype(BF16)))
        slide(kc_ref, ko_ref, kn_t, 2 * p)
        slide(kc_ref, ko_ref, kn_t, 2 * p + 1)
    s_c = jnp.stack(sc_parts) - bias_c[None]
    s_n = jnp.stack(sn_parts) - bias_n[None]
    m = jnp.maximum(jnp.maximum(jnp.max(s_c, axis=-1, keepdims=True),
                                jnp.max(s_n, axis=-1, keepdims=True)), sink[None])
    p_c = jnp.exp2(s_c - m)
    p_n = jnp.exp2(s_n - m)
    den = (jnp.sum(p_c, axis=-1, keepdims=True) + jnp.sum(p_n, axis=-1, keepdims=True)
           + jnp.exp2(sink[None] - m))
    inv = 1.0 / den
    p_c = p_c.astype(BF16)
    p_n = p_n.astype(BF16)
    for p in range(n_pairs):
        rows = pl.ds(r0 + p * pair_rows, pair_rows)
        vn = proj_ref[rows, o_v:o_v + w_kv]
        vc = jnp.concatenate([vc_ref[2 * p], vc_ref[2 * p + 1]], axis=1)
        slide(vc_ref, vo_ref, vn_t, 2 * p)
        slide(vc_ref, vo_ref, vn_t, 2 * p + 1)
        o = (_dot_nt(p_c[p], vc.astype(BF16))
             + _dot(p_n[p], jnp.concatenate([vn, zpad], axis=0).astype(BF16))) * inv[p]
        cols = []
        for c in range(n_heads // 2):
            pieces = []
            for hf in range(2):
                hd = 2 * c + hf
                piece = o[hd * pair_rows:(hd + 1) * pair_rows]
                if hd // group != hf:
                    piece = pltpu.roll(piece, HEAD_DIM, 1)
                pieces.append(piece)
            cols.append(jnp.where(lo, pieces[0], pieces[1]))
        ybuf_ref[rows, :] = jnp.concatenate(cols, axis=1)

    @pl.when(g == pl.num_programs(0) - 1)
    def _():
        ycat_ref[:, w_a:w_a + w_b] = _rms(ybuf_ref[...], gout_ref[:, w_a:w_a + w_b]).astype(BF16)
        y = _dot(ycat_ref[...], wout_ref[...])
        xo_ref[...] = x_ref[...] + _rms(y, g3_ref[...])


def _mixer_sample(x_all, layer, n_prompt_rows, n_seq, t_new, norm_g, win, wout, gout_all, coef_all,
                  bias_all, gsgu_all, wconv_all, cexp1_all, cexp2_all, sink_col_all, kc_all, vc_all,
                  ks_stack, vs_stack, w_a, w_b, w_kv, w_c):
    d = x_all.shape[1]
    m = n_seq * t_new
    assert n_prompt_rows % m == 0
    in_w = win.shape[2]
    n_heads = w_b // HEAD_DIM
    gsz = LANES // t_new
    assert n_seq % gsz == 0 and gsz % 2 == 0 and 2 * t_new == 8 and WINDOW == LANES
    ng = n_seq // gsz
    const2 = lambda g: (0, 0)
    const3 = lambda g: (0, 0, 0)
    layer3 = lambda g: (layer, 0, 0)
    layer4 = lambda g: (layer, 0, 0, 0)
    step_in = lambda g: (layer, g, 0, 0)
    kernel = functools.partial(_mixs_kernel, layer=layer, slopes=_alibi_slopes(n_heads),
                               t_new=t_new, w_a=w_a, w_b=w_b, w_kv=w_kv, w_c=w_c)
    coef_bytes = coef_all.size // coef_all.shape[0] * 4
    depth = win.shape[0]
    vmem = ((win.size + wout.size) // depth * 2 + 4 * m * d * 4 + m * in_w * 4
            + 8 * gsz * WINDOW * w_kv * 4 + coef_bytes * 2 + 24 * MIB)
    return pl.pallas_call(
        kernel,
        grid=(ng,),
        in_specs=[
            pl.BlockSpec((m, d), lambda g: (n_prompt_rows // m, 0)),
            pl.BlockSpec(norm_g.shape, const3),
            pl.BlockSpec((None,) + win.shape[1:], layer3, pipeline_mode=pl.Buffered(1)),
            pl.BlockSpec((None,) + wout.shape[1:], layer3, pipeline_mode=pl.Buffered(1)),
            pl.BlockSpec(gout_all.shape, const2),
            pl.BlockSpec((None,) + coef_all.shape[1:], layer4),
            pl.BlockSpec((None,) + bias_all.shape[1:], layer3),
            pl.BlockSpec(gsgu_all.shape, const2),
            pl.BlockSpec(wconv_all.shape, const3),
            pl.BlockSpec((None,) + cexp1_all.shape[1:], layer3),
            pl.BlockSpec((None,) + cexp2_all.shape[1:], layer3),
            pl.BlockSpec((None,) + sink_col_all.shape[1:], layer3),
            pl.BlockSpec((None, gsz, w_kv, WINDOW), step_in),
            pl.BlockSpec((None, gsz, w_kv, WINDOW), step_in),
            pl.BlockSpec(memory_space=pl.ANY),
            pl.BlockSpec(memory_space=pl.ANY),
        ],
        out_specs=[
            pl.BlockSpec((m, d), const2),
            pl.BlockSpec((t_new, w_a, n_seq), const3),
            pl.BlockSpec((m, w_c), const2),
            pl.BlockSpec((None, gsz, w_kv, WINDOW), step_in),
            pl.BlockSpec((None, gsz, w_kv, WINDOW), step_in),
        ],
        out_shape=[
            jax.ShapeDtypeStruct((m, d), F32),
            jax.ShapeDtypeStruct((t_new, w_a, n_seq), F32),
            jax.ShapeDtypeStruct((m, w_c), F32),
            jax.ShapeDtypeStruct(ks_stack.shape, F32),
            jax.ShapeDtypeStruct(vs_stack.shape, F32),
        ],
        input_output_aliases={14: 3, 15: 4},
        scratch_shapes=[
            pltpu.VMEM((m, in_w), F32),
            pltpu.VMEM((n_heads, m, LANES), F32),
            pltpu.VMEM((m, w_b), F32),
            pltpu.VMEM((m, gout_all.shape[1]), BF16),
            pltpu.VMEM((ng, w_kv, LANES), F32),
            pltpu.VMEM((ng, w_kv, LANES), F32),
            pltpu.VMEM((w_a // LANES, m, LANES), F32),
        ],
        compiler_params=pltpu.CompilerParams(
            dimension_semantics=("arbitrary",), vmem_limit_bytes=vmem),
        name="mixer_sample",
    )(x_all, norm_g, win, wout, gout_all, coef_all, bias_all, gsgu_all, wconv_all, cexp1_all,
      cexp2_all, sink_col_all, kc_all, vc_all, ks_stack, vs_stack)


def kernel(x_prompt, x_sample, cache_swa_k, cache_swa_v, cache_conv, norm_g, w_ffn_gu, w_ffn_down,
           w_mix_in, w_mix_out, g_mix_out, w_sgu, b_sgu, g_sgu, attn_sinks, w_conv):
    batch, seq, d = x_prompt.shape
    n_seq, t_new, _ = x_sample.shape
    depth = norm_g.shape[0]
    w_a = g_sgu.shape[1]
    w_c = w_conv.shape[2]
    w_kv = KV_HEADS * HEAD_DIM
    w_b = g_mix_out.shape[1] - w_a - w_c
    n_heads_a = w_a // HEAD_DIM
    assert t_new <= CHUNK and seq % CHUNK == 0

    xp = x_prompt.reshape(batch * seq, d)
    xs = x_sample.reshape(n_seq * t_new, d)
    to_dp = lambda c: jnp.transpose(c, (0, 1, 3, 4, 2)).reshape(depth, c.shape[1], w_kv, c.shape[2])
    from_dp = lambda c: jnp.transpose(
        c.reshape(depth, c.shape[1], KV_HEADS, HEAD_DIM, c.shape[3]), (0, 1, 4, 2, 3))
    kc_all = to_dp(cache_swa_k)
    vc_all = to_dp(cache_swa_v)

    win_all = w_mix_in.astype(BF16)
    wout_all = w_mix_out.astype(BF16)
    bsgu_full_all = jnp.repeat(jnp.swapaxes(b_sgu, 1, 2), HEAD_DIM, axis=2)
    w4 = jnp.tril(w_sgu[:, :, :t_new, :t_new])
    tt = np.arange(t_new)
    coef_all = jnp.stack([
        jnp.where((tt >= dlt)[None, None, :], w4[:, :, tt, np.maximum(tt - dlt, 0)], 0.0)
        for dlt in range(t_new)], axis=1)
    coef_all = jnp.repeat(coef_all.transpose(0, 1, 3, 2), HEAD_DIM, axis=3)
    coef_all = jnp.tile(coef_all, (1, 1, n_seq, 1))
    bias_all = jnp.tile(jnp.repeat(jnp.swapaxes(b_sgu[:, :, :t_new], 1, 2), HEAD_DIM, axis=2),
                        (1, n_seq, 1))
    pad_t = lambda c: jnp.pad(c, ((0, 0), (0, 0), (0, t_new - c.shape[2]), (0, 0))).reshape(
        depth, n_seq * t_new, w_c)
    cexp2_all = pad_t(cache_conv)
    cexp1_all = pad_t(cache_conv[:, :, 1:])
    sink_col_all = jnp.repeat(attn_sinks, 2 * t_new, axis=1)[:, :, None]

    n_prompt_rows = batch * seq
    outs = {k: [] for k in ("sgu", "kp", "vp", "cp", "cs")}
    ks = jnp.zeros(kc_all.shape, F32)
    vs = jnp.zeros(vc_all.shape, F32)
    x_in = (xp, xs)
    for l in range(depth):
        x_all = _ffn(x_in, norm_g, w_ffn_gu, w_ffn_down, l, 0, n_prompt_rows, False)
        xp, kp, vp, cp = _mixer_prompt(
            x_all, l, batch, seq, norm_g, win_all, wout_all, g_mix_out, w_sgu, bsgu_full_all,
            g_sgu, attn_sinks, w_conv, w_a, w_b, w_kv, w_c)
        xs, vsgu, z_s, ks, vs = _mixer_sample(
            x_all, l, n_prompt_rows, n_seq, t_new, norm_g, win_all, wout_all, g_mix_out,
            coef_all, bias_all, g_sgu, w_conv, cexp1_all, cexp2_all, sink_col_all, kc_all, vc_all,
            ks, vs, w_a, w_b, w_kv, w_c)
        last = l == depth - 1
        res = _ffn((xp, xs), norm_g, w_ffn_gu, w_ffn_down, l, 1, n_prompt_rows, last)
        if last:
            xp, xs = res[0], res[1][:n_seq * t_new]
        else:
            x_in = (res,)

        outs["sgu"].append(vsgu)
        outs["kp"].append(kp)
        outs["vp"].append(vp)
        outs["cp"].append(cp)
        outs["cs"].append(z_s.reshape(n_seq, t_new, w_c)[:, t_new - (CONV_W - 1):])
    sgu = jnp.transpose(
        jnp.stack(outs["sgu"]).reshape(depth, t_new, n_heads_a, HEAD_DIM, n_seq), (0, 4, 1, 2, 3))
    return (xp.reshape(batch, seq, d), xs.reshape(n_seq, t_new, d), sgu,
            from_dp(jnp.stack(outs["kp"])), from_dp(jnp.stack(outs["vp"])),
            from_dp(ks), from_dp(vs),
            jnp.stack(outs["cp"]), jnp.stack(outs["cs"]))
```

```python
import functools

import numpy as np
import jax
import jax.numpy as jnp
from jax import lax
from jax.experimental import pallas as pl
from jax.experimental.pallas import tpu as pltpu

F32 = jnp.float32
BF16 = jnp.bfloat16

HEAD_DIM = 64
KV_HEADS = 2
WINDOW = 128
CHUNK = 128
CONV_W = 3
EPS = 1e-6
NEG = -1e30
SCALE = HEAD_DIM ** -0.5
LOG2E = float(np.log2(np.e))
LANES = 128

FFN_TM = 1024
FFN_SUB = 256
FFN_TF = 256
MIX_T = 512
MIB = 1024 * 1024


def _rms(x, g):
    return x * lax.rsqrt(jnp.mean(x * x, axis=-1, keepdims=True) + EPS) * g


def _gelu(x):
    return 0.5 * x * (1.0 + lax.erf(x * np.float32(np.sqrt(0.5))))


def _head_layer_norm(x, g):
    lane = lax.broadcasted_iota(jnp.int32, (1, x.shape[1]), 1)
    out = jnp.zeros_like(x)
    for hh in range(x.shape[1] // HEAD_DIM):
        m = (lane >= hh * HEAD_DIM) & (lane < (hh + 1) * HEAD_DIM)
        mu = jnp.sum(jnp.where(m, x, 0.0), axis=-1, keepdims=True) / HEAD_DIM
        d = jnp.where(m, x - mu, 0.0)
        var = jnp.sum(d * d, axis=-1, keepdims=True) / HEAD_DIM
        out = out + d * lax.rsqrt(var + EPS)
    return out * g


def _dot(a, b):
    return jnp.dot(a, b, preferred_element_type=F32)


def _dot_nt(a, b):
    return lax.dot_general(a, b, (((1,), (1,)), ((), ())), preferred_element_type=F32)


def _alibi_slopes(n_heads):
    return [float(2.0 ** (-8.0 * h / n_heads)) for h in range(1, n_heads + 1)]


def _ffn_kernel(xp_ref, xs_ref, norm_ref, wgu_hbm, wdn_hbm, op_ref, os_ref,
                wgu16_ref, wdn16_ref, stage_gu_ref, stage_dn_ref, sem_ref, act_ref,
                *, layer, half, d_ff, tf, n_stream, sub_rows):
    i = pl.program_id(0)
    cw = stage_gu_ref.shape[2]
    rw = stage_dn_ref.shape[1]

    def gu_copy(c, slot):
        return pltpu.make_async_copy(wgu_hbm.at[layer, half, :, pl.ds(c * cw, cw)],
                                     stage_gu_ref.at[slot], sem_ref.at[0, slot])

    def dn_copy(c, slot):
        return pltpu.make_async_copy(wdn_hbm.at[layer, half, pl.ds(c * rw, rw), :],
                                     stage_dn_ref.at[slot], sem_ref.at[1, slot])

    @pl.when(i == 0)
    def _():
        gu_copy(0, 0).start()
        dn_copy(0, 0).start()
        for c in range(n_stream):
            slot = c % 2
            if c + 1 < n_stream:
                gu_copy(c + 1, 1 - slot).start()
                dn_copy(c + 1, 1 - slot).start()
            gu_copy(c, slot).wait()
            dn_copy(c, slot).wait()
            wgu16_ref[:, c * cw:(c + 1) * cw] = stage_gu_ref[slot].astype(BF16)
            wdn16_ref[c * rw:(c + 1) * rw, :] = stage_dn_ref[slot].astype(BF16)

    g_pre = norm_ref[layer, 4 * half:4 * half + 1, :]
    g_post = norm_ref[layer, 4 * half + 1:4 * half + 2, :]
    n_chunks = d_ff // tf

    def hidden_chunks(h, rows, first, count):
        for c in range(first, first + count):
            gate = _dot(h, wgu16_ref[:, c * tf:(c + 1) * tf])
            up = _dot(h, wgu16_ref[:, d_ff + c * tf:d_ff + (c + 1) * tf])
            act_ref[rows, c * tf:(c + 1) * tf] = (jax.nn.silu(gate) * up).astype(BF16)

    def run_tile(x_ref, o_ref):
        n_sub = x_ref.shape[0] // sub_rows
        rows = [slice(k * sub_rows, (k + 1) * sub_rows) for k in range(n_sub)]
        h_next = _rms(x_ref[rows[0], :], g_pre).astype(BF16)
        y_prev = None
        for k in range(n_sub):
            h = h_next
            arows = slice((k % 2) * sub_rows, (k % 2 + 1) * sub_rows)
            hidden_chunks(h, arows, 0, 2)
            if k + 1 < n_sub:
                h_next = _rms(x_ref[rows[k + 1], :], g_pre).astype(BF16)
            hidden_chunks(h, arows, 2, 1)
            if k > 0:
                o_ref[rows[k - 1], :] = x_ref[rows[k - 1], :] + 0.5 * _rms(y_prev, g_post)
            hidden_chunks(h, arows, 3, n_chunks - 3)
            y_prev = _dot(act_ref[arows, :], wdn16_ref[...])
        o_ref[rows[-1], :] = x_ref[rows[-1], :] + 0.5 * _rms(y_prev, g_post)

    @pl.when(i == 0)
    def _():
        run_tile(xs_ref, os_ref)

    @pl.when(i > 0)
    def _():
        run_tile(xp_ref, op_ref)


def _ffn(xp, xs, norm_g, w_gu, w_down, layer, half):
    n_prompt_rows, d = xp.shape
    n_sample_rows = xs.shape[0]
    d_ff = w_down.shape[2]
    tm = FFN_TM
    sub = FFN_SUB
    assert n_prompt_rows % tm == 0 and tm % sub == 0 and n_sample_rows % sub == 0
    assert d_ff % FFN_TF == 0
    npt = n_prompt_rows // tm
    n_stream = d_ff // FFN_TF
    cw = 2 * d_ff // n_stream
    rw = d_ff // n_stream
    assert cw % LANES == 0 and rw % 8 == 0
    prompt_tile = lambda i: (jnp.maximum(i - 1, 0), 0)
    const = lambda i: (0, 0)
    in_specs = [pl.BlockSpec((tm, d), prompt_tile),
                pl.BlockSpec((n_sample_rows, d), const, pipeline_mode=pl.Buffered(1)),
                pl.BlockSpec(norm_g.shape, lambda i: (0, 0, 0)),
                pl.BlockSpec(memory_space=pl.ANY),
                pl.BlockSpec(memory_space=pl.ANY)]
    out_specs = [pl.BlockSpec((tm, d), prompt_tile), pl.BlockSpec((n_sample_rows, d), const)]
    out_shape = [jax.ShapeDtypeStruct((n_prompt_rows, d), F32),
                 jax.ShapeDtypeStruct((n_sample_rows, d), F32)]
    weights16 = 3 * d * d_ff * 2
    staging = 2 * (d * cw + rw * d) * 4
    windows = (4 * tm + 3 * n_sample_rows) * d * 4
    vmem = weights16 + staging + windows + 2 * sub * d_ff * 2 + 10 * MIB
    return pl.pallas_call(
        functools.partial(_ffn_kernel, layer=layer, half=half, d_ff=d_ff, tf=FFN_TF,
                          n_stream=n_stream, sub_rows=sub),
        grid=(npt + 1,),
        in_specs=in_specs,
        out_specs=out_specs,
        out_shape=out_shape,
        scratch_shapes=[
            pltpu.VMEM((d, 2 * d_ff), BF16),
            pltpu.VMEM((d_ff, d), BF16),
            pltpu.VMEM((2, d, cw), F32),
            pltpu.VMEM((2, rw, d), F32),
            pltpu.SemaphoreType.DMA((2, 2)),
            pltpu.VMEM((2 * sub, d_ff), BF16),
        ],
        compiler_params=pltpu.CompilerParams(
            dimension_semantics=("arbitrary",), vmem_limit_bytes=vmem),
        name="ffn_half_step",
    )(xp, xs, norm_g, w_gu, w_down)


def _mixp_kernel(*refs, **statics):
    parity = lax.rem(pl.program_id(0), 2)

    @pl.when(parity == 0)
    def _():
        _mixp_body(*refs, parity=0, **statics)

    @pl.when(parity == 1)
    def _():
        _mixp_body(*refs, parity=1, **statics)


def _mixp_body(sinks_all_ref, xnext_ref, x_ref, norm_ref, win_ref, wout_ref, gout_all_ref, wsgu_ref,
               bsgu_ref, gsgu_all_ref, wconv_all_ref,
               xo_ref, ko_ref, vo_ref, co_ref,
               proj_even_ref, proj_odd_ref, kb_ref, krb_ref, vb_ref, vrb_ref, zbuf_ref, ycat_ref,
               abias_ref,
               *, parity, layer, tiles_per_seq, slopes, w_a, w_b, w_kv, w_c):
    proj_next_ref, proj_ref = ((proj_even_ref, proj_odd_ref) if parity == 0
                               else (proj_odd_ref, proj_even_ref))
    g2_ref = norm_ref.at[layer, pl.ds(2, 1)]
    g3_ref = norm_ref.at[layer, pl.ds(3, 1)]
    gout_ref = gout_all_ref.at[pl.ds(layer, 1)]
    gsgu_ref = gsgu_all_ref.at[pl.ds(layer, 1)]
    wconv_ref = wconv_all_ref.at[layer]
    step = pl.program_id(0)
    tile = jnp.maximum(step - 1, 0)
    j = lax.rem(tile, tiles_per_seq)
    t_rows = x_ref.shape[0]
    nblk = t_rows // WINDOW
    o_q = 2 * w_a
    o_k = o_q + w_b
    o_v = o_k + w_kv
    o_gb = o_v + w_kv
    o_gc = o_gb + w_c
    o_hc = o_gc + w_c

    @pl.when(step == 0)
    def _():
        proj_ref[...] = jnp.zeros(proj_ref.shape, F32)
        qi = lax.broadcasted_iota(jnp.int32, (WINDOW, 2 * WINDOW), 0)
        kk = lax.broadcasted_iota(jnp.int32, (WINDOW, 2 * WINDOW), 1)
        dist_i = WINDOW + qi - kk
        band = (dist_i >= 0) & (dist_i < WINDOW)
        dist = dist_i.astype(F32)
        for hd in range(len(slopes)):
            abias_ref[hd] = jnp.where(band, (slopes[hd] * LOG2E) * dist, -NEG)

    @pl.when(j == 0)
    def _():
        kb_ref[0:WINDOW, :] = jnp.zeros((WINDOW, w_kv), BF16)
        krb_ref[0:WINDOW, :] = jnp.zeros((WINDOW, w_kv), BF16)
        vb_ref[0:WINDOW, :] = jnp.zeros((WINDOW, w_kv), BF16)
        vrb_ref[0:WINDOW, :] = jnp.zeros((WINDOW, w_kv), BF16)
        zbuf_ref[0:8, :] = jnp.zeros((8, w_c), F32)

    hn = _rms(xnext_ref[...], g2_ref[...]).astype(BF16)
    n_pchunks = 8
    pcw = proj_next_ref.shape[1] // n_pchunks

    def project_chunks(first, count):
        for c in range(first, first + count):
            proj_next_ref[:, c * pcw:(c + 1) * pcw] = _dot(hn, win_ref[:, c * pcw:(c + 1) * pcw])

    x = x_ref[...]
    project_chunks(0, 1)

    u = _gelu(proj_ref[:, 0:w_a])
    v = _head_layer_norm(_gelu(proj_ref[:, w_a:2 * w_a]), gsgu_ref[...])
    lane_a = lax.broadcasted_iota(jnp.int32, (1, w_a), 1)
    ri = lax.broadcasted_iota(jnp.int32, (CHUNK, CHUNK), 0)
    ci = lax.broadcasted_iota(jnp.int32, (CHUNK, CHUNK), 1)
    n_heads_a = w_a // HEAD_DIM
    w_tril = [jnp.where(ri >= ci, wsgu_ref[hh], 0.0).astype(BF16) for hh in range(n_heads_a)]
    bias = bsgu_ref[...]
    ya_parts = []
    for n in range(nblk):
        vblk = v[n * CHUNK:(n + 1) * CHUNK]
        mix = bias
        for hh in range(n_heads_a):
            mh = (lane_a >= hh * HEAD_DIM) & (lane_a < (hh + 1) * HEAD_DIM)
            mix = mix + _dot(w_tril[hh], jnp.where(mh, vblk, 0.0).astype(BF16))
        ya_parts.append(u[n * CHUNK:(n + 1) * CHUNK] * mix)
    ya = jnp.concatenate(ya_parts, axis=0)
    ycat_ref[:, 0:w_a] = _rms(ya, gout_ref[:, 0:w_a]).astype(BF16)
    project_chunks(1, 1)

    lane = lax.broadcasted_iota(jnp.int32, (1, LANES), 1)
    lo = lane < HEAD_DIM
    k = proj_ref[:, o_k:o_k + w_kv]
    vv = proj_ref[:, o_v:o_v + w_kv]
    kb_ref[WINDOW:, :] = k.astype(BF16)
    krb_ref[WINDOW:, :] = pltpu.roll(k, HEAD_DIM, 1).astype(BF16)
    vb_ref[WINDOW:, :] = vv.astype(BF16)
    vrb_ref[WINDOW:, :] = pltpu.roll(vv, HEAD_DIM, 1).astype(BF16)

    kj = lax.broadcasted_iota(jnp.int32, (WINDOW, 2 * WINDOW), 1)
    has_prev = jnp.full((WINDOW, 2 * WINDOW), j, jnp.int32) > 0
    first_ok = (kj >= WINDOW) | has_prev

    n_q_cols = w_b // LANES
    half = n_q_cols // 2
    yb_parts = []
    for n in range(nblk):
        rows = slice(n * WINDOW, (n + 1) * WINDOW)
        qcols = [proj_ref[rows, o_q + c * LANES:o_q + (c + 1) * LANES] * (SCALE * LOG2E)
                 for c in range(n_q_cols)]
        keep_a = [lo if c < half else ~lo for c in range(n_q_cols)]
        lhs_a = jnp.concatenate([jnp.where(keep_a[c], qcols[c], 0.0) for c in range(n_q_cols)],
                                axis=0).astype(BF16)
        lhs_b = jnp.concatenate([jnp.where(keep_a[c], 0.0, qcols[c]) for c in range(n_q_cols)],
                                axis=0).astype(BF16)
        band_rows = slice(n * WINDOW, (n + 2) * WINDOW)
        s_a = _dot_nt(lhs_a, kb_ref[band_rows, :])
        s_b = _dot_nt(lhs_b, krb_ref[band_rows, :])
        project_chunks(2 + n, 1)
        p_a, p_b, inv_a, inv_b = [], [], [], []
        for c in range(n_q_cols):
            head_a = 2 * c if c < half else 2 * c + 1
            head_b = 2 * c + 1 if c < half else 2 * c
            for s_all, head, acc, inv in ((s_a, head_a, p_a, inv_a), (s_b, head_b, p_b, inv_b)):
                s = s_all[c * WINDOW:(c + 1) * WINDOW] - abias_ref[head]
                if n == 0:
                    s = jnp.where(first_ok, s, NEG)
                sink = sinks_all_ref[layer, head] * LOG2E
                m = jnp.maximum(jnp.max(s, axis=-1, keepdims=True), sink)
                p = jnp.exp2(s - m)
                den = jnp.sum(p, axis=-1, keepdims=True) + jnp.exp2(sink - m)
                acc.append(p.astype(BF16))
                inv.append(1.0 / den)
        o_a = _dot(jnp.concatenate(p_a, axis=0), vb_ref[band_rows, :])
        o_b = _dot(jnp.concatenate(p_b, axis=0), vrb_ref[band_rows, :])
        cols = []
        for c in range(n_q_cols):
            ra = o_a[c * WINDOW:(c + 1) * WINDOW] * inv_a[c]
            rb = o_b[c * WINDOW:(c + 1) * WINDOW] * inv_b[c]
            cols.append(jnp.where(lo, ra, rb) if c < half else jnp.where(lo, rb, ra))
        yb_parts.append(jnp.concatenate(cols, axis=1))
    yb = jnp.concatenate(yb_parts, axis=0)
    ycat_ref[:, w_a:w_a + w_b] = _rms(yb, gout_ref[:, w_a:w_a + w_b]).astype(BF16)
    kb_ref[0:WINDOW, :] = kb_ref[t_rows:t_rows + WINDOW, :]
    krb_ref[0:WINDOW, :] = krb_ref[t_rows:t_rows + WINDOW, :]
    vb_ref[0:WINDOW, :] = vb_ref[t_rows:t_rows + WINDOW, :]
    vrb_ref[0:WINDOW, :] = vrb_ref[t_rows:t_rows + WINDOW, :]

    project_chunks(2 + nblk, 1)

    z = proj_ref[:, o_gc:o_gc + w_c] * proj_ref[:, o_hc:o_hc + w_c]
    zbuf_ref[8:8 + t_rows, :] = z
    conv = zbuf_ref[8 - (CONV_W - 1):8 - (CONV_W - 1) + t_rows, :] * wconv_ref[0:1, :]
    for jj in range(1, CONV_W):
        off = 8 - (CONV_W - 1) + jj
        conv = conv + zbuf_ref[off:off + t_rows, :] * wconv_ref[jj:jj + 1, :]
    yc = proj_ref[:, o_gb:o_gb + w_c] * conv
    ycat_ref[:, w_a + w_b:] = _rms(yc, gout_ref[:, w_a + w_b:]).astype(BF16)
    z_tail = zbuf_ref[8 + t_rows - (CONV_W - 1):8 + t_rows, :]
    zbuf_ref[8 - (CONV_W - 1):8, :] = z_tail

    y = _dot(ycat_ref[...], wout_ref[...])
    project_chunks(3 + nblk, n_pchunks - 3 - nblk)
    xo_ref[...] = x + _rms(y, g3_ref[...])

    @pl.when((j == tiles_per_seq - 1) & (step > 0))
    def _():
        ko_ref[0] = proj_ref[t_rows - WINDOW:, o_k:o_k + w_kv].T
        vo_ref[0] = proj_ref[t_rows - WINDOW:, o_v:o_v + w_kv].T
        co_ref[0] = z_tail


def _mixer_prompt(x_all, layer, batch, seq, norm_g, win, wout, gout_all, wsgu_all, bsgu_full_all,
                  gsgu_all, sinks_all, wconv_all, w_a, w_b, w_kv, w_c):
    d = x_all.shape[1]
    m = batch * seq
    t = MIX_T
    assert seq % t == 0 and t % WINDOW == 0
    nt = seq // t
    in_w = win.shape[2]
    n_heads_b = w_b // HEAD_DIM
    n_tiles = batch * nt
    const2 = lambda s, *_: (0, 0)
    const3 = lambda s, *_: (0, 0, 0)
    layer3 = lambda s, *_: (layer, 0, 0)
    layer4 = lambda s, *_: (layer, 0, 0, 0)
    row_next = lambda s, *_: (jnp.minimum(s, n_tiles - 1), 0)
    row = lambda s, *_: (jnp.maximum(s - 1, 0), 0)
    per_b = lambda s, *_: (jnp.maximum(s - 1, 0) // nt, 0, 0)
    kernel = functools.partial(_mixp_kernel, layer=layer, tiles_per_seq=nt,
                               slopes=_alibi_slopes(n_heads_b),
                               w_a=w_a, w_b=w_b, w_kv=w_kv, w_c=w_c)
    grid_spec = pltpu.PrefetchScalarGridSpec(
        num_scalar_prefetch=1,
        grid=(n_tiles + 1,),
        in_specs=[
            pl.BlockSpec((t, d), row_next),
            pl.BlockSpec((t, d), row),
            pl.BlockSpec(norm_g.shape, const3),
            pl.BlockSpec((None,) + win.shape[1:], layer3, pipeline_mode=pl.Buffered(1)),
            pl.BlockSpec((None,) + wout.shape[1:], layer3, pipeline_mode=pl.Buffered(1)),
            pl.BlockSpec(gout_all.shape, const2),
            pl.BlockSpec((None,) + wsgu_all.shape[1:], layer4),
            pl.BlockSpec((None,) + bsgu_full_all.shape[1:], layer3),
            pl.BlockSpec(gsgu_all.shape, const2),
            pl.BlockSpec(wconv_all.shape, const3),
        ],
        out_specs=[
            pl.BlockSpec((t, d), row),
            pl.BlockSpec((1, w_kv, WINDOW), per_b),
            pl.BlockSpec((1, w_kv, WINDOW), per_b),
            pl.BlockSpec((1, CONV_W - 1, w_c), per_b),
        ],
        scratch_shapes=[
            pltpu.VMEM((t, in_w), F32),
            pltpu.VMEM((t, in_w), F32),
            pltpu.VMEM((t + WINDOW, w_kv), BF16),
            pltpu.VMEM((t + WINDOW, w_kv), BF16),
            pltpu.VMEM((t + WINDOW, w_kv), BF16),
            pltpu.VMEM((t + WINDOW, w_kv), BF16),
            pltpu.VMEM((t + 8, w_c), F32),
            pltpu.VMEM((t, gout_all.shape[1]), BF16),
            pltpu.VMEM((n_heads_b, WINDOW, 2 * WINDOW), F32),
        ],
    )
    depth = win.shape[0]
    vmem = (win.size + wout.size) // depth * 2 + 6 * t * d * 4 + 2 * t * in_w * 4 + 24 * MIB
    return pl.pallas_call(
        kernel,
        grid_spec=grid_spec,
        out_shape=[
            jax.ShapeDtypeStruct((m, d), F32),
            jax.ShapeDtypeStruct((batch, w_kv, WINDOW), F32),
            jax.ShapeDtypeStruct((batch, w_kv, WINDOW), F32),
            jax.ShapeDtypeStruct((batch, CONV_W - 1, w_c), F32),
        ],
        compiler_params=pltpu.CompilerParams(
            dimension_semantics=("arbitrary",), vmem_limit_bytes=vmem),
        name="mixer_prompt",
    )(sinks_all, x_all, x_all, norm_g, win, wout, gout_all, wsgu_all, bsgu_full_all, gsgu_all,
      wconv_all)


def _mixs_kernel(x_ref, norm_ref, win_ref, wout_ref, gout_all_ref, coef_ref, bias_ref, gsgu_all_ref,
                 wconv_all_ref, cexp1_ref, cexp2_ref, sinkcol_ref, kc_ref, vc_ref,
                 ks_stack_ref, vs_stack_ref,
                 xo_ref, vsgu_ref, z_ref, ko_ref, vo_ref,
                 proj_ref, qprep_ref, ybuf_ref, ycat_ref, knt_ref, vnt_ref, vtmp_ref,
                 *, layer, slopes, t_new, w_a, w_b, w_kv, w_c):
    del ks_stack_ref, vs_stack_ref
    g2_ref = norm_ref.at[layer, pl.ds(2, 1)]
    g3_ref = norm_ref.at[layer, pl.ds(3, 1)]
    gout_ref = gout_all_ref.at[pl.ds(layer, 1)]
    gsgu_ref = gsgu_all_ref.at[pl.ds(layer, 1)]
    wconv_ref = wconv_all_ref.at[layer]
    g = pl.program_id(0)
    rows_all = x_ref.shape[0]
    n_seq = rows_all // t_new
    gsz = kc_ref.shape[0]
    o_q = 2 * w_a
    o_k = o_q + w_b
    o_v = o_k + w_kv
    o_gb = o_v + w_kv
    o_gc = o_gb + w_c
    o_hc = o_gc + w_c
    n_heads = w_b // HEAD_DIM
    group = n_heads // KV_HEADS
    lane = lax.broadcasted_iota(jnp.int32, (1, LANES), 1)
    lo = lane < HEAD_DIM

    @pl.when(g == 0)
    def _():
        x = x_ref[...]
        h = _rms(x, g2_ref[...]).astype(BF16)
        proj_ref[...] = _dot(h, win_ref[...])
        tpos = lax.broadcasted_iota(jnp.int32, (rows_all, 1), 0) % t_new

        u = _gelu(proj_ref[:, 0:w_a])
        v = _head_layer_norm(_gelu(proj_ref[:, w_a:2 * w_a]), gsgu_ref[...])
        for c in range(w_a // LANES):
            vtmp_ref[c] = v[:, c * LANES:(c + 1) * LANES]
        for tt in range(t_new):
            for c in range(w_a // LANES):
                vsgu_ref[tt, c * LANES:(c + 1) * LANES, :] = (
                    vtmp_ref[c, pl.ds(tt, n_seq, stride=t_new), :].T)
        mix = bias_ref[...] + coef_ref[0] * v
        for dlt in range(1, t_new):
            mix = mix + coef_ref[dlt] * pltpu.roll(v, dlt, 0)
        ycat_ref[:, 0:w_a] = _rms(u * mix, gout_ref[:, 0:w_a]).astype(BF16)

        z = proj_ref[:, o_gc:o_gc + w_c] * proj_ref[:, o_hc:o_hc + w_c]
        z_ref[...] = z
        s2 = jnp.where(tpos >= 2, pltpu.roll(z, 2, 0), 0.0) + cexp2_ref[...]
        s1 = jnp.where(tpos >= 1, pltpu.roll(z, 1, 0), 0.0) + cexp1_ref[...]
        conv = s2 * wconv_ref[0:1, :] + s1 * wconv_ref[1:2, :] + z * wconv_ref[2:3, :]
        yc = proj_ref[:, o_gb:o_gb + w_c] * conv
        ycat_ref[:, w_a + w_b:] = _rms(yc, gout_ref[:, w_a + w_b:]).astype(BF16)

        for hd in range(n_heads):
            c, hf, kvh = hd // 2, hd % 2, hd // group
            piece = proj_ref[:, o_q + c * LANES:o_q + (c + 1) * LANES]
            if hf != kvh:
                piece = pltpu.roll(piece, HEAD_DIM, 1)
            qprep_ref[hd] = jnp.where(lo if kvh == 0 else ~lo, piece * (SCALE * LOG2E), 0.0)

        for c in range(rows_all // LANES):
            knt_ref[c] = proj_ref[c * LANES:(c + 1) * LANES, o_k:o_k + w_kv].T
            vnt_ref[c] = proj_ref[c * LANES:(c + 1) * LANES, o_v:o_v + w_kv].T

    pair_rows = 2 * t_new
    n_pairs = gsz // 2
    lrows = n_heads * pair_rows
    ri = lax.broadcasted_iota(jnp.int32, (lrows, 1), 0)
    r_in = ri % pair_rows
    seq_q = r_in // t_new
    t_q = r_in % t_new
    cj = lax.broadcasted_iota(jnp.int32, (1, 2 * WINDOW), 1)
    valid_c = (cj // WINDOW == seq_q) & (cj % WINDOW >= t_q + 1)
    dist_c = (WINDOW + t_q - cj % WINDOW).astype(F32)
    nj = lax.broadcasted_iota(jnp.int32, (1, LANES), 1)
    valid_n = (nj < pair_rows) & (nj // t_new == seq_q) & (nj % t_new <= t_q)
    dist_n = (t_q - nj % t_new).astype(F32)
    slope_col = jnp.zeros((lrows, 1), F32)
    for hd in range(n_heads):
        slope_col = jnp.where(ri // pair_rows == hd, slopes[hd], slope_col)
    sink = sinkcol_ref[...] * LOG2E
    bias_c = jnp.where(valid_c, (slope_col * LOG2E) * dist_c, -NEG)
    bias_n = jnp.where(valid_n, (slope_col * LOG2E) * dist_n, -NEG)
    zpad = jnp.zeros((LANES - pair_rows, w_kv), F32)

    keep = lane < WINDOW - t_new
    kn_t = knt_ref[g]
    vn_t = vnt_ref[g]

    def slide(src_ref, dst_ref, new_t, i):
        shift = (WINDOW - t_new - t_new * i) % LANES
        new = pltpu.roll(new_t, shift, 1) if shift else new_t
        dst_ref[i] = jnp.where(keep, pltpu.roll(src_ref[i], WINDOW - t_new, 1), new)

    r0 = pl.multiple_of(g * (n_pairs * pair_rows), n_pairs * pair_rows)
    sc_parts, sn_parts = [], []
    for p in range(n_pairs):
        rows = pl.ds(r0 + p * pair_rows, pair_rows)
        lhs = jnp.concatenate([qprep_ref[hd, rows, :] for hd in range(n_heads)], axis=0).astype(BF16)
        kn = proj_ref[rows, o_k:o_k + w_kv]
        kc = jnp.concatenate([kc_ref[2 * p], kc_ref[2 * p + 1]], axis=1)
        sc_parts.append(_dot(lhs, kc.astype(BF16)))
        sn_parts.append(_dot_nt(lhs, jnp.concatenate([kn, zpad], axis=0).astype(BF16)))
        slide(kc_ref, ko_ref, kn_t, 2 * p)
        slide(kc_ref, ko_ref, kn_t, 2 * p + 1)
    s_c = jnp.stack(sc_parts) - bias_c[None]
    s_n = jnp.stack(sn_parts) - bias_n[None]
    m = jnp.maximum(jnp.maximum(jnp.max(s_c, axis=-1, keepdims=True),
                                jnp.max(s_n, axis=-1, keepdims=True)), sink[None])
    p_c = jnp.exp2(s_c - m)
    p_n = jnp.exp2(s_n - m)
    den = (jnp.sum(p_c, axis=-1, keepdims=True) + jnp.sum(p_n, axis=-1, keepdims=True)
           + jnp.exp2(sink[None] - m))
    inv = 1.0 / den
    p_c = p_c.astype(BF16)
    p_n = p_n.astype(BF16)
    for p in range(n_pairs):
        rows = pl.ds(r0 + p * pair_rows, pair_rows)
        vn = proj_ref[rows, o_v:o_v + w_kv]
        vc = jnp.concatenate([vc_ref[2 * p], vc_ref[2 * p + 1]], axis=1)
        slide(vc_ref, vo_ref, vn_t, 2 * p)
        slide(vc_ref, vo_ref, vn_t, 2 * p + 1)
        o = (_dot_nt(p_c[p], vc.astype(BF16))
             + _dot(p_n[p], jnp.concatenate([vn, zpad], axis=0).astype(BF16))) * inv[p]
        cols = []
        for c in range(n_heads // 2):
            pieces = []
            for hf in range(2):
                hd = 2 * c + hf
                piece = o[hd * pair_rows:(hd + 1) * pair_rows]
                if hd // group != hf:
                    piece = pltpu.roll(piece, HEAD_DIM, 1)
                pieces.append(piece)
            cols.append(jnp.where(lo, pieces[0], pieces[1]))
        ybuf_ref[rows, :] = jnp.concatenate(cols, axis=1)

    @pl.when(g == pl.num_programs(0) - 1)
    def _():
        ycat_ref[:, w_a:w_a + w_b] = _rms(ybuf_ref[...], gout_ref[:, w_a:w_a + w_b]).astype(BF16)
        y = _dot(ycat_ref[...], wout_ref[...])
        xo_ref[...] = x_ref[...] + _rms(y, g3_ref[...])


def _mixer_sample(x_all, layer, n_seq, t_new, norm_g, win, wout, gout_all, coef_all,
                  bias_all, gsgu_all, wconv_all, cexp1_all, cexp2_all, sink_col_all, kc_all, vc_all,
                  ks_stack, vs_stack, w_a, w_b, w_kv, w_c):
    d = x_all.shape[1]
    m = n_seq * t_new
    assert x_all.shape[0] == m
    in_w = win.shape[2]
    n_heads = w_b // HEAD_DIM
    gsz = LANES // t_new
    assert n_seq % gsz == 0 and gsz % 2 == 0 and 2 * t_new == 8 and WINDOW == LANES
    ng = n_seq // gsz
    const2 = lambda g: (0, 0)
    const3 = lambda g: (0, 0, 0)
    layer3 = lambda g: (layer, 0, 0)
    layer4 = lambda g: (layer, 0, 0, 0)
    step_in = lambda g: (layer, g, 0, 0)
    kernel = functools.partial(_mixs_kernel, layer=layer, slopes=_alibi_slopes(n_heads),
                               t_new=t_new, w_a=w_a, w_b=w_b, w_kv=w_kv, w_c=w_c)
    coef_bytes = coef_all.size // coef_all.shape[0] * 4
    depth = win.shape[0]
    vmem = ((win.size + wout.size) // depth * 2 + 4 * m * d * 4 + m * in_w * 4
            + 8 * gsz * WINDOW * w_kv * 4 + coef_bytes * 2 + 24 * MIB)
    return pl.pallas_call(
        kernel,
        grid=(ng,),
        in_specs=[
            pl.BlockSpec((m, d), const2),
            pl.BlockSpec(norm_g.shape, const3),
            pl.BlockSpec((None,) + win.shape[1:], layer3, pipeline_mode=pl.Buffered(1)),
            pl.BlockSpec((None,) + wout.shape[1:], layer3, pipeline_mode=pl.Buffered(1)),
            pl.BlockSpec(gout_all.shape, const2),
            pl.BlockSpec((None,) + coef_all.shape[1:], layer4),
            pl.BlockSpec((None,) + bias_all.shape[1:], layer3),
            pl.BlockSpec(gsgu_all.shape, const2),
            pl.BlockSpec(wconv_all.shape, const3),
            pl.BlockSpec((None,) + cexp1_all.shape[1:], layer3),
            pl.BlockSpec((None,) + cexp2_all.shape[1:], layer3),
            pl.BlockSpec((None,) + sink_col_all.shape[1:], layer3),
            pl.BlockSpec((None, gsz, w_kv, WINDOW), step_in),
            pl.BlockSpec((None, gsz, w_kv, WINDOW), step_in),
            pl.BlockSpec(memory_space=pl.ANY),
            pl.BlockSpec(memory_space=pl.ANY),
        ],
        out_specs=[
            pl.BlockSpec((m, d), const2),
            pl.BlockSpec((t_new, w_a, n_seq), const3),
            pl.BlockSpec((m, w_c), const2),
            pl.BlockSpec((None, gsz, w_kv, WINDOW), step_in),
            pl.BlockSpec((None, gsz, w_kv, WINDOW), step_in),
        ],
        out_shape=[
            jax.ShapeDtypeStruct((m, d), F32),
            jax.ShapeDtypeStruct((t_new, w_a, n_seq), F32),
            jax.ShapeDtypeStruct((m, w_c), F32),
            jax.ShapeDtypeStruct(ks_stack.shape, F32),
            jax.ShapeDtypeStruct(vs_stack.shape, F32),
        ],
        input_output_aliases={14: 3, 15: 4},
        scratch_shapes=[
            pltpu.VMEM((m, in_w), F32),
            pltpu.VMEM((n_heads, m, LANES), F32),
            pltpu.VMEM((m, w_b), F32),
            pltpu.VMEM((m, gout_all.shape[1]), BF16),
            pltpu.VMEM((ng, w_kv, LANES), F32),
            pltpu.VMEM((ng, w_kv, LANES), F32),
            pltpu.VMEM((w_a // LANES, m, LANES), F32),
        ],
        compiler_params=pltpu.CompilerParams(
            dimension_semantics=("arbitrary",), vmem_limit_bytes=vmem),
        name="mixer_sample",
    )(x_all, norm_g, win, wout, gout_all, coef_all, bias_all, gsgu_all, wconv_all, cexp1_all,
      cexp2_all, sink_col_all, kc_all, vc_all, ks_stack, vs_stack)


def kernel(x_prompt, x_sample, cache_swa_k, cache_swa_v, cache_conv, norm_g, w_ffn_gu, w_ffn_down,
           w_mix_in, w_mix_out, g_mix_out, w_sgu, b_sgu, g_sgu, attn_sinks, w_conv):
    batch, seq, d = x_prompt.shape
    n_seq, t_new, _ = x_sample.shape
    depth = norm_g.shape[0]
    w_a = g_sgu.shape[1]
    w_c = w_conv.shape[2]
    w_kv = KV_HEADS * HEAD_DIM
    w_b = g_mix_out.shape[1] - w_a - w_c
    n_heads_a = w_a // HEAD_DIM
    assert t_new <= CHUNK and seq % CHUNK == 0

    xp = x_prompt.reshape(batch * seq, d)
    xs = x_sample.reshape(n_seq * t_new, d)
    to_dp = lambda c: jnp.transpose(c, (0, 1, 3, 4, 2)).reshape(depth, c.shape[1], w_kv, c.shape[2])
    from_dp = lambda c: jnp.transpose(
        c.reshape(depth, c.shape[1], KV_HEADS, HEAD_DIM, c.shape[3]), (0, 1, 4, 2, 3))
    kc_all = to_dp(cache_swa_k)
    vc_all = to_dp(cache_swa_v)

    win_all = w_mix_in.astype(BF16)
    wout_all = w_mix_out.astype(BF16)
    bsgu_full_all = jnp.repeat(jnp.swapaxes(b_sgu, 1, 2), HEAD_DIM, axis=2)
    w4 = jnp.tril(w_sgu[:, :, :t_new, :t_new])
    tt = np.arange(t_new)
    coef_all = jnp.stack([
        jnp.where((tt >= dlt)[None, None, :], w4[:, :, tt, np.maximum(tt - dlt, 0)], 0.0)
        for dlt in range(t_new)], axis=1)
    coef_all = jnp.repeat(coef_all.transpose(0, 1, 3, 2), HEAD_DIM, axis=3)
    coef_all = jnp.tile(coef_all, (1, 1, n_seq, 1))
    bias_all = jnp.tile(jnp.repeat(jnp.swapaxes(b_sgu[:, :, :t_new], 1, 2), HEAD_DIM, axis=2),
                        (1, n_seq, 1))
    pad_t = lambda c: jnp.pad(c, ((0, 0), (0, 0), (0, t_new - c.shape[2]), (0, 0))).reshape(
        depth, n_seq * t_new, w_c)
    cexp2_all = pad_t(cache_conv)
    cexp1_all = pad_t(cache_conv[:, :, 1:])
    sink_col_all = jnp.repeat(attn_sinks, 2 * t_new, axis=1)[:, :, None]

    outs = {k: [] for k in ("sgu", "kp", "vp", "cp", "cs")}
    ks = jnp.zeros(kc_all.shape, F32)
    vs = jnp.zeros(vc_all.shape, F32)
    for l in range(depth):
        xp, xs = _ffn(xp, xs, norm_g, w_ffn_gu, w_ffn_down, l, 0)
        xs_mixed, vsgu, z_s, ks, vs = _mixer_sample(
            xs, l, n_seq, t_new, norm_g, win_all, wout_all, g_mix_out,
            coef_all, bias_all, g_sgu, w_conv, cexp1_all, cexp2_all, sink_col_all, kc_all, vc_all,
            ks, vs, w_a, w_b, w_kv, w_c)
        xp, kp, vp, cp = _mixer_prompt(
            xp, l, batch, seq, norm_g, win_all, wout_all, g_mix_out, w_sgu, bsgu_full_all,
            g_sgu, attn_sinks, w_conv, w_a, w_b, w_kv, w_c)
        xp, xs = _ffn(xp, xs_mixed, norm_g, w_ffn_gu, w_ffn_down, l, 1)

        outs["sgu"].append(vsgu)
        outs["kp"].append(kp)
        outs["vp"].append(vp)
        outs["cp"].append(cp)
        outs["cs"].append(z_s.reshape(n_seq, t_new, w_c)[:, t_new - (CONV_W - 1):])
    sgu = jnp.transpose(
        jnp.stack(outs["sgu"]).reshape(depth, t_new, n_heads_a, HEAD_DIM, n_seq), (0, 4, 1, 2, 3))
    return (xp.reshape(batch, seq, d), xs.reshape(n_seq, t_new, d), sgu,
            from_dp(jnp.stack(outs["kp"])), from_dp(jnp.stack(outs["vp"])),
            from_dp(ks), from_dp(vs),
            jnp.stack(outs["cp"]), jnp.stack(outs["cs"]))
```

```python
import functools

import numpy as np
import jax
import jax.numpy as jnp
from jax import lax
from jax.experimental import pallas as pl
from jax.experimental.pallas import tpu as pltpu

F32 = jnp.float32
BF16 = jnp.bfloat16

HEAD_DIM = 64
KV_HEADS = 2
WINDOW = 128
CHUNK = 128
CONV_W = 3
EPS = 1e-6
NEG = -1e30
SCALE = HEAD_DIM ** -0.5
LOG2E = float(np.log2(np.e))
LANES = 128

FFN_TM = 512
FFN_TF = 256
MIX_T = 512
MIB = 1024 * 1024


def _rms(x, g):
    return x * lax.rsqrt(jnp.mean(x * x, axis=-1, keepdims=True) + EPS) * g


def _gelu(x):
    return 0.5 * x * (1.0 + lax.erf(x * np.float32(np.sqrt(0.5))))


def _head_layer_norm(x, g):
    lane = lax.broadcasted_iota(jnp.int32, (1, x.shape[1]), 1)
    out = jnp.zeros_like(x)
    for hh in range(x.shape[1] // HEAD_DIM):
        m = (lane >= hh * HEAD_DIM) & (lane < (hh + 1) * HEAD_DIM)
        mu = jnp.sum(jnp.where(m, x, 0.0), axis=-1, keepdims=True) / HEAD_DIM
        d = jnp.where(m, x - mu, 0.0)
        var = jnp.sum(d * d, axis=-1, keepdims=True) / HEAD_DIM
        out = out + d * lax.rsqrt(var + EPS)
    return out * g


def _dot(a, b):
    return jnp.dot(a, b, preferred_element_type=F32)


def _dot_nt(a, b):
    return lax.dot_general(a, b, (((1,), (1,)), ((), ())), preferred_element_type=F32)


def _alibi_slopes(n_heads):
    return [float(2.0 ** (-8.0 * h / n_heads)) for h in range(1, n_heads + 1)]


def _ffn_kernel(*refs, layer, half, n_prompt_tiles, split_in, split_out, d_ff, tf, n_stream):
    refs = list(refs)
    xp_ref = refs.pop(0)
    xs_ref = refs.pop(0) if split_in else None
    norm_ref, wgu_hbm, wdn_hbm = refs[:3]
    refs = refs[3:]
    op_ref = refs.pop(0)
    os_ref = refs.pop(0) if split_out else None
    wgu16_ref, wdn16_ref, stage_gu_ref, stage_dn_ref, sem_ref, act_ref = refs
    i = pl.program_id(0)
    cw = stage_gu_ref.shape[2]
    rw = stage_dn_ref.shape[1]

    def gu_copy(c, slot):
        return pltpu.make_async_copy(wgu_hbm.at[layer, half, :, pl.ds(c * cw, cw)],
                                     stage_gu_ref.at[slot], sem_ref.at[0, slot])

    def dn_copy(c, slot):
        return pltpu.make_async_copy(wdn_hbm.at[layer, half, pl.ds(c * rw, rw), :],
                                     stage_dn_ref.at[slot], sem_ref.at[1, slot])

    @pl.when(i == 0)
    def _():
        gu_copy(0, 0).start()
        dn_copy(0, 0).start()
        for c in range(n_stream):
            slot = c % 2
            if c + 1 < n_stream:
                gu_copy(c + 1, 1 - slot).start()
                dn_copy(c + 1, 1 - slot).start()
            gu_copy(c, slot).wait()
            dn_copy(c, slot).wait()
            wgu16_ref[:, c * cw:(c + 1) * cw] = stage_gu_ref[slot].astype(BF16)
            wdn16_ref[c * rw:(c + 1) * rw, :] = stage_dn_ref[slot].astype(BF16)

    if split_in and split_out:
        x = jnp.where(i == 0, xs_ref[...], xp_ref[...])
    elif split_in:
        x = jnp.where(i < n_prompt_tiles, xp_ref[...], xs_ref[...])
    else:
        x = xp_ref[...]
    g_pre = norm_ref[layer, 4 * half:4 * half + 1, :]
    g_post = norm_ref[layer, 4 * half + 1:4 * half + 2, :]
    n_chunks = d_ff // tf

    def hidden_chunks(h, rows, first, count):
        for c in range(first, first + count):
            gate = _dot(h, wgu16_ref[:, c * tf:(c + 1) * tf])
            up = _dot(h, wgu16_ref[:, d_ff + c * tf:d_ff + (c + 1) * tf])
            act_ref[rows, c * tf:(c + 1) * tf] = (jax.nn.silu(gate) * up).astype(BF16)

    def store(rows, val):
        op_ref[rows, :] = val
        if split_out:
            os_ref[rows, :] = val

    hr = x.shape[0] // 2
    ra, rb = slice(0, hr), slice(hr, 2 * hr)
    xa, xb = x[ra], x[rb]
    ha = _rms(xa, g_pre).astype(BF16)
    hidden_chunks(ha, ra, 0, 2)
    hb = _rms(xb, g_pre).astype(BF16)
    hidden_chunks(ha, ra, 2, n_chunks - 2)
    ya = _dot(act_ref[ra, :], wdn16_ref[...])
    hidden_chunks(hb, rb, 0, 3)
    store(ra, xa + 0.5 * _rms(ya, g_post))
    hidden_chunks(hb, rb, 3, n_chunks - 3)
    yb = _dot(act_ref[rb, :], wdn16_ref[...])
    store(rb, xb + 0.5 * _rms(yb, g_post))


def _ffn(xs_in, norm_g, w_gu, w_down, layer, half, n_prompt_rows, split_out):
    split_in = len(xs_in) == 2
    d = xs_in[0].shape[1]
    d_ff = w_down.shape[2]
    tm = FFN_TM
    n_sample_rows = xs_in[1].shape[0] if split_in else xs_in[0].shape[0] - n_prompt_rows
    assert n_prompt_rows % tm == 0 and n_sample_rows == tm and d_ff % FFN_TF == 0
    npt = n_prompt_rows // tm
    n_stream = d_ff // FFN_TF
    cw = 2 * d_ff // n_stream
    rw = d_ff // n_stream
    assert cw % LANES == 0 and rw % 8 == 0
    assert split_in or not split_out
    if split_out:
        prompt_tile = lambda i: (jnp.maximum(i - 1, 0), 0)
    else:
        prompt_tile = lambda i: (jnp.minimum(i, npt - 1), 0)
    const = lambda i: (0, 0)
    in_specs = []
    if split_in:
        in_specs += [pl.BlockSpec((tm, d), prompt_tile),
                     pl.BlockSpec((tm, d), const, pipeline_mode=pl.Buffered(1))]
    else:
        in_specs += [pl.BlockSpec((tm, d), lambda i: (i, 0))]
    in_specs += [pl.BlockSpec(norm_g.shape, lambda i: (0, 0, 0)),
                 pl.BlockSpec(memory_space=pl.ANY),
                 pl.BlockSpec(memory_space=pl.ANY)]
    if split_out:
        out_specs = [pl.BlockSpec((tm, d), prompt_tile),
                     pl.BlockSpec((tm, d), lambda i: (jnp.minimum(i, 1), 0))]
        out_shape = [jax.ShapeDtypeStruct((n_prompt_rows, d), F32),
                     jax.ShapeDtypeStruct((2 * n_sample_rows, d), F32)]
    else:
        out_specs = pl.BlockSpec((tm, d), lambda i: (i, 0))
        out_shape = jax.ShapeDtypeStruct((n_prompt_rows + n_sample_rows, d), F32)
    weights16 = 3 * d * d_ff * 2
    staging = 2 * (d * cw + rw * d) * 4
    vmem = weights16 + staging + 7 * tm * d * 4 + tm * d_ff * 2 + 10 * MIB
    return pl.pallas_call(
        functools.partial(_ffn_kernel, layer=layer, half=half, n_prompt_tiles=npt,
                          split_in=split_in, split_out=split_out, d_ff=d_ff, tf=FFN_TF,
                          n_stream=n_stream),
        grid=(npt + 1,),
        in_specs=in_specs,
        out_specs=out_specs,
        out_shape=out_shape,
        scratch_shapes=[
            pltpu.VMEM((d, 2 * d_ff), BF16),
            pltpu.VMEM((d_ff, d), BF16),
            pltpu.VMEM((2, d, cw), F32),
            pltpu.VMEM((2, rw, d), F32),
            pltpu.SemaphoreType.DMA((2, 2)),
            pltpu.VMEM((tm, d_ff), BF16),
        ],
        compiler_params=pltpu.CompilerParams(
            dimension_semantics=("arbitrary",), vmem_limit_bytes=vmem),
        name="ffn_half_step",
    )(*xs_in, norm_g, w_gu, w_down)


def _mixp_kernel(*refs, **statics):
    parity = lax.rem(pl.program_id(0), 2)

    @pl.when(parity == 0)
    def _():
        _mixp_body(*refs, parity=0, **statics)

    @pl.when(parity == 1)
    def _():
        _mixp_body(*refs, parity=1, **statics)


def _mixp_body(sinks_all_ref, xnext_ref, x_ref, norm_ref, win_ref, wout_ref, gout_all_ref, wsgu_ref,
               bsgu_ref, gsgu_all_ref, wconv_all_ref,
               xo_ref, ko_ref, vo_ref, co_ref,
               proj_even_ref, proj_odd_ref, kb_ref, krb_ref, vb_ref, vrb_ref, zbuf_ref, ycat_ref,
               abias_ref,
               *, parity, layer, tiles_per_seq, slopes, w_a, w_b, w_kv, w_c):
    proj_next_ref, proj_ref = ((proj_even_ref, proj_odd_ref) if parity == 0
                               else (proj_odd_ref, proj_even_ref))
    g2_ref = norm_ref.at[layer, pl.ds(2, 1)]
    g3_ref = norm_ref.at[layer, pl.ds(3, 1)]
    gout_ref = gout_all_ref.at[pl.ds(layer, 1)]
    gsgu_ref = gsgu_all_ref.at[pl.ds(layer, 1)]
    wconv_ref = wconv_all_ref.at[layer]
    step = pl.program_id(0)
    tile = jnp.maximum(step - 1, 0)
    j = lax.rem(tile, tiles_per_seq)
    t_rows = x_ref.shape[0]
    nblk = t_rows // WINDOW
    o_q = 2 * w_a
    o_k = o_q + w_b
    o_v = o_k + w_kv
    o_gb = o_v + w_kv
    o_gc = o_gb + w_c
    o_hc = o_gc + w_c

    @pl.when(step == 0)
    def _():
        proj_ref[...] = jnp.zeros(proj_ref.shape, F32)
        qi = lax.broadcasted_iota(jnp.int32, (WINDOW, 2 * WINDOW), 0)
        kk = lax.broadcasted_iota(jnp.int32, (WINDOW, 2 * WINDOW), 1)
        dist_i = WINDOW + qi - kk
        band = (dist_i >= 0) & (dist_i < WINDOW)
        dist = dist_i.astype(F32)
        for hd in range(len(slopes)):
            abias_ref[hd] = jnp.where(band, (slopes[hd] * LOG2E) * dist, -NEG)

    @pl.when(j == 0)
    def _():
        kb_ref[0:WINDOW, :] = jnp.zeros((WINDOW, w_kv), BF16)
        krb_ref[0:WINDOW, :] = jnp.zeros((WINDOW, w_kv), BF16)
        vb_ref[0:WINDOW, :] = jnp.zeros((WINDOW, w_kv), BF16)
        vrb_ref[0:WINDOW, :] = jnp.zeros((WINDOW, w_kv), BF16)
        zbuf_ref[0:8, :] = jnp.zeros((8, w_c), F32)

    hn = _rms(xnext_ref[...], g2_ref[...]).astype(BF16)
    n_pchunks = 8
    pcw = proj_next_ref.shape[1] // n_pchunks

    def project_chunks(first, count):
        for c in range(first, first + count):
            proj_next_ref[:, c * pcw:(c + 1) * pcw] = _dot(hn, win_ref[:, c * pcw:(c + 1) * pcw])

    x = x_ref[...]
    n_early = n_pchunks - nblk
    project_chunks(0, n_early)

    u = _gelu(proj_ref[:, 0:w_a])
    v = _head_layer_norm(_gelu(proj_ref[:, w_a:2 * w_a]), gsgu_ref[...])
    lane_a = lax.broadcasted_iota(jnp.int32, (1, w_a), 1)
    ri = lax.broadcasted_iota(jnp.int32, (CHUNK, CHUNK), 0)
    ci = lax.broadcasted_iota(jnp.int32, (CHUNK, CHUNK), 1)
    n_heads_a = w_a // HEAD_DIM
    w_tril = [jnp.where(ri >= ci, wsgu_ref[hh], 0.0).astype(BF16) for hh in range(n_heads_a)]
    bias = bsgu_ref[...]
    ya_parts = []
    for n in range(nblk):
        vblk = v[n * CHUNK:(n + 1) * CHUNK]
        mix = bias
        for hh in range(n_heads_a):
            mh = (lane_a >= hh * HEAD_DIM) & (lane_a < (hh + 1) * HEAD_DIM)
            mix = mix + _dot(w_tril[hh], jnp.where(mh, vblk, 0.0).astype(BF16))
        ya_parts.append(u[n * CHUNK:(n + 1) * CHUNK] * mix)
    ya = jnp.concatenate(ya_parts, axis=0)
    ycat_ref[:, 0:w_a] = _rms(ya, gout_ref[:, 0:w_a]).astype(BF16)
    lane = lax.broadcasted_iota(jnp.int32, (1, LANES), 1)
    lo = lane < HEAD_DIM
    k = proj_ref[:, o_k:o_k + w_kv]
    vv = proj_ref[:, o_v:o_v + w_kv]
    kb_ref[WINDOW:, :] = k.astype(BF16)
    krb_ref[WINDOW:, :] = pltpu.roll(k, HEAD_DIM, 1).astype(BF16)
    vb_ref[WINDOW:, :] = vv.astype(BF16)
    vrb_ref[WINDOW:, :] = pltpu.roll(vv, HEAD_DIM, 1).astype(BF16)

    kj = lax.broadcasted_iota(jnp.int32, (WINDOW, 2 * WINDOW), 1)
    has_prev = jnp.full((WINDOW, 2 * WINDOW), j, jnp.int32) > 0
    first_ok = (kj >= WINDOW) | has_prev

    n_q_cols = w_b // LANES
    half = n_q_cols // 2
    yb_parts = []
    for n in range(nblk):
        rows = slice(n * WINDOW, (n + 1) * WINDOW)
        qcols = [proj_ref[rows, o_q + c * LANES:o_q + (c + 1) * LANES] * (SCALE * LOG2E)
                 for c in range(n_q_cols)]
        keep_a = [lo if c < half else ~lo for c in range(n_q_cols)]
        lhs_a = jnp.concatenate([jnp.where(keep_a[c], qcols[c], 0.0) for c in range(n_q_cols)],
                                axis=0).astype(BF16)
        lhs_b = jnp.concatenate([jnp.where(keep_a[c], 0.0, qcols[c]) for c in range(n_q_cols)],
                                axis=0).astype(BF16)
        band_rows = slice(n * WINDOW, (n + 2) * WINDOW)
        s_a = _dot_nt(lhs_a, kb_ref[band_rows, :])
        s_b = _dot_nt(lhs_b, krb_ref[band_rows, :])
        project_chunks(n_early + n, 1)
        p_a, p_b, inv_a, inv_b = [], [], [], []
        for c in range(n_q_cols):
            head_a = 2 * c if c < half else 2 * c + 1
            head_b = 2 * c + 1 if c < half else 2 * c
            for s_all, head, acc, inv in ((s_a, head_a, p_a, inv_a), (s_b, head_b, p_b, inv_b)):
                s = s_all[c * WINDOW:(c + 1) * WINDOW] - abias_ref[head]
                if n == 0:
                    s = jnp.where(first_ok, s, NEG)
                sink = sinks_all_ref[layer, head] * LOG2E
                m = jnp.maximum(jnp.max(s, axis=-1, keepdims=True), sink)
                p = jnp.exp2(s - m)
                den = jnp.sum(p, axis=-1, keepdims=True) + jnp.exp2(sink - m)
                acc.append(p.astype(BF16))
                inv.append(1.0 / den)
        o_a = _dot(jnp.concatenate(p_a, axis=0), vb_ref[band_rows, :])
        o_b = _dot(jnp.concatenate(p_b, axis=0), vrb_ref[band_rows, :])
        cols = []
        for c in range(n_q_cols):
            ra = o_a[c * WINDOW:(c + 1) * WINDOW] * inv_a[c]
            rb = o_b[c * WINDOW:(c + 1) * WINDOW] * inv_b[c]
            cols.append(jnp.where(lo, ra, rb) if c < half else jnp.where(lo, rb, ra))
        yb_parts.append(jnp.concatenate(cols, axis=1))
    yb = jnp.concatenate(yb_parts, axis=0)
    ycat_ref[:, w_a:w_a + w_b] = _rms(yb, gout_ref[:, w_a:w_a + w_b]).astype(BF16)
    kb_ref[0:WINDOW, :] = kb_ref[t_rows:t_rows + WINDOW, :]
    krb_ref[0:WINDOW, :] = krb_ref[t_rows:t_rows + WINDOW, :]
    vb_ref[0:WINDOW, :] = vb_ref[t_rows:t_rows + WINDOW, :]
    vrb_ref[0:WINDOW, :] = vrb_ref[t_rows:t_rows + WINDOW, :]

    z = proj_ref[:, o_gc:o_gc + w_c] * proj_ref[:, o_hc:o_hc + w_c]
    zbuf_ref[8:8 + t_rows, :] = z
    conv = zbuf_ref[8 - (CONV_W - 1):8 - (CONV_W - 1) + t_rows, :] * wconv_ref[0:1, :]
    for jj in range(1, CONV_W):
        off = 8 - (CONV_W - 1) + jj
        conv = conv + zbuf_ref[off:off + t_rows, :] * wconv_ref[jj:jj + 1, :]
    yc = proj_ref[:, o_gb:o_gb + w_c] * conv
    ycat_ref[:, w_a + w_b:] = _rms(yc, gout_ref[:, w_a + w_b:]).astype(BF16)
    z_tail = zbuf_ref[8 + t_rows - (CONV_W - 1):8 + t_rows, :]
    zbuf_ref[8 - (CONV_W - 1):8, :] = z_tail

    y = _dot(ycat_ref[...], wout_ref[...])
    xo_ref[...] = x + _rms(y, g3_ref[...])

    @pl.when((j == tiles_per_seq - 1) & (step > 0))
    def _():
        ko_ref[0] = proj_ref[t_rows - WINDOW:, o_k:o_k + w_kv].T
        vo_ref[0] = proj_ref[t_rows - WINDOW:, o_v:o_v + w_kv].T
        co_ref[0] = z_tail


def _mixer_prompt(x_all, layer, batch, seq, norm_g, win, wout, gout_all, wsgu_all, bsgu_full_all,
                  gsgu_all, sinks_all, wconv_all, w_a, w_b, w_kv, w_c):
    d = x_all.shape[1]
    m = batch * seq
    t = MIX_T
    assert seq % t == 0 and t % WINDOW == 0
    nt = seq // t
    in_w = win.shape[2]
    n_heads_b = w_b // HEAD_DIM
    n_tiles = batch * nt
    const2 = lambda s, *_: (0, 0)
    const3 = lambda s, *_: (0, 0, 0)
    layer3 = lambda s, *_: (layer, 0, 0)
    layer4 = lambda s, *_: (layer, 0, 0, 0)
    row_next = lambda s, *_: (jnp.minimum(s, n_tiles - 1), 0)
    row = lambda s, *_: (jnp.maximum(s - 1, 0), 0)
    per_b = lambda s, *_: (jnp.maximum(s - 1, 0) // nt, 0, 0)
    kernel = functools.partial(_mixp_kernel, layer=layer, tiles_per_seq=nt,
                               slopes=_alibi_slopes(n_heads_b),
                               w_a=w_a, w_b=w_b, w_kv=w_kv, w_c=w_c)
    grid_spec = pltpu.PrefetchScalarGridSpec(
        num_scalar_prefetch=1,
        grid=(n_tiles + 1,),
        in_specs=[
            pl.BlockSpec((t, d), row_next),
            pl.BlockSpec((t, d), row),
            pl.BlockSpec(norm_g.shape, const3),
            pl.BlockSpec((None,) + win.shape[1:], layer3, pipeline_mode=pl.Buffered(1)),
            pl.BlockSpec((None,) + wout.shape[1:], layer3, pipeline_mode=pl.Buffered(1)),
            pl.BlockSpec(gout_all.shape, const2),
            pl.BlockSpec((None,) + wsgu_all.shape[1:], layer4),
            pl.BlockSpec((None,) + bsgu_full_all.shape[1:], layer3),
            pl.BlockSpec(gsgu_all.shape, const2),
            pl.BlockSpec(wconv_all.shape, const3),
        ],
        out_specs=[
            pl.BlockSpec((t, d), row),
            pl.BlockSpec((1, w_kv, WINDOW), per_b),
            pl.BlockSpec((1, w_kv, WINDOW), per_b),
            pl.BlockSpec((1, CONV_W - 1, w_c), per_b),
        ],
        scratch_shapes=[
            pltpu.VMEM((t, in_w), F32),
            pltpu.VMEM((t, in_w), F32),
            pltpu.VMEM((t + WINDOW, w_kv), BF16),
            pltpu.VMEM((t + WINDOW, w_kv), BF16),
            pltpu.VMEM((t + WINDOW, w_kv), BF16),
            pltpu.VMEM((t + WINDOW, w_kv), BF16),
            pltpu.VMEM((t + 8, w_c), F32),
            pltpu.VMEM((t, gout_all.shape[1]), BF16),
            pltpu.VMEM((n_heads_b, WINDOW, 2 * WINDOW), F32),
        ],
    )
    depth = win.shape[0]
    vmem = (win.size + wout.size) // depth * 2 + 6 * t * d * 4 + 2 * t * in_w * 4 + 24 * MIB
    return pl.pallas_call(
        kernel,
        grid_spec=grid_spec,
        out_shape=[
            jax.ShapeDtypeStruct((m, d), F32),
            jax.ShapeDtypeStruct((batch, w_kv, WINDOW), F32),
            jax.ShapeDtypeStruct((batch, w_kv, WINDOW), F32),
            jax.ShapeDtypeStruct((batch, CONV_W - 1, w_c), F32),
        ],
        compiler_params=pltpu.CompilerParams(
            dimension_semantics=("arbitrary",), vmem_limit_bytes=vmem),
        name="mixer_prompt",
    )(sinks_all, x_all, x_all, norm_g, win, wout, gout_all, wsgu_all, bsgu_full_all, gsgu_all,
      wconv_all)


def _mixs_kernel(x_ref, norm_ref, win_ref, wout_ref, gout_all_ref, coef_ref, bias_ref, gsgu_all_ref,
                 wconv_all_ref, cexp1_ref, cexp2_ref, sinkcol_ref, kc_ref, vc_ref,
                 ks_stack_ref, vs_stack_ref,
                 xo_ref, vsgu_ref, z_ref, ko_ref, vo_ref,
                 proj_ref, qprep_ref, ybuf_ref, ycat_ref, knt_ref, vnt_ref, vtmp_ref,
                 *, layer, slopes, t_new, w_a, w_b, w_kv, w_c):
    del ks_stack_ref, vs_stack_ref
    g2_ref = norm_ref.at[layer, pl.ds(2, 1)]
    g3_ref = norm_ref.at[layer, pl.ds(3, 1)]
    gout_ref = gout_all_ref.at[pl.ds(layer, 1)]
    gsgu_ref = gsgu_all_ref.at[pl.ds(layer, 1)]
    wconv_ref = wconv_all_ref.at[layer]
    g = pl.program_id(0)
    rows_all = x_ref.shape[0]
    n_seq = rows_all // t_new
    gsz = kc_ref.shape[0]
    o_q = 2 * w_a
    o_k = o_q + w_b
    o_v = o_k + w_kv
    o_gb = o_v + w_kv
    o_gc = o_gb + w_c
    o_hc = o_gc + w_c
    n_heads = w_b // HEAD_DIM
    group = n_heads // KV_HEADS
    lane = lax.broadcasted_iota(jnp.int32, (1, LANES), 1)
    lo = lane < HEAD_DIM

    @pl.when(g == 0)
    def _():
        x = x_ref[...]
        h = _rms(x, g2_ref[...]).astype(BF16)
        proj_ref[...] = _dot(h, win_ref[...])
        tpos = lax.broadcasted_iota(jnp.int32, (rows_all, 1), 0) % t_new

        u = _gelu(proj_ref[:, 0:w_a])
        v = _head_layer_norm(_gelu(proj_ref[:, w_a:2 * w_a]), gsgu_ref[...])
        for c in range(w_a // LANES):
            vtmp_ref[c] = v[:, c * LANES:(c + 1) * LANES]
        for tt in range(t_new):
            for c in range(w_a // LANES):
                vsgu_ref[tt, c * LANES:(c + 1) * LANES, :] = (
                    vtmp_ref[c, pl.ds(tt, n_seq, stride=t_new), :].T)
        mix = bias_ref[...] + coef_ref[0] * v
        for dlt in range(1, t_new):
            mix = mix + coef_ref[dlt] * pltpu.roll(v, dlt, 0)
        ycat_ref[:, 0:w_a] = _rms(u * mix, gout_ref[:, 0:w_a]).astype(BF16)

        z = proj_ref[:, o_gc:o_gc + w_c] * proj_ref[:, o_hc:o_hc + w_c]
        z_ref[...] = z
        s2 = jnp.where(tpos >= 2, pltpu.roll(z, 2, 0), 0.0) + cexp2_ref[...]
        s1 = jnp.where(tpos >= 1, pltpu.roll(z, 1, 0), 0.0) + cexp1_ref[...]
        conv = s2 * wconv_ref[0:1, :] + s1 * wconv_ref[1:2, :] + z * wconv_ref[2:3, :]
        yc = proj_ref[:, o_gb:o_gb + w_c] * conv
        ycat_ref[:, w_a + w_b:] = _rms(yc, gout_ref[:, w_a + w_b:]).astype(BF16)

        for hd in range(n_heads):
            c, hf, kvh = hd // 2, hd % 2, hd // group
            piece = proj_ref[:, o_q + c * LANES:o_q + (c + 1) * LANES]
            if hf != kvh:
                piece = pltpu.roll(piece, HEAD_DIM, 1)
            qprep_ref[hd] = jnp.where(lo if kvh == 0 else ~lo, piece * (SCALE * LOG2E), 0.0)

        for c in range(rows_all // LANES):
            knt_ref[c] = proj_ref[c * LANES:(c + 1) * LANES, o_k:o_k + w_kv].T
            vnt_ref[c] = proj_ref[c * LANES:(c + 1) * LANES, o_v:o_v + w_kv].T

    pair_rows = 2 * t_new
    n_pairs = gsz // 2
    lrows = n_heads * pair_rows
    ri = lax.broadcasted_iota(jnp.int32, (lrows, 1), 0)
    r_in = ri % pair_rows
    seq_q = r_in // t_new
    t_q = r_in % t_new
    cj = lax.broadcasted_iota(jnp.int32, (1, 2 * WINDOW), 1)
    valid_c = (cj // WINDOW == seq_q) & (cj % WINDOW >= t_q + 1)
    dist_c = (WINDOW + t_q - cj % WINDOW).astype(F32)
    nj = lax.broadcasted_iota(jnp.int32, (1, LANES), 1)
    valid_n = (nj < pair_rows) & (nj // t_new == seq_q) & (nj % t_new <= t_q)
    dist_n = (t_q - nj % t_new).astype(F32)
    slope_col = jnp.zeros((lrows, 1), F32)
    for hd in range(n_heads):
        slope_col = jnp.where(ri // pair_rows == hd, slopes[hd], slope_col)
    sink = sinkcol_ref[...] * LOG2E
    bias_c = jnp.where(valid_c, (slope_col * LOG2E) * dist_c, -NEG)
    bias_n = jnp.where(valid_n, (slope_col * LOG2E) * dist_n, -NEG)
    zpad = jnp.zeros((LANES - pair_rows, w_kv), F32)

    keep = lane < WINDOW - t_new
    kn_t = knt_ref[g]
    vn_t = vnt_ref[g]

    def slide(src_ref, dst_ref, new_t, i):
        shift = (WINDOW - t_new - t_new * i) % LANES
        new = pltpu.roll(new_t, shift, 1) if shift else new_t
        dst_ref[i] = jnp.where(keep, pltpu.roll(src_ref[i], WINDOW - t_new, 1), new)

    r0 = pl.multiple_of(g * (n_pairs * pair_rows), n_pairs * pair_rows)
    sc_parts, sn_parts = [], []
    for p in range(n_pairs):
        rows = pl.ds(r0 + p * pair_rows, pair_rows)
        lhs = jnp.concatenate([qprep_ref[hd, rows, :] for hd in range(n_heads)], axis=0).astype(BF16)
        kn = proj_ref[rows, o_k:o_k + w_kv]
        kc = jnp.concatenate([kc_ref[2 * p], kc_ref[2 * p + 1]], axis=1)
        sc_parts.append(_dot(lhs, kc.astype(BF16)))
        sn_parts.append(_dot_nt(lhs, jnp.concatenate([kn, zpad], axis=0).astype(BF16)))
        slide(kc_ref, ko_ref, kn_t, 2 * p)
        slide(kc_ref, ko_ref, kn_t, 2 * p + 1)
    s_c = jnp.stack(sc_parts) - bias_c[None]
    s_n = jnp.stack(sn_parts) - bias_n[None]
    m = jnp.maximum(jnp.maximum(jnp.max(s_c, axis=-1, keepdims=True),
                                jnp.max(s_n, axis=-1, keepdims=True)), sink[None])
    p_c = jnp.exp2(s_c - m)
    p_n = jnp.exp2(s_n - m)
    den = (jnp.sum(p_c, axis=-1, keepdims=True) + jnp.sum(p_n, axis=-1, keepdims=True)
           + jnp.exp2(sink[None] - m))
    inv = 1.0 / den
    p_c = p_c.astype(BF16)
    p_n = p_n.astype(BF16)
    for p in range(n_pairs):
        rows = pl.ds(r0 + p * pair_rows, pair_rows)
        vn = proj_ref[rows, o_v:o_v + w_kv]
        vc = jnp.concatenate([vc_ref[2 * p], vc_ref[2 * p + 1]], axis=1)
        slide(vc_ref, vo_ref, vn_t, 2 * p)
        slide(vc_ref, vo_ref, vn_t, 2 * p + 1)
        o = (_dot_nt(p_c[p], vc.astype(BF16))
             + _dot(p_n[p], jnp.concatenate([vn, zpad], axis=0).astype(BF16))) * inv[p]
        cols = []
        for c in range(n_heads // 2):
            pieces = []
            for hf in range(2):
                hd = 2 * c + hf
                piece = o[hd * pair_rows:(hd + 1) * pair_rows]
                if hd // group != hf:
                    piece = pltpu.roll(piece, HEAD_DIM, 1)
                pieces.append(piece)
            cols.append(jnp.where(lo, pieces[0], pieces[1]))
        ybuf_ref[rows, :] = jnp.concatenate(cols, axis=1)

    @pl.when(g == pl.num_programs(0) - 1)
    def _():
        ycat_ref[:, w_a:w_a + w_b] = _rms(ybuf_ref[...], gout_ref[:, w_a:w_a + w_b]).astype(BF16)
        y = _dot(ycat_ref[...], wout_ref[...])
        xo_ref[...] = x_ref[...] + _rms(y, g3_ref[...])


def _mixer_sample(x_all, layer, n_prompt_rows, n_seq, t_new, norm_g, win, wout, gout_all, coef_all,
                  bias_all, gsgu_all, wconv_all, cexp1_all, cexp2_all, sink_col_all, kc_all, vc_all,
                  ks_stack, vs_stack, w_a, w_b, w_kv, w_c):
    d = x_all.shape[1]
    m = n_seq * t_new
    assert n_prompt_rows % m == 0
    in_w = win.shape[2]
    n_heads = w_b // HEAD_DIM
    gsz = LANES // t_new
    assert n_seq % gsz == 0 and gsz % 2 == 0 and 2 * t_new == 8 and WINDOW == LANES
    ng = n_seq // gsz
    const2 = lambda g: (0, 0)
    const3 = lambda g: (0, 0, 0)
    layer3 = lambda g: (layer, 0, 0)
    layer4 = lambda g: (layer, 0, 0, 0)
    step_in = lambda g: (layer, g, 0, 0)
    kernel = functools.partial(_mixs_kernel, layer=layer, slopes=_alibi_slopes(n_heads),
                               t_new=t_new, w_a=w_a, w_b=w_b, w_kv=w_kv, w_c=w_c)
    coef_bytes = coef_all.size // coef_all.shape[0] * 4
    depth = win.shape[0]
    vmem = ((win.size + wout.size) // depth * 2 + 4 * m * d * 4 + m * in_w * 4
            + 8 * gsz * WINDOW * w_kv * 4 + coef_bytes * 2 + 24 * MIB)
    return pl.pallas_call(
        kernel,
        grid=(ng,),
        in_specs=[
            pl.BlockSpec((m, d), lambda g: (n_prompt_rows // m, 0)),
            pl.BlockSpec(norm_g.shape, const3),
            pl.BlockSpec((None,) + win.shape[1:], layer3, pipeline_mode=pl.Buffered(1)),
            pl.BlockSpec((None,) + wout.shape[1:], layer3, pipeline_mode=pl.Buffered(1)),
            pl.BlockSpec(gout_all.shape, const2),
            pl.BlockSpec((None,) + coef_all.shape[1:], layer4),
            pl.BlockSpec((None,) + bias_all.shape[1:], layer3),
            pl.BlockSpec(gsgu_all.shape, const2),
            pl.BlockSpec(wconv_all.shape, const3),
            pl.BlockSpec((None,) + cexp1_all.shape[1:], layer3),
            pl.BlockSpec((None,) + cexp2_all.shape[1:], layer3),
            pl.BlockSpec((None,) + sink_col_all.shape[1:], layer3),
            pl.BlockSpec((None, gsz, w_kv, WINDOW), step_in),
            pl.BlockSpec((None, gsz, w_kv, WINDOW), step_in),
            pl.BlockSpec(memory_space=pl.ANY),
            pl.BlockSpec(memory_space=pl.ANY),
        ],
        out_specs=[
            pl.BlockSpec((m, d), const2),
            pl.BlockSpec((t_new, w_a, n_seq), const3),
            pl.BlockSpec((m, w_c), const2),
            pl.BlockSpec((None, gsz, w_kv, WINDOW), step_in),
            pl.BlockSpec((None, gsz, w_kv, WINDOW), step_in),
        ],
        out_shape=[
            jax.ShapeDtypeStruct((m, d), F32),
            jax.ShapeDtypeStruct((t_new, w_a, n_seq), F32),
            jax.ShapeDtypeStruct((m, w_c), F32),
            jax.ShapeDtypeStruct(ks_stack.shape, F32),
            jax.ShapeDtypeStruct(vs_stack.shape, F32),
        ],
        input_output_aliases={14: 3, 15: 4},
        scratch_shapes=[
            pltpu.VMEM((m, in_w), F32),
            pltpu.VMEM((n_heads, m, LANES), F32),
            pltpu.VMEM((m, w_b), F32),
            pltpu.VMEM((m, gout_all.shape[1]), BF16),
            pltpu.VMEM((ng, w_kv, LANES), F32),
            pltpu.VMEM((ng, w_kv, LANES), F32),
            pltpu.VMEM((w_a // LANES, m, LANES), F32),
        ],
        compiler_params=pltpu.CompilerParams(
            dimension_semantics=("arbitrary",), vmem_limit_bytes=vmem),
        name="mixer_sample",
    )(x_all, norm_g, win, wout, gout_all, coef_all, bias_all, gsgu_all, wconv_all, cexp1_all,
      cexp2_all, sink_col_all, kc_all, vc_all, ks_stack, vs_stack)


def kernel(x_prompt, x_sample, cache_swa_k, cache_swa_v, cache_conv, norm_g, w_ffn_gu, w_ffn_down,
           w_mix_in, w_mix_out, g_mix_out, w_sgu, b_sgu, g_sgu, attn_sinks, w_conv):
    batch, seq, d = x_prompt.shape
    n_seq, t_new, _ = x_sample.shape
    depth = norm_g.shape[0]
    w_a = g_sgu.shape[1]
    w_c = w_conv.shape[2]
    w_kv = KV_HEADS * HEAD_DIM
    w_b = g_mix_out.shape[1] - w_a - w_c
    n_heads_a = w_a // HEAD_DIM
    assert t_new <= CHUNK and seq % CHUNK == 0

    xp = x_prompt.reshape(batch * seq, d)
    xs = x_sample.reshape(n_seq * t_new, d)
    to_dp = lambda c: jnp.transpose(c, (0, 1, 3, 4, 2)).reshape(depth, c.shape[1], w_kv, c.shape[2])
    from_dp = lambda c: jnp.transpose(
        c.reshape(depth, c.shape[1], KV_HEADS, HEAD_DIM, c.shape[3]), (0, 1, 4, 2, 3))
    kc_all = to_dp(cache_swa_k)
    vc_all = to_dp(cache_swa_v)

    win_all = w_mix_in.astype(BF16)
    wout_all = w_mix_out.astype(BF16)
    bsgu_full_all = jnp.repeat(jnp.swapaxes(b_sgu, 1, 2), HEAD_DIM, axis=2)
    w4 = jnp.tril(w_sgu[:, :, :t_new, :t_new])
    tt = np.arange(t_new)
    coef_all = jnp.stack([
        jnp.where((tt >= dlt)[None, None, :], w4[:, :, tt, np.maximum(tt - dlt, 0)], 0.0)
        for dlt in range(t_new)], axis=1)
    coef_all = jnp.repeat(coef_all.transpose(0, 1, 3, 2), HEAD_DIM, axis=3)
    coef_all = jnp.tile(coef_all, (1, 1, n_seq, 1))
    bias_all = jnp.tile(jnp.repeat(jnp.swapaxes(b_sgu[:, :, :t_new], 1, 2), HEAD_DIM, axis=2),
                        (1, n_seq, 1))
    pad_t = lambda c: jnp.pad(c, ((0, 0), (0, 0), (0, t_new - c.shape[2]), (0, 0))).reshape(
        depth, n_seq * t_new, w_c)
    cexp2_all = pad_t(cache_conv)
    cexp1_all = pad_t(cache_conv[:, :, 1:])
    sink_col_all = jnp.repeat(attn_sinks, 2 * t_new, axis=1)[:, :, None]

    n_prompt_rows = batch * seq
    outs = {k: [] for k in ("sgu", "kp", "vp", "cp", "cs")}
    ks = jnp.zeros(kc_all.shape, F32)
    vs = jnp.zeros(vc_all.shape, F32)
    x_in = (xp, xs)
    for l in range(depth):
        x_all = _ffn(x_in, norm_g, w_ffn_gu, w_ffn_down, l, 0, n_prompt_rows, False)
        xp, kp, vp, cp = _mixer_prompt(
            x_all, l, batch, seq, norm_g, win_all, wout_all, g_mix_out, w_sgu, bsgu_full_all,
            g_sgu, attn_sinks, w_conv, w_a, w_b, w_kv, w_c)
        xs, vsgu, z_s, ks, vs = _mixer_sample(
            x_all, l, n_prompt_rows, n_seq, t_new, norm_g, win_all, wout_all, g_mix_out,
            coef_all, bias_all, g_sgu, w_conv, cexp1_all, cexp2_all, sink_col_all, kc_all, vc_all,
            ks, vs, w_a, w_b, w_kv, w_c)
        last = l == depth - 1
        res = _ffn((xp, xs), norm_g, w_ffn_gu, w_ffn_down, l, 1, n_prompt_rows, last)
        if last:
            xp, xs = res[0], res[1][:n_seq * t_new]
        else:
            x_in = (res,)

        outs["sgu"].append(vsgu)
        outs["kp"].append(kp)
        outs["vp"].append(vp)
        outs["cp"].append(cp)
        outs["cs"].append(z_s.reshape(n_seq, t_new, w_c)[:, t_new - (CONV_W - 1):])
    sgu = jnp.transpose(
        jnp.stack(outs["sgu"]).reshape(depth, t_new, n_heads_a, HEAD_DIM, n_seq), (0, 4, 1, 2, 3))
    return (xp.reshape(batch, seq, d), xs.reshape(n_seq, t_new, d), sgu,
            from_dp(jnp.stack(outs["kp"])), from_dp(jnp.stack(outs["vp"])),
            from_dp(ks), from_dp(vs),
            jnp.stack(outs["cp"]), jnp.stack(outs["cs"]))
```

```python
import functools

import numpy as np
import jax
import jax.numpy as jnp
from jax import lax
from jax.experimental import pallas as pl
from jax.experimental.pallas import tpu as pltpu

F32 = jnp.float32
BF16 = jnp.bfloat16

HEAD_DIM = 64
KV_HEADS = 2
WINDOW = 128
CHUNK = 128
CONV_W = 3
EPS = 1e-6
NEG = -1e30
SCALE = HEAD_DIM ** -0.5
LOG2E = float(np.log2(np.e))
LANES = 128

FFN_TM = 512
FFN_TF = 256
MIX_T = 512
MIB = 1024 * 1024


def _rms(x, g):
    return x * lax.rsqrt(jnp.mean(x * x, axis=-1, keepdims=True) + EPS) * g


def _gelu(x):
    return 0.5 * x * (1.0 + lax.erf(x * np.float32(np.sqrt(0.5))))


def _head_layer_norm(x, g):
    lane = lax.broadcasted_iota(jnp.int32, (1, x.shape[1]), 1)
    out = jnp.zeros_like(x)
    for hh in range(x.shape[1] // HEAD_DIM):
        m = (lane >= hh * HEAD_DIM) & (lane < (hh + 1) * HEAD_DIM)
        mu = jnp.sum(jnp.where(m, x, 0.0), axis=-1, keepdims=True) / HEAD_DIM
        d = jnp.where(m, x - mu, 0.0)
        var = jnp.sum(d * d, axis=-1, keepdims=True) / HEAD_DIM
        out = out + d * lax.rsqrt(var + EPS)
    return out * g


def _dot(a, b):
    return jnp.dot(a, b, preferred_element_type=F32)


def _dot_nt(a, b):
    return lax.dot_general(a, b, (((1,), (1,)), ((), ())), preferred_element_type=F32)


def _alibi_slopes(n_heads):
    return [float(2.0 ** (-8.0 * h / n_heads)) for h in range(1, n_heads + 1)]


def _ffn_kernel(*refs, layer, half, n_prompt_tiles, split_in, split_out, d_ff, tf, n_stream):
    refs = list(refs)
    xp_ref = refs.pop(0)
    xs_ref = refs.pop(0) if split_in else None
    norm_ref, wgu_hbm, wdn_hbm = refs[:3]
    refs = refs[3:]
    op_ref = refs.pop(0)
    os_ref = refs.pop(0) if split_out else None
    wgu16_ref, wdn16_ref, stage_gu_ref, stage_dn_ref, sem_ref, act_ref = refs
    i = pl.program_id(0)
    cw = stage_gu_ref.shape[2]
    rw = stage_dn_ref.shape[1]

    def gu_copy(c, slot):
        return pltpu.make_async_copy(wgu_hbm.at[layer, half, :, pl.ds(c * cw, cw)],
                                     stage_gu_ref.at[slot], sem_ref.at[0, slot])

    def dn_copy(c, slot):
        return pltpu.make_async_copy(wdn_hbm.at[layer, half, pl.ds(c * rw, rw), :],
                                     stage_dn_ref.at[slot], sem_ref.at[1, slot])

    @pl.when(i == 0)
    def _():
        gu_copy(0, 0).start()
        dn_copy(0, 0).start()
        for c in range(n_stream):
            slot = c % 2
            if c + 1 < n_stream:
                gu_copy(c + 1, 1 - slot).start()
                dn_copy(c + 1, 1 - slot).start()
            gu_copy(c, slot).wait()
            dn_copy(c, slot).wait()
            wgu16_ref[:, c * cw:(c + 1) * cw] = stage_gu_ref[slot].astype(BF16)
            wdn16_ref[c * rw:(c + 1) * rw, :] = stage_dn_ref[slot].astype(BF16)

    if split_in and split_out:
        x = jnp.where(i == 0, xs_ref[...], xp_ref[...])
    elif split_in:
        x = jnp.where(i < n_prompt_tiles, xp_ref[...], xs_ref[...])
    else:
        x = xp_ref[...]
    g_pre = norm_ref[layer, 4 * half:4 * half + 1, :]
    g_post = norm_ref[layer, 4 * half + 1:4 * half + 2, :]
    n_chunks = d_ff // tf

    def hidden_chunks(h, rows, first, count):
        for c in range(first, first + count):
            gate = _dot(h, wgu16_ref[:, c * tf:(c + 1) * tf])
            up = _dot(h, wgu16_ref[:, d_ff + c * tf:d_ff + (c + 1) * tf])
            act_ref[rows, c * tf:(c + 1) * tf] = (jax.nn.silu(gate) * up).astype(BF16)

    def store(rows, val):
        op_ref[rows, :] = val
        if split_out:
            os_ref[rows, :] = val

    hr = x.shape[0] // 2
    ra, rb = slice(0, hr), slice(hr, 2 * hr)
    xa, xb = x[ra], x[rb]
    ha = _rms(xa, g_pre).astype(BF16)
    hidden_chunks(ha, ra, 0, 2)
    hb = _rms(xb, g_pre).astype(BF16)
    hidden_chunks(ha, ra, 2, n_chunks - 2)
    ya = _dot(act_ref[ra, :], wdn16_ref[...])
    hidden_chunks(hb, rb, 0, 3)
    store(ra, xa + 0.5 * _rms(ya, g_post))
    hidden_chunks(hb, rb, 3, n_chunks - 3)
    yb = _dot(act_ref[rb, :], wdn16_ref[...])
    store(rb, xb + 0.5 * _rms(yb, g_post))


def _ffn(xs_in, norm_g, w_gu, w_down, layer, half, n_prompt_rows, split_out):
    split_in = len(xs_in) == 2
    d = xs_in[0].shape[1]
    d_ff = w_down.shape[2]
    tm = FFN_TM
    n_sample_rows = xs_in[1].shape[0] if split_in else xs_in[0].shape[0] - n_prompt_rows
    assert n_prompt_rows % tm == 0 and n_sample_rows == tm and d_ff % FFN_TF == 0
    npt = n_prompt_rows // tm
    n_stream = d_ff // FFN_TF
    cw = 2 * d_ff // n_stream
    rw = d_ff // n_stream
    assert cw % LANES == 0 and rw % 8 == 0
    assert split_in or not split_out
    if split_out:
        prompt_tile = lambda i: (jnp.maximum(i - 1, 0), 0)
    else:
        prompt_tile = lambda i: (jnp.minimum(i, npt - 1), 0)
    const = lambda i: (0, 0)
    in_specs = []
    if split_in:
        in_specs += [pl.BlockSpec((tm, d), prompt_tile),
                     pl.BlockSpec((tm, d), const, pipeline_mode=pl.Buffered(1))]
    else:
        in_specs += [pl.BlockSpec((tm, d), lambda i: (i, 0))]
    in_specs += [pl.BlockSpec(norm_g.shape, lambda i: (0, 0, 0)),
                 pl.BlockSpec(memory_space=pl.ANY),
                 pl.BlockSpec(memory_space=pl.ANY)]
    if split_out:
        out_specs = [pl.BlockSpec((tm, d), prompt_tile),
                     pl.BlockSpec((tm, d), lambda i: (jnp.minimum(i, 1), 0))]
        out_shape = [jax.ShapeDtypeStruct((n_prompt_rows, d), F32),
                     jax.ShapeDtypeStruct((2 * n_sample_rows, d), F32)]
    else:
        out_specs = pl.BlockSpec((tm, d), lambda i: (i, 0))
        out_shape = jax.ShapeDtypeStruct((n_prompt_rows + n_sample_rows, d), F32)
    weights16 = 3 * d * d_ff * 2
    staging = 2 * (d * cw + rw * d) * 4
    vmem = weights16 + staging + 7 * tm * d * 4 + tm * d_ff * 2 + 10 * MIB
    return pl.pallas_call(
        functools.partial(_ffn_kernel, layer=layer, half=half, n_prompt_tiles=npt,
                          split_in=split_in, split_out=split_out, d_ff=d_ff, tf=FFN_TF,
                          n_stream=n_stream),
        grid=(npt + 1,),
        in_specs=in_specs,
        out_specs=out_specs,
        out_shape=out_shape,
        scratch_shapes=[
            pltpu.VMEM((d, 2 * d_ff), BF16),
            pltpu.VMEM((d_ff, d), BF16),
            pltpu.VMEM((2, d, cw), F32),
            pltpu.VMEM((2, rw, d), F32),
            pltpu.SemaphoreType.DMA((2, 2)),
            pltpu.VMEM((tm, d_ff), BF16),
        ],
        compiler_params=pltpu.CompilerParams(
            dimension_semantics=("arbitrary",), vmem_limit_bytes=vmem),
        name="ffn_half_step",
    )(*xs_in, norm_g, w_gu, w_down)


def _mixp_kernel(*refs, **statics):
    parity = lax.rem(pl.program_id(0), 2)

    @pl.when(parity == 0)
    def _():
        _mixp_body(*refs, parity=0, **statics)

    @pl.when(parity == 1)
    def _():
        _mixp_body(*refs, parity=1, **statics)


def _mixp_body(sinks_all_ref, xnext_ref, x_ref, norm_ref, win_ref, wout_ref, gout_all_ref, wsgu_ref,
               bsgu_ref, gsgu_all_ref, wconv_all_ref,
               xo_ref, ko_ref, vo_ref, co_ref,
               proj_even_ref, proj_odd_ref, kb_ref, krb_ref, vb_ref, vrb_ref, zbuf_ref, ycat_ref,
               abias_ref,
               *, parity, layer, tiles_per_seq, slopes, w_a, w_b, w_kv, w_c):
    proj_next_ref, proj_ref = ((proj_even_ref, proj_odd_ref) if parity == 0
                               else (proj_odd_ref, proj_even_ref))
    g2_ref = norm_ref.at[layer, pl.ds(2, 1)]
    g3_ref = norm_ref.at[layer, pl.ds(3, 1)]
    gout_ref = gout_all_ref.at[pl.ds(layer, 1)]
    gsgu_ref = gsgu_all_ref.at[pl.ds(layer, 1)]
    wconv_ref = wconv_all_ref.at[layer]
    step = pl.program_id(0)
    tile = jnp.maximum(step - 1, 0)
    j = lax.rem(tile, tiles_per_seq)
    t_rows = x_ref.shape[0]
    nblk = t_rows // WINDOW
    o_q = 2 * w_a
    o_k = o_q + w_b
    o_v = o_k + w_kv
    o_gb = o_v + w_kv
    o_gc = o_gb + w_c
    o_hc = o_gc + w_c

    @pl.when(step == 0)
    def _():
        proj_ref[...] = jnp.zeros(proj_ref.shape, F32)
        qi = lax.broadcasted_iota(jnp.int32, (WINDOW, 2 * WINDOW), 0)
        kk = lax.broadcasted_iota(jnp.int32, (WINDOW, 2 * WINDOW), 1)
        dist_i = WINDOW + qi - kk
        band = (dist_i >= 0) & (dist_i < WINDOW)
        dist = dist_i.astype(F32)
        for hd in range(len(slopes)):
            abias_ref[hd] = jnp.where(band, (slopes[hd] * LOG2E) * dist, -NEG)

    @pl.when(j == 0)
    def _():
        kb_ref[0:WINDOW, :] = jnp.zeros((WINDOW, w_kv), BF16)
        krb_ref[0:WINDOW, :] = jnp.zeros((WINDOW, w_kv), BF16)
        vb_ref[0:WINDOW, :] = jnp.zeros((WINDOW, w_kv), BF16)
        vrb_ref[0:WINDOW, :] = jnp.zeros((WINDOW, w_kv), BF16)
        zbuf_ref[0:8, :] = jnp.zeros((8, w_c), F32)

    hn = _rms(xnext_ref[...], g2_ref[...]).astype(BF16)
    n_pchunks = 8
    pcw = proj_next_ref.shape[1] // n_pchunks

    def project_chunks(first, count):
        for c in range(first, first + count):
            proj_next_ref[:, c * pcw:(c + 1) * pcw] = _dot(hn, win_ref[:, c * pcw:(c + 1) * pcw])

    x = x_ref[...]
    n_early = n_pchunks - nblk
    project_chunks(0, n_early)

    u = _gelu(proj_ref[:, 0:w_a])
    v = _head_layer_norm(_gelu(proj_ref[:, w_a:2 * w_a]), gsgu_ref[...])
    lane_a = lax.broadcasted_iota(jnp.int32, (1, w_a), 1)
    ri = lax.broadcasted_iota(jnp.int32, (CHUNK, CHUNK), 0)
    ci = lax.broadcasted_iota(jnp.int32, (CHUNK, CHUNK), 1)
    n_heads_a = w_a // HEAD_DIM
    w_tril = [jnp.where(ri >= ci, wsgu_ref[hh], 0.0).astype(BF16) for hh in range(n_heads_a)]
    bias = bsgu_ref[...]
    ya_parts = []
    for n in range(nblk):
        vblk = v[n * CHUNK:(n + 1) * CHUNK]
        mix = bias
        for hh in range(n_heads_a):
            mh = (lane_a >= hh * HEAD_DIM) & (lane_a < (hh + 1) * HEAD_DIM)
            mix = mix + _dot(w_tril[hh], jnp.where(mh, vblk, 0.0).astype(BF16))
        ya_parts.append(u[n * CHUNK:(n + 1) * CHUNK] * mix)
    ya = jnp.concatenate(ya_parts, axis=0)
    ycat_ref[:, 0:w_a] = _rms(ya, gout_ref[:, 0:w_a]).astype(BF16)
    lane = lax.broadcasted_iota(jnp.int32, (1, LANES), 1)
    lo = lane < HEAD_DIM
    k = proj_ref[:, o_k:o_k + w_kv]
    vv = proj_ref[:, o_v:o_v + w_kv]
    kb_ref[WINDOW:, :] = k.astype(BF16)
    krb_ref[WINDOW:, :] = pltpu.roll(k, HEAD_DIM, 1).astype(BF16)
    vb_ref[WINDOW:, :] = vv.astype(BF16)
    vrb_ref[WINDOW:, :] = pltpu.roll(vv, HEAD_DIM, 1).astype(BF16)

    kj = lax.broadcasted_iota(jnp.int32, (WINDOW, 2 * WINDOW), 1)
    has_prev = jnp.full((WINDOW, 2 * WINDOW), j, jnp.int32) > 0
    first_ok = (kj >= WINDOW) | has_prev

    n_q_cols = w_b // LANES
    half = n_q_cols // 2
    yb_parts = []

    def scores(n):
        rows = slice(n * WINDOW, (n + 1) * WINDOW)
        qcols = [proj_ref[rows, o_q + c * LANES:o_q + (c + 1) * LANES] * (SCALE * LOG2E)
                 for c in range(n_q_cols)]
        keep_a = [lo if c < half else ~lo for c in range(n_q_cols)]
        lhs_a = jnp.concatenate([jnp.where(keep_a[c], qcols[c], 0.0) for c in range(n_q_cols)],
                                axis=0).astype(BF16)
        lhs_b = jnp.concatenate([jnp.where(keep_a[c], 0.0, qcols[c]) for c in range(n_q_cols)],
                                axis=0).astype(BF16)
        band_rows = slice(n * WINDOW, (n + 2) * WINDOW)
        return _dot_nt(lhs_a, kb_ref[band_rows, :]), _dot_nt(lhs_b, krb_ref[band_rows, :])

    s_next = scores(0)
    for n in range(nblk):
        band_rows = slice(n * WINDOW, (n + 2) * WINDOW)
        s_a, s_b = s_next
        project_chunks(n_early + n, 1)
        if n + 1 < nblk:
            s_next = scores(n + 1)
        p_a, p_b, inv_a, inv_b = [], [], [], []
        for c in range(n_q_cols):
            head_a = 2 * c if c < half else 2 * c + 1
            head_b = 2 * c + 1 if c < half else 2 * c
            for s_all, head, acc, inv in ((s_a, head_a, p_a, inv_a), (s_b, head_b, p_b, inv_b)):
                s = s_all[c * WINDOW:(c + 1) * WINDOW] - abias_ref[head]
                if n == 0:
                    s = jnp.where(first_ok, s, NEG)
                sink = sinks_all_ref[layer, head] * LOG2E
                m = jnp.maximum(jnp.max(s, axis=-1, keepdims=True), sink)
                p = jnp.exp2(s - m)
                den = jnp.sum(p, axis=-1, keepdims=True) + jnp.exp2(sink - m)
                acc.append(p.astype(BF16))
                inv.append(1.0 / den)
        o_a = _dot(jnp.concatenate(p_a, axis=0), vb_ref[band_rows, :])
        o_b = _dot(jnp.concatenate(p_b, axis=0), vrb_ref[band_rows, :])
        cols = []
        for c in range(n_q_cols):
            ra = o_a[c * WINDOW:(c + 1) * WINDOW] * inv_a[c]
            rb = o_b[c * WINDOW:(c + 1) * WINDOW] * inv_b[c]
            cols.append(jnp.where(lo, ra, rb) if c < half else jnp.where(lo, rb, ra))
        yb_parts.append(jnp.concatenate(cols, axis=1))
    yb = jnp.concatenate(yb_parts, axis=0)
    ycat_ref[:, w_a:w_a + w_b] = _rms(yb, gout_ref[:, w_a:w_a + w_b]).astype(BF16)
    kb_ref[0:WINDOW, :] = kb_ref[t_rows:t_rows + WINDOW, :]
    krb_ref[0:WINDOW, :] = krb_ref[t_rows:t_rows + WINDOW, :]
    vb_ref[0:WINDOW, :] = vb_ref[t_rows:t_rows + WINDOW, :]
    vrb_ref[0:WINDOW, :] = vrb_ref[t_rows:t_rows + WINDOW, :]

    z = proj_ref[:, o_gc:o_gc + w_c] * proj_ref[:, o_hc:o_hc + w_c]
    zbuf_ref[8:8 + t_rows, :] = z
    conv = zbuf_ref[8 - (CONV_W - 1):8 - (CONV_W - 1) + t_rows, :] * wconv_ref[0:1, :]
    for jj in range(1, CONV_W):
        off = 8 - (CONV_W - 1) + jj
        conv = conv + zbuf_ref[off:off + t_rows, :] * wconv_ref[jj:jj + 1, :]
    yc = proj_ref[:, o_gb:o_gb + w_c] * conv
    ycat_ref[:, w_a + w_b:] = _rms(yc, gout_ref[:, w_a + w_b:]).astype(BF16)
    z_tail = zbuf_ref[8 + t_rows - (CONV_W - 1):8 + t_rows, :]
    zbuf_ref[8 - (CONV_W - 1):8, :] = z_tail

    y = _dot(ycat_ref[...], wout_ref[...])
    xo_ref[...] = x + _rms(y, g3_ref[...])

    @pl.when((j == tiles_per_seq - 1) & (step > 0))
    def _():
        ko_ref[0] = proj_ref[t_rows - WINDOW:, o_k:o_k + w_kv].T
        vo_ref[0] = proj_ref[t_rows - WINDOW:, o_v:o_v + w_kv].T
        co_ref[0] = z_tail


def _mixer_prompt(x_all, layer, batch, seq, norm_g, win, wout, gout_all, wsgu_all, bsgu_full_all,
                  gsgu_all, sinks_all, wconv_all, w_a, w_b, w_kv, w_c):
    d = x_all.shape[1]
    m = batch * seq
    t = MIX_T
    assert seq % t == 0 and t % WINDOW == 0
    nt = seq // t
    in_w = win.shape[2]
    n_heads_b = w_b // HEAD_DIM
    n_tiles = batch * nt
    const2 = lambda s, *_: (0, 0)
    const3 = lambda s, *_: (0, 0, 0)
    layer3 = lambda s, *_: (layer, 0, 0)
    layer4 = lambda s, *_: (layer, 0, 0, 0)
    row_next = lambda s, *_: (jnp.minimum(s, n_tiles - 1), 0)
    row = lambda s, *_: (jnp.maximum(s - 1, 0), 0)
    per_b = lambda s, *_: (jnp.maximum(s - 1, 0) // nt, 0, 0)
    kernel = functools.partial(_mixp_kernel, layer=layer, tiles_per_seq=nt,
                               slopes=_alibi_slopes(n_heads_b),
                               w_a=w_a, w_b=w_b, w_kv=w_kv, w_c=w_c)
    grid_spec = pltpu.PrefetchScalarGridSpec(
        num_scalar_prefetch=1,
        grid=(n_tiles + 1,),
        in_specs=[
            pl.BlockSpec((t, d), row_next),
            pl.BlockSpec((t, d), row),
            pl.BlockSpec(norm_g.shape, const3),
            pl.BlockSpec((None,) + win.shape[1:], layer3, pipeline_mode=pl.Buffered(1)),
            pl.BlockSpec((None,) + wout.shape[1:], layer3, pipeline_mode=pl.Buffered(1)),
            pl.BlockSpec(gout_all.shape, const2),
            pl.BlockSpec((None,) + wsgu_all.shape[1:], layer4),
            pl.BlockSpec((None,) + bsgu_full_all.shape[1:], layer3),
            pl.BlockSpec(gsgu_all.shape, const2),
            pl.BlockSpec(wconv_all.shape, const3),
        ],
        out_specs=[
            pl.BlockSpec((t, d), row),
            pl.BlockSpec((1, w_kv, WINDOW), per_b),
            pl.BlockSpec((1, w_kv, WINDOW), per_b),
            pl.BlockSpec((1, CONV_W - 1, w_c), per_b),
        ],
        scratch_shapes=[
            pltpu.VMEM((t, in_w), F32),
            pltpu.VMEM((t, in_w), F32),
            pltpu.VMEM((t + WINDOW, w_kv), BF16),
            pltpu.VMEM((t + WINDOW, w_kv), BF16),
            pltpu.VMEM((t + WINDOW, w_kv), BF16),
            pltpu.VMEM((t + WINDOW, w_kv), BF16),
            pltpu.VMEM((t + 8, w_c), F32),
            pltpu.VMEM((t, gout_all.shape[1]), BF16),
            pltpu.VMEM((n_heads_b, WINDOW, 2 * WINDOW), F32),
        ],
    )
    depth = win.shape[0]
    vmem = (win.size + wout.size) // depth * 2 + 6 * t * d * 4 + 2 * t * in_w * 4 + 24 * MIB
    return pl.pallas_call(
        kernel,
        grid_spec=grid_spec,
        out_shape=[
            jax.ShapeDtypeStruct((m, d), F32),
            jax.ShapeDtypeStruct((batch, w_kv, WINDOW), F32),
            jax.ShapeDtypeStruct((batch, w_kv, WINDOW), F32),
            jax.ShapeDtypeStruct((batch, CONV_W - 1, w_c), F32),
        ],
        compiler_params=pltpu.CompilerParams(
            dimension_semantics=("arbitrary",), vmem_limit_bytes=vmem),
        name="mixer_prompt",
    )(sinks_all, x_all, x_all, norm_g, win, wout, gout_all, wsgu_all, bsgu_full_all, gsgu_all,
      wconv_all)


def _mixs_kernel(x_ref, norm_ref, win_ref, wout_ref, gout_all_ref, coef_ref, bias_ref, gsgu_all_ref,
                 wconv_all_ref, cexp1_ref, cexp2_ref, sinkcol_ref, kc_ref, vc_ref,
                 ks_stack_ref, vs_stack_ref,
                 xo_ref, vsgu_ref, z_ref, ko_ref, vo_ref,
                 proj_ref, qprep_ref, ybuf_ref, ycat_ref, knt_ref, vnt_ref, vtmp_ref,
                 *, layer, slopes, t_new, w_a, w_b, w_kv, w_c):
    del ks_stack_ref, vs_stack_ref
    g2_ref = norm_ref.at[layer, pl.ds(2, 1)]
    g3_ref = norm_ref.at[layer, pl.ds(3, 1)]
    gout_ref = gout_all_ref.at[pl.ds(layer, 1)]
    gsgu_ref = gsgu_all_ref.at[pl.ds(layer, 1)]
    wconv_ref = wconv_all_ref.at[layer]
    g = pl.program_id(0)
    rows_all = x_ref.shape[0]
    n_seq = rows_all // t_new
    gsz = kc_ref.shape[0]
    o_q = 2 * w_a
    o_k = o_q + w_b
    o_v = o_k + w_kv
    o_gb = o_v + w_kv
    o_gc = o_gb + w_c
    o_hc = o_gc + w_c
    n_heads = w_b // HEAD_DIM
    group = n_heads // KV_HEADS
    lane = lax.broadcasted_iota(jnp.int32, (1, LANES), 1)
    lo = lane < HEAD_DIM

    @pl.when(g == 0)
    def _():
        x = x_ref[...]
        h = _rms(x, g2_ref[...]).astype(BF16)
        proj_ref[...] = _dot(h, win_ref[...])
        tpos = lax.broadcasted_iota(jnp.int32, (rows_all, 1), 0) % t_new

        u = _gelu(proj_ref[:, 0:w_a])
        v = _head_layer_norm(_gelu(proj_ref[:, w_a:2 * w_a]), gsgu_ref[...])
        for c in range(w_a // LANES):
            vtmp_ref[c] = v[:, c * LANES:(c + 1) * LANES]
        for tt in range(t_new):
            for c in range(w_a // LANES):
                vsgu_ref[tt, c * LANES:(c + 1) * LANES, :] = (
                    vtmp_ref[c, pl.ds(tt, n_seq, stride=t_new), :].T)
        mix = bias_ref[...] + coef_ref[0] * v
        for dlt in range(1, t_new):
            mix = mix + coef_ref[dlt] * pltpu.roll(v, dlt, 0)
        ycat_ref[:, 0:w_a] = _rms(u * mix, gout_ref[:, 0:w_a]).astype(BF16)

        z = proj_ref[:, o_gc:o_gc + w_c] * proj_ref[:, o_hc:o_hc + w_c]
        z_ref[...] = z
        s2 = jnp.where(tpos >= 2, pltpu.roll(z, 2, 0), 0.0) + cexp2_ref[...]
        s1 = jnp.where(tpos >= 1, pltpu.roll(z, 1, 0), 0.0) + cexp1_ref[...]
        conv = s2 * wconv_ref[0:1, :] + s1 * wconv_ref[1:2, :] + z * wconv_ref[2:3, :]
        yc = proj_ref[:, o_gb:o_gb + w_c] * conv
        ycat_ref[:, w_a + w_b:] = _rms(yc, gout_ref[:, w_a + w_b:]).astype(BF16)

        for hd in range(n_heads):
            c, hf, kvh = hd // 2, hd % 2, hd // group
            piece = proj_ref[:, o_q + c * LANES:o_q + (c + 1) * LANES]
            if hf != kvh:
                piece = pltpu.roll(piece, HEAD_DIM, 1)
            qprep_ref[hd] = jnp.where(lo if kvh == 0 else ~lo, piece * (SCALE * LOG2E), 0.0)

        for c in range(rows_all // LANES):
            knt_ref[c] = proj_ref[c * LANES:(c + 1) * LANES, o_k:o_k + w_kv].T
            vnt_ref[c] = proj_ref[c * LANES:(c + 1) * LANES, o_v:o_v + w_kv].T

    pair_rows = 2 * t_new
    n_pairs = gsz // 2
    lrows = n_heads * pair_rows
    ri = lax.broadcasted_iota(jnp.int32, (lrows, 1), 0)
    r_in = ri % pair_rows
    seq_q = r_in // t_new
    t_q = r_in % t_new
    cj = lax.broadcasted_iota(jnp.int32, (1, 2 * WINDOW), 1)
    valid_c = (cj // WINDOW == seq_q) & (cj % WINDOW >= t_q + 1)
    dist_c = (WINDOW + t_q - cj % WINDOW).astype(F32)
    nj = lax.broadcasted_iota(jnp.int32, (1, LANES), 1)
    valid_n = (nj < pair_rows) & (nj // t_new == seq_q) & (nj % t_new <= t_q)
    dist_n = (t_q - nj % t_new).astype(F32)
    slope_col = jnp.zeros((lrows, 1), F32)
    for hd in range(n_heads):
        slope_col = jnp.where(ri // pair_rows == hd, slopes[hd], slope_col)
    sink = sinkcol_ref[...] * LOG2E
    bias_c = jnp.where(valid_c, (slope_col * LOG2E) * dist_c, -NEG)
    bias_n = jnp.where(valid_n, (slope_col * LOG2E) * dist_n, -NEG)
    zpad = jnp.zeros((LANES - pair_rows, w_kv), F32)

    keep = lane < WINDOW - t_new
    kn_t = knt_ref[g]
    vn_t = vnt_ref[g]

    def slide(src_ref, dst_ref, new_t, i):
        shift = (WINDOW - t_new - t_new * i) % LANES
        new = pltpu.roll(new_t, shift, 1) if shift else new_t
        dst_ref[i] = jnp.where(keep, pltpu.roll(src_ref[i], WINDOW - t_new, 1), new)

    r0 = pl.multiple_of(g * (n_pairs * pair_rows), n_pairs * pair_rows)
    sc_parts, sn_parts = [], []
    for p in range(n_pairs):
        rows = pl.ds(r0 + p * pair_rows, pair_rows)
        lhs = jnp.concatenate([qprep_ref[hd, rows, :] for hd in range(n_heads)], axis=0).astype(BF16)
        kn = proj_ref[rows, o_k:o_k + w_kv]
        kc = jnp.concatenate([kc_ref[2 * p], kc_ref[2 * p + 1]], axis=1)
        sc_parts.append(_dot(lhs, kc.astype(BF16)))
        sn_parts.append(_dot_nt(lhs, jnp.concatenate([kn, zpad], axis=0).astype(BF16)))
        slide(kc_ref, ko_ref, kn_t, 2 * p)
        slide(kc_ref, ko_ref, kn_t, 2 * p + 1)
    s_c = jnp.stack(sc_parts) - bias_c[None]
    s_n = jnp.stack(sn_parts) - bias_n[None]
    m = jnp.maximum(jnp.maximum(jnp.max(s_c, axis=-1, keepdims=True),
                                jnp.max(s_n, axis=-1, keepdims=True)), sink[None])
    p_c = jnp.exp2(s_c - m)
    p_n = jnp.exp2(s_n - m)
    den = (jnp.sum(p_c, axis=-1, keepdims=True) + jnp.sum(p_n, axis=-1, keepdims=True)
           + jnp.exp2(sink[None] - m))
    inv = 1.0 / den
    p_c = p_c.astype(BF16)
    p_n = p_n.astype(BF16)
    for p in range(n_pairs):
        rows = pl.ds(r0 + p * pair_rows, pair_rows)
        vn = proj_ref[rows, o_v:o_v + w_kv]
        vc = jnp.concatenate([vc_ref[2 * p], vc_ref[2 * p + 1]], axis=1)
        slide(vc_ref, vo_ref, vn_t, 2 * p)
        slide(vc_ref, vo_ref, vn_t, 2 * p + 1)
        o = (_dot_nt(p_c[p], vc.astype(BF16))
             + _dot(p_n[p], jnp.concatenate([vn, zpad], axis=0).astype(BF16))) * inv[p]
        cols = []
        for c in range(n_heads // 2):
            pieces = []
            for hf in range(2):
                hd = 2 * c + hf
                piece = o[hd * pair_rows:(hd + 1) * pair_rows]
                if hd // group != hf:
                    piece = pltpu.roll(piece, HEAD_DIM, 1)
                pieces.append(piece)
            cols.append(jnp.where(lo, pieces[0], pieces[1]))
        ybuf_ref[rows, :] = jnp.concatenate(cols, axis=1)

    @pl.when(g == pl.num_programs(0) - 1)
    def _():
        ycat_ref[:, w_a:w_a + w_b] = _rms(ybuf_ref[...], gout_ref[:, w_a:w_a + w_b]).astype(BF16)
        y = _dot(ycat_ref[...], wout_ref[...])
        xo_ref[...] = x_ref[...] + _rms(y, g3_ref[...])


def _mixer_sample(x_all, layer, n_prompt_rows, n_seq, t_new, norm_g, win, wout, gout_all, coef_all,
                  bias_all, gsgu_all, wconv_all, cexp1_all, cexp2_all, sink_col_all, kc_all, vc_all,
                  ks_stack, vs_stack, w_a, w_b, w_kv, w_c):
    d = x_all.shape[1]
    m = n_seq * t_new
    assert n_prompt_rows % m == 0
    in_w = win.shape[2]
    n_heads = w_b // HEAD_DIM
    gsz = LANES // t_new
    assert n_seq % gsz == 0 and gsz % 2 == 0 and 2 * t_new == 8 and WINDOW == LANES
    ng = n_seq // gsz
    const2 = lambda g: (0, 0)
    const3 = lambda g: (0, 0, 0)
    layer3 = lambda g: (layer, 0, 0)
    layer4 = lambda g: (layer, 0, 0, 0)
    step_in = lambda g: (layer, g, 0, 0)
    kernel = functools.partial(_mixs_kernel, layer=layer, slopes=_alibi_slopes(n_heads),
                               t_new=t_new, w_a=w_a, w_b=w_b, w_kv=w_kv, w_c=w_c)
    coef_bytes = coef_all.size // coef_all.shape[0] * 4
    depth = win.shape[0]
    vmem = ((win.size + wout.size) // depth * 2 + 4 * m * d * 4 + m * in_w * 4
            + 8 * gsz * WINDOW * w_kv * 4 + coef_bytes * 2 + 24 * MIB)
    return pl.pallas_call(
        kernel,
        grid=(ng,),
        in_specs=[
            pl.BlockSpec((m, d), lambda g: (n_prompt_rows // m, 0)),
            pl.BlockSpec(norm_g.shape, const3),
            pl.BlockSpec((None,) + win.shape[1:], layer3, pipeline_mode=pl.Buffered(1)),
            pl.BlockSpec((None,) + wout.shape[1:], layer3, pipeline_mode=pl.Buffered(1)),
            pl.BlockSpec(gout_all.shape, const2),
            pl.BlockSpec((None,) + coef_all.shape[1:], layer4),
            pl.BlockSpec((None,) + bias_all.shape[1:], layer3),
            pl.BlockSpec(gsgu_all.shape, const2),
            pl.BlockSpec(wconv_all.shape, const3),
            pl.BlockSpec((None,) + cexp1_all.shape[1:], layer3),
            pl.BlockSpec((None,) + cexp2_all.shape[1:], layer3),
            pl.BlockSpec((None,) + sink_col_all.shape[1:], layer3),
            pl.BlockSpec((None, gsz, w_kv, WINDOW), step_in),
            pl.BlockSpec((None, gsz, w_kv, WINDOW), step_in),
            pl.BlockSpec(memory_space=pl.ANY),
            pl.BlockSpec(memory_space=pl.ANY),
        ],
        out_specs=[
            pl.BlockSpec((m, d), const2),
            pl.BlockSpec((t_new, w_a, n_seq), const3),
            pl.BlockSpec((m, w_c), const2),
            pl.BlockSpec((None, gsz, w_kv, WINDOW), step_in),
            pl.BlockSpec((None, gsz, w_kv, WINDOW), step_in),
        ],
        out_shape=[
            jax.ShapeDtypeStruct((m, d), F32),
            jax.ShapeDtypeStruct((t_new, w_a, n_seq), F32),
            jax.ShapeDtypeStruct((m, w_c), F32),
            jax.ShapeDtypeStruct(ks_stack.shape, F32),
            jax.ShapeDtypeStruct(vs_stack.shape, F32),
        ],
        input_output_aliases={14: 3, 15: 4},
        scratch_shapes=[
            pltpu.VMEM((m, in_w), F32),
            pltpu.VMEM((n_heads, m, LANES), F32),
            pltpu.VMEM((m, w_b), F32),
            pltpu.VMEM((m, gout_all.shape[1]), BF16),
            pltpu.VMEM((ng, w_kv, LANES), F32),
            pltpu.VMEM((ng, w_kv, LANES), F32),
            pltpu.VMEM((w_a // LANES, m, LANES), F32),
        ],
        compiler_params=pltpu.CompilerParams(
            dimension_semantics=("arbitrary",), vmem_limit_bytes=vmem),
        name="mixer_sample",
    )(x_all, norm_g, win, wout, gout_all, coef_all, bias_all, gsgu_all, wconv_all, cexp1_all,
      cexp2_all, sink_col_all, kc_all, vc_all, ks_stack, vs_stack)


def kernel(x_prompt, x_sample, cache_swa_k, cache_swa_v, cache_conv, norm_g, w_ffn_gu, w_ffn_down,
           w_mix_in, w_mix_out, g_mix_out, w_sgu, b_sgu, g_sgu, attn_sinks, w_conv):
    batch, seq, d = x_prompt.shape
    n_seq, t_new, _ = x_sample.shape
    depth = norm_g.shape[0]
    w_a = g_sgu.shape[1]
    w_c = w_conv.shape[2]
    w_kv = KV_HEADS * HEAD_DIM
    w_b = g_mix_out.shape[1] - w_a - w_c
    n_heads_a = w_a // HEAD_DIM
    assert t_new <= CHUNK and seq % CHUNK == 0

    xp = x_prompt.reshape(batch * seq, d)
    xs = x_sample.reshape(n_seq * t_new, d)
    to_dp = lambda c: jnp.transpose(c, (0, 1, 3, 4, 2)).reshape(depth, c.shape[1], w_kv, c.shape[2])
    from_dp = lambda c: jnp.transpose(
        c.reshape(depth, c.shape[1], KV_HEADS, HEAD_DIM, c.shape[3]), (0, 1, 4, 2, 3))
    kc_all = to_dp(cache_swa_k)
    vc_all = to_dp(cache_swa_v)

    win_all = w_mix_in.astype(BF16)
    wout_all = w_mix_out.astype(BF16)
    bsgu_full_all = jnp.repeat(jnp.swapaxes(b_sgu, 1, 2), HEAD_DIM, axis=2)
    w4 = jnp.tril(w_sgu[:, :, :t_new, :t_new])
    tt = np.arange(t_new)
    coef_all = jnp.stack([
        jnp.where((tt >= dlt)[None, None, :], w4[:, :, tt, np.maximum(tt - dlt, 0)], 0.0)
        for dlt in range(t_new)], axis=1)
    coef_all = jnp.repeat(coef_all.transpose(0, 1, 3, 2), HEAD_DIM, axis=3)
    coef_all = jnp.tile(coef_all, (1, 1, n_seq, 1))
    bias_all = jnp.tile(jnp.repeat(jnp.swapaxes(b_sgu[:, :, :t_new], 1, 2), HEAD_DIM, axis=2),
                        (1, n_seq, 1))
    pad_t = lambda c: jnp.pad(c, ((0, 0), (0, 0), (0, t_new - c.shape[2]), (0, 0))).reshape(
        depth, n_seq * t_new, w_c)
    cexp2_all = pad_t(cache_conv)
    cexp1_all = pad_t(cache_conv[:, :, 1:])
    sink_col_all = jnp.repeat(attn_sinks, 2 * t_new, axis=1)[:, :, None]

    n_prompt_rows = batch * seq
    outs = {k: [] for k in ("sgu", "kp", "vp", "cp", "cs")}
    ks = jnp.zeros(kc_all.shape, F32)
    vs = jnp.zeros(vc_all.shape, F32)
    x_in = (xp, xs)
    for l in range(depth):
        x_all = _ffn(x_in, norm_g, w_ffn_gu, w_ffn_down, l, 0, n_prompt_rows, False)
        xp, kp, vp, cp = _mixer_prompt(
            x_all, l, batch, seq, norm_g, win_all, wout_all, g_mix_out, w_sgu, bsgu_full_all,
            g_sgu, attn_sinks, w_conv, w_a, w_b, w_kv, w_c)
        xs, vsgu, z_s, ks, vs = _mixer_sample(
            x_all, l, n_prompt_rows, n_seq, t_new, norm_g, win_all, wout_all, g_mix_out,
            coef_all, bias_all, g_sgu, w_conv, cexp1_all, cexp2_all, sink_col_all, kc_all, vc_all,
            ks, vs, w_a, w_b, w_kv, w_c)
        last = l == depth - 1
        res = _ffn((xp, xs), norm_g, w_ffn_gu, w_ffn_down, l, 1, n_prompt_rows, last)
        if last:
            xp, xs = res[0], res[1][:n_seq * t_new]
        else:
            x_in = (res,)

        outs["sgu"].append(vsgu)
        outs["kp"].append(kp)
        outs["vp"].append(vp)
        outs["cp"].append(cp)
        outs["cs"].append(z_s.reshape(n_seq, t_new, w_c)[:, t_new - (CONV_W - 1):])
    sgu = jnp.transpose(
        jnp.stack(outs["sgu"]).reshape(depth, t_new, n_heads_a, HEAD_DIM, n_seq), (0, 4, 1, 2, 3))
    return (xp.reshape(batch, seq, d), xs.reshape(n_seq, t_new, d), sgu,
            from_dp(jnp.stack(outs["kp"])), from_dp(jnp.stack(outs["vp"])),
            from_dp(ks), from_dp(vs),
            jnp.stack(outs["cp"]), jnp.stack(outs["cs"]))
```

```python
import functools

import numpy as np
import jax
import jax.numpy as jnp
from jax import lax
from jax.experimental import pallas as pl
from jax.experimental.pallas import tpu as pltpu

F32 = jnp.float32
BF16 = jnp.bfloat16

HEAD_DIM = 64
KV_HEADS = 2
WINDOW = 128
CHUNK = 128
CONV_W = 3
EPS = 1e-6
NEG = -1e30
SCALE = HEAD_DIM ** -0.5
LOG2E = float(np.log2(np.e))
LANES = 128

FFN_TM = 512
FFN_TF = 256
MIX_T = 512
MIB = 1024 * 1024


def _rms(x, g):
    return x * lax.rsqrt(jnp.mean(x * x, axis=-1, keepdims=True) + EPS) * g


def _gelu(x):
    return 0.5 * x * (1.0 + lax.erf(x * np.float32(np.sqrt(0.5))))


def _head_layer_norm(x, g):
    lane = lax.broadcasted_iota(jnp.int32, (1, x.shape[1]), 1)
    out = jnp.zeros_like(x)
    for hh in range(x.shape[1] // HEAD_DIM):
        m = (lane >= hh * HEAD_DIM) & (lane < (hh + 1) * HEAD_DIM)
        mu = jnp.sum(jnp.where(m, x, 0.0), axis=-1, keepdims=True) / HEAD_DIM
        d = jnp.where(m, x - mu, 0.0)
        var = jnp.sum(d * d, axis=-1, keepdims=True) / HEAD_DIM
        out = out + d * lax.rsqrt(var + EPS)
    return out * g


def _dot(a, b):
    return jnp.dot(a, b, preferred_element_type=F32)


def _dot_nt(a, b):
    return lax.dot_general(a, b, (((1,), (1,)), ((), ())), preferred_element_type=F32)


def _alibi_slopes(n_heads):
    return [float(2.0 ** (-8.0 * h / n_heads)) for h in range(1, n_heads + 1)]


def _ffn_kernel(*refs, layer, half, n_prompt_tiles, split_in, split_out, d_ff, tf, n_stream):
    refs = list(refs)
    xp_ref = refs.pop(0)
    xs_ref = refs.pop(0) if split_in else None
    norm_ref, wgu_hbm, wdn_hbm = refs[:3]
    refs = refs[3:]
    op_ref = refs.pop(0)
    os_ref = refs.pop(0) if split_out else None
    wgu16_ref, wdn16_ref, stage_gu_ref, stage_dn_ref, sem_ref, act_ref = refs
    i = pl.program_id(0)
    cw = stage_gu_ref.shape[2]
    rw = stage_dn_ref.shape[1]

    assert cw == 2 * tf and rw == tf and n_stream == d_ff // tf

    def gu_copy(c, slot, part):
        return pltpu.make_async_copy(
            wgu_hbm.at[layer, half, :, pl.ds(part * d_ff + c * tf, tf)],
            stage_gu_ref.at[slot, :, pl.ds(part * tf, tf)], sem_ref.at[part, slot])

    def dn_copy(c, slot):
        return pltpu.make_async_copy(wdn_hbm.at[layer, half, pl.ds(c * rw, rw), :],
                                     stage_dn_ref.at[slot], sem_ref.at[2, slot])

    def start_chunk(c):
        gu_copy(c, c % 2, 0).start()
        gu_copy(c, c % 2, 1).start()
        dn_copy(c, c % 2).start()

    if split_in and split_out:
        x = jnp.where(i == 0, xs_ref[...], xp_ref[...])
    elif split_in:
        x = jnp.where(i < n_prompt_tiles, xp_ref[...], xs_ref[...])
    else:
        x = xp_ref[...]
    g_pre = norm_ref[layer, 4 * half:4 * half + 1, :]
    g_post = norm_ref[layer, 4 * half + 1:4 * half + 2, :]
    n_chunks = d_ff // tf

    def hidden_chunks(h, rows, first, count):
        for c in range(first, first + count):
            gate = _dot(h, wgu16_ref[:, c * tf:(c + 1) * tf])
            up = _dot(h, wgu16_ref[:, d_ff + c * tf:d_ff + (c + 1) * tf])
            act_ref[rows, c * tf:(c + 1) * tf] = (jax.nn.silu(gate) * up).astype(BF16)

    def store(rows, val):
        op_ref[rows, :] = val
        if split_out:
            os_ref[rows, :] = val

    hr = x.shape[0] // 2
    ra, rb = slice(0, hr), slice(hr, 2 * hr)
    xa, xb = x[ra], x[rb]

    @pl.when(i == 0)
    def _():
        start_chunk(0)
        ha = _rms(xa, g_pre).astype(BF16)
        hb = _rms(xb, g_pre).astype(BF16)
        for c in range(n_stream):
            slot = c % 2
            if c + 1 < n_stream:
                start_chunk(c + 1)
            gu_copy(c, slot, 0).wait()
            gu_copy(c, slot, 1).wait()
            wgu16_ref[:, c * tf:(c + 1) * tf] = stage_gu_ref[slot, :, :tf].astype(BF16)
            wgu16_ref[:, d_ff + c * tf:d_ff + (c + 1) * tf] = (
                stage_gu_ref[slot, :, tf:].astype(BF16))
            hidden_chunks(ha, ra, c, 1)
            hidden_chunks(hb, rb, c, 1)
            dn_copy(c, slot).wait()
            wdn16_ref[c * rw:(c + 1) * rw, :] = stage_dn_ref[slot].astype(BF16)
        ya = _dot(act_ref[ra, :], wdn16_ref[...])
        store(ra, xa + 0.5 * _rms(ya, g_post))
        yb = _dot(act_ref[rb, :], wdn16_ref[...])
        store(rb, xb + 0.5 * _rms(yb, g_post))

    @pl.when(i > 0)
    def _():
        ha = _rms(xa, g_pre).astype(BF16)
        hidden_chunks(ha, ra, 0, 2)
        hb = _rms(xb, g_pre).astype(BF16)
        hidden_chunks(ha, ra, 2, n_chunks - 2)
        ya = _dot(act_ref[ra, :], wdn16_ref[...])
        hidden_chunks(hb, rb, 0, 3)
        store(ra, xa + 0.5 * _rms(ya, g_post))
        hidden_chunks(hb, rb, 3, n_chunks - 3)
        yb = _dot(act_ref[rb, :], wdn16_ref[...])
        store(rb, xb + 0.5 * _rms(yb, g_post))


def _ffn(xs_in, norm_g, w_gu, w_down, layer, half, n_prompt_rows, split_out):
    split_in = len(xs_in) == 2
    d = xs_in[0].shape[1]
    d_ff = w_down.shape[2]
    tm = FFN_TM
    n_sample_rows = xs_in[1].shape[0] if split_in else xs_in[0].shape[0] - n_prompt_rows
    assert n_prompt_rows % tm == 0 and n_sample_rows == tm and d_ff % FFN_TF == 0
    npt = n_prompt_rows // tm
    n_stream = d_ff // FFN_TF
    cw = 2 * d_ff // n_stream
    rw = d_ff // n_stream
    assert cw % LANES == 0 and rw % 8 == 0
    assert split_in or not split_out
    if split_out:
        prompt_tile = lambda i: (jnp.maximum(i - 1, 0), 0)
    else:
        prompt_tile = lambda i: (jnp.minimum(i, npt - 1), 0)
    const = lambda i: (0, 0)
    in_specs = []
    if split_in:
        in_specs += [pl.BlockSpec((tm, d), prompt_tile),
                     pl.BlockSpec((tm, d), const, pipeline_mode=pl.Buffered(1))]
    else:
        in_specs += [pl.BlockSpec((tm, d), lambda i: (i, 0))]
    in_specs += [pl.BlockSpec(norm_g.shape, lambda i: (0, 0, 0)),
                 pl.BlockSpec(memory_space=pl.ANY),
                 pl.BlockSpec(memory_space=pl.ANY)]
    if split_out:
        out_specs = [pl.BlockSpec((tm, d), prompt_tile),
                     pl.BlockSpec((tm, d), lambda i: (jnp.minimum(i, 1), 0))]
        out_shape = [jax.ShapeDtypeStruct((n_prompt_rows, d), F32),
                     jax.ShapeDtypeStruct((2 * n_sample_rows, d), F32)]
    else:
        out_specs = pl.BlockSpec((tm, d), lambda i: (i, 0))
        out_shape = jax.ShapeDtypeStruct((n_prompt_rows + n_sample_rows, d), F32)
    weights16 = 3 * d * d_ff * 2
    staging = 2 * (d * cw + rw * d) * 4
    vmem = weights16 + staging + 7 * tm * d * 4 + tm * d_ff * 2 + 10 * MIB
    return pl.pallas_call(
        functools.partial(_ffn_kernel, layer=layer, half=half, n_prompt_tiles=npt,
                          split_in=split_in, split_out=split_out, d_ff=d_ff, tf=FFN_TF,
                          n_stream=n_stream),
        grid=(npt + 1,),
        in_specs=in_specs,
        out_specs=out_specs,
        out_shape=out_shape,
        scratch_shapes=[
            pltpu.VMEM((d, 2 * d_ff), BF16),
            pltpu.VMEM((d_ff, d), BF16),
            pltpu.VMEM((2, d, cw), F32),
            pltpu.VMEM((2, rw, d), F32),
            pltpu.SemaphoreType.DMA((3, 2)),
            pltpu.VMEM((tm, d_ff), BF16),
        ],
        compiler_params=pltpu.CompilerParams(
            dimension_semantics=("arbitrary",), vmem_limit_bytes=vmem),
        name="ffn_half_step",
    )(*xs_in, norm_g, w_gu, w_down)


def _mixp_kernel(*refs, **statics):
    parity = lax.rem(pl.program_id(0), 2)

    @pl.when(parity == 0)
    def _():
        _mixp_body(*refs, parity=0, **statics)

    @pl.when(parity == 1)
    def _():
        _mixp_body(*refs, parity=1, **statics)


def _mixp_body(sinks_all_ref, xnext_ref, x_ref, norm_ref, win_ref, wout_ref, gout_all_ref, wsgu_ref,
               bsgu_ref, gsgu_all_ref, wconv_all_ref,
               xo_ref, ko_ref, vo_ref, co_ref,
               proj_even_ref, proj_odd_ref, kb_ref, krb_ref, vb_ref, vrb_ref, zbuf_ref, ycat_ref,
               abias_ref,
               *, parity, layer, tiles_per_seq, slopes, w_a, w_b, w_kv, w_c):
    proj_next_ref, proj_ref = ((proj_even_ref, proj_odd_ref) if parity == 0
                               else (proj_odd_ref, proj_even_ref))
    g2_ref = norm_ref.at[layer, pl.ds(2, 1)]
    g3_ref = norm_ref.at[layer, pl.ds(3, 1)]
    gout_ref = gout_all_ref.at[pl.ds(layer, 1)]
    gsgu_ref = gsgu_all_ref.at[pl.ds(layer, 1)]
    wconv_ref = wconv_all_ref.at[layer]
    step = pl.program_id(0)
    tile = jnp.maximum(step - 1, 0)
    j = lax.rem(tile, tiles_per_seq)
    t_rows = x_ref.shape[0]
    nblk = t_rows // WINDOW
    o_q = 2 * w_a
    o_k = o_q + w_b
    o_v = o_k + w_kv
    o_gb = o_v + w_kv
    o_gc = o_gb + w_c
    o_hc = o_gc + w_c

    @pl.when(step == 0)
    def _():
        proj_ref[...] = jnp.zeros(proj_ref.shape, F32)
        qi = lax.broadcasted_iota(jnp.int32, (WINDOW, 2 * WINDOW), 0)
        kk = lax.broadcasted_iota(jnp.int32, (WINDOW, 2 * WINDOW), 1)
        dist_i = WINDOW + qi - kk
        band = (dist_i >= 0) & (dist_i < WINDOW)
        dist = dist_i.astype(F32)
        for hd in range(len(slopes)):
            abias_ref[hd] = jnp.where(band, (slopes[hd] * LOG2E) * dist, -NEG)

    @pl.when(j == 0)
    def _():
        kb_ref[0:WINDOW, :] = jnp.zeros((WINDOW, w_kv), BF16)
        krb_ref[0:WINDOW, :] = jnp.zeros((WINDOW, w_kv), BF16)
        vb_ref[0:WINDOW, :] = jnp.zeros((WINDOW, w_kv), BF16)
        vrb_ref[0:WINDOW, :] = jnp.zeros((WINDOW, w_kv), BF16)
        zbuf_ref[0:8, :] = jnp.zeros((8, w_c), F32)

    hn = _rms(xnext_ref[...], g2_ref[...]).astype(BF16)
    n_pchunks = 8
    pcw = proj_next_ref.shape[1] // n_pchunks

    def project_chunks(first, count):
        for c in range(first, first + count):
            proj_next_ref[:, c * pcw:(c + 1) * pcw] = _dot(hn, win_ref[:, c * pcw:(c + 1) * pcw])

    x = x_ref[...]
    n_early = n_pchunks - nblk
    project_chunks(0, n_early)

    u = _gelu(proj_ref[:, 0:w_a])
    v = _head_layer_norm(_gelu(proj_ref[:, w_a:2 * w_a]), gsgu_ref[...])
    lane_a = lax.broadcasted_iota(jnp.int32, (1, w_a), 1)
    ri = lax.broadcasted_iota(jnp.int32, (CHUNK, CHUNK), 0)
    ci = lax.broadcasted_iota(jnp.int32, (CHUNK, CHUNK), 1)
    n_heads_a = w_a // HEAD_DIM
    w_tril = [jnp.where(ri >= ci, wsgu_ref[hh], 0.0).astype(BF16) for hh in range(n_heads_a)]
    bias = bsgu_ref[...]
    ya_parts = []
    for n in range(nblk):
        vblk = v[n * CHUNK:(n + 1) * CHUNK]
        mix = bias
        for hh in range(n_heads_a):
            mh = (lane_a >= hh * HEAD_DIM) & (lane_a < (hh + 1) * HEAD_DIM)
            mix = mix + _dot(w_tril[hh], jnp.where(mh, vblk, 0.0).astype(BF16))
        ya_parts.append(u[n * CHUNK:(n + 1) * CHUNK] * mix)
    ya = jnp.concatenate(ya_parts, axis=0)
    ycat_ref[:, 0:w_a] = _rms(ya, gout_ref[:, 0:w_a]).astype(BF16)
    lane = lax.broadcasted_iota(jnp.int32, (1, LANES), 1)
    lo = lane < HEAD_DIM
    k = proj_ref[:, o_k:o_k + w_kv]
    vv = proj_ref[:, o_v:o_v + w_kv]
    kb_ref[WINDOW:, :] = k.astype(BF16)
    krb_ref[WINDOW:, :] = pltpu.roll(k, HEAD_DIM, 1).astype(BF16)
    vb_ref[WINDOW:, :] = vv.astype(BF16)
    vrb_ref[WINDOW:, :] = pltpu.roll(vv, HEAD_DIM, 1).astype(BF16)

    kj = lax.broadcasted_iota(jnp.int32, (WINDOW, 2 * WINDOW), 1)
    has_prev = jnp.full((WINDOW, 2 * WINDOW), j, jnp.int32) > 0
    first_ok = (kj >= WINDOW) | has_prev

    n_q_cols = w_b // LANES
    half = n_q_cols // 2
    yb_parts = []

    def scores(n):
        rows = slice(n * WINDOW, (n + 1) * WINDOW)
        qcols = [proj_ref[rows, o_q + c * LANES:o_q + (c + 1) * LANES] * (SCALE * LOG2E)
                 for c in range(n_q_cols)]
        keep_a = [lo if c < half else ~lo for c in range(n_q_cols)]
        lhs_a = jnp.concatenate([jnp.where(keep_a[c], qcols[c], 0.0) for c in range(n_q_cols)],
                                axis=0).astype(BF16)
        lhs_b = jnp.concatenate([jnp.where(keep_a[c], 0.0, qcols[c]) for c in range(n_q_cols)],
                                axis=0).astype(BF16)
        band_rows = slice(n * WINDOW, (n + 2) * WINDOW)
        return _dot_nt(lhs_a, kb_ref[band_rows, :]), _dot_nt(lhs_b, krb_ref[band_rows, :])

    s_next = scores(0)
    for n in range(nblk):
        band_rows = slice(n * WINDOW, (n + 2) * WINDOW)
        s_a, s_b = s_next
        project_chunks(n_early + n, 1)
        if n + 1 < nblk:
            s_next = scores(n + 1)
        p_a, p_b, inv_a, inv_b = [], [], [], []
        for c in range(n_q_cols):
            head_a = 2 * c if c < half else 2 * c + 1
            head_b = 2 * c + 1 if c < half else 2 * c
            for s_all, head, acc, inv in ((s_a, head_a, p_a, inv_a), (s_b, head_b, p_b, inv_b)):
                s = s_all[c * WINDOW:(c + 1) * WINDOW] - abias_ref[head]
                if n == 0:
                    s = jnp.where(first_ok, s, NEG)
                sink = sinks_all_ref[layer, head] * LOG2E
                m = jnp.maximum(jnp.max(s, axis=-1, keepdims=True), sink)
                p = jnp.exp2(s - m)
                den = jnp.sum(p, axis=-1, keepdims=True) + jnp.exp2(sink - m)
                acc.append(p.astype(BF16))
                inv.append(1.0 / den)
        o_a = _dot(jnp.concatenate(p_a, axis=0), vb_ref[band_rows, :])
        o_b = _dot(jnp.concatenate(p_b, axis=0), vrb_ref[band_rows, :])
        cols = []
        for c in range(n_q_cols):
            ra = o_a[c * WINDOW:(c + 1) * WINDOW] * inv_a[c]
            rb = o_b[c * WINDOW:(c + 1) * WINDOW] * inv_b[c]
            cols.append(jnp.where(lo, ra, rb) if c < half else jnp.where(lo, rb, ra))
        yb_parts.append(jnp.concatenate(cols, axis=1))
    yb = jnp.concatenate(yb_parts, axis=0)
    ycat_ref[:, w_a:w_a + w_b] = _rms(yb, gout_ref[:, w_a:w_a + w_b]).astype(BF16)
    kb_ref[0:WINDOW, :] = kb_ref[t_rows:t_rows + WINDOW, :]
    krb_ref[0:WINDOW, :] = krb_ref[t_rows:t_rows + WINDOW, :]
    vb_ref[0:WINDOW, :] = vb_ref[t_rows:t_rows + WINDOW, :]
    vrb_ref[0:WINDOW, :] = vrb_ref[t_rows:t_rows + WINDOW, :]

    z = proj_ref[:, o_gc:o_gc + w_c] * proj_ref[:, o_hc:o_hc + w_c]
    zbuf_ref[8:8 + t_rows, :] = z
    conv = zbuf_ref[8 - (CONV_W - 1):8 - (CONV_W - 1) + t_rows, :] * wconv_ref[0:1, :]
    for jj in range(1, CONV_W):
        off = 8 - (CONV_W - 1) + jj
        conv = conv + zbuf_ref[off:off + t_rows, :] * wconv_ref[jj:jj + 1, :]
    yc = proj_ref[:, o_gb:o_gb + w_c] * conv
    ycat_ref[:, w_a + w_b:] = _rms(yc, gout_ref[:, w_a + w_b:]).astype(BF16)
    z_tail = zbuf_ref[8 + t_rows - (CONV_W - 1):8 + t_rows, :]
    zbuf_ref[8 - (CONV_W - 1):8, :] = z_tail

    y = _dot(ycat_ref[...], wout_ref[...])
    xo_ref[...] = x + _rms(y, g3_ref[...])

    @pl.when((j == tiles_per_seq - 1) & (step > 0))
    def _():
        ko_ref[0] = proj_ref[t_rows - WINDOW:, o_k:o_k + w_kv].T
        vo_ref[0] = proj_ref[t_rows - WINDOW:, o_v:o_v + w_kv].T
        co_ref[0] = z_tail


def _mixer_prompt(x_all, layer, batch, seq, norm_g, win, wout, gout_all, wsgu_all, bsgu_full_all,
                  gsgu_all, sinks_all, wconv_all, w_a, w_b, w_kv, w_c):
    d = x_all.shape[1]
    m = batch * seq
    t = MIX_T
    assert seq % t == 0 and t % WINDOW == 0
    nt = seq // t
    in_w = win.shape[2]
    n_heads_b = w_b // HEAD_DIM
    n_tiles = batch * nt
    const2 = lambda s, *_: (0, 0)
    const3 = lambda s, *_: (0, 0, 0)
    layer3 = lambda s, *_: (layer, 0, 0)
    layer4 = lambda s, *_: (layer, 0, 0, 0)
    row_next = lambda s, *_: (jnp.minimum(s, n_tiles - 1), 0)
    row = lambda s, *_: (jnp.maximum(s - 1, 0), 0)
    per_b = lambda s, *_: (jnp.maximum(s - 1, 0) // nt, 0, 0)
    kernel = functools.partial(_mixp_kernel, layer=layer, tiles_per_seq=nt,
                               slopes=_alibi_slopes(n_heads_b),
                               w_a=w_a, w_b=w_b, w_kv=w_kv, w_c=w_c)
    grid_spec = pltpu.PrefetchScalarGridSpec(
        num_scalar_prefetch=1,
        grid=(n_tiles + 1,),
        in_specs=[
            pl.BlockSpec((t, d), row_next),
            pl.BlockSpec((t, d), row),
            pl.BlockSpec(norm_g.shape, const3),
            pl.BlockSpec((None,) + win.shape[1:], layer3, pipeline_mode=pl.Buffered(1)),
            pl.BlockSpec((None,) + wout.shape[1:], layer3, pipeline_mode=pl.Buffered(1)),
            pl.BlockSpec(gout_all.shape, const2),
            pl.BlockSpec((None,) + wsgu_all.shape[1:], layer4),
            pl.BlockSpec((None,) + bsgu_full_all.shape[1:], layer3),
            pl.BlockSpec(gsgu_all.shape, const2),
            pl.BlockSpec(wconv_all.shape, const3),
        ],
        out_specs=[
            pl.BlockSpec((t, d), row),
            pl.BlockSpec((1, w_kv, WINDOW), per_b),
            pl.BlockSpec((1, w_kv, WINDOW), per_b),
            pl.BlockSpec((1, CONV_W - 1, w_c), per_b),
        ],
        scratch_shapes=[
            pltpu.VMEM((t, in_w), F32),
            pltpu.VMEM((t, in_w), F32),
            pltpu.VMEM((t + WINDOW, w_kv), BF16),
            pltpu.VMEM((t + WINDOW, w_kv), BF16),
            pltpu.VMEM((t + WINDOW, w_kv), BF16),
            pltpu.VMEM((t + WINDOW, w_kv), BF16),
            pltpu.VMEM((t + 8, w_c), F32),
            pltpu.VMEM((t, gout_all.shape[1]), BF16),
            pltpu.VMEM((n_heads_b, WINDOW, 2 * WINDOW), F32),
        ],
    )
    depth = win.shape[0]
    vmem = (win.size + wout.size) // depth * 2 + 6 * t * d * 4 + 2 * t * in_w * 4 + 24 * MIB
    return pl.pallas_call(
        kernel,
        grid_spec=grid_spec,
        out_shape=[
            jax.ShapeDtypeStruct((m, d), F32),
            jax.ShapeDtypeStruct((batch, w_kv, WINDOW), F32),
            jax.ShapeDtypeStruct((batch, w_kv, WINDOW), F32),
            jax.ShapeDtypeStruct((batch, CONV_W - 1, w_c), F32),
        ],
        compiler_params=pltpu.CompilerParams(
            dimension_semantics=("arbitrary",), vmem_limit_bytes=vmem),
        name="mixer_prompt",
    )(sinks_all, x_all, x_all, norm_g, win, wout, gout_all, wsgu_all, bsgu_full_all, gsgu_all,
      wconv_all)


def _mixs_kernel(x_ref, norm_ref, win_ref, wout_ref, gout_all_ref, coef_ref, bias_ref, gsgu_all_ref,
                 wconv_all_ref, cexp1_ref, cexp2_ref, sinkcol_ref, kc_ref, vc_ref,
                 ks_stack_ref, vs_stack_ref,
                 xo_ref, vsgu_ref, z_ref, ko_ref, vo_ref,
                 proj_ref, qprep_ref, ybuf_ref, ycat_ref, knt_ref, vnt_ref, vtmp_ref,
                 *, layer, slopes, t_new, w_a, w_b, w_kv, w_c):
    del ks_stack_ref, vs_stack_ref
    g2_ref = norm_ref.at[layer, pl.ds(2, 1)]
    g3_ref = norm_ref.at[layer, pl.ds(3, 1)]
    gout_ref = gout_all_ref.at[pl.ds(layer, 1)]
    gsgu_ref = gsgu_all_ref.at[pl.ds(layer, 1)]
    wconv_ref = wconv_all_ref.at[layer]
    g = pl.program_id(0)
    rows_all = x_ref.shape[0]
    n_seq = rows_all // t_new
    gsz = kc_ref.shape[0]
    o_q = 2 * w_a
    o_k = o_q + w_b
    o_v = o_k + w_kv
    o_gb = o_v + w_kv
    o_gc = o_gb + w_c
    o_hc = o_gc + w_c
    n_heads = w_b // HEAD_DIM
    group = n_heads // KV_HEADS
    lane = lax.broadcasted_iota(jnp.int32, (1, LANES), 1)
    lo = lane < HEAD_DIM

    @pl.when(g == 0)
    def _():
        x = x_ref[...]
        h = _rms(x, g2_ref[...]).astype(BF16)
        proj_ref[...] = _dot(h, win_ref[...])
        tpos = lax.broadcasted_iota(jnp.int32, (rows_all, 1), 0) % t_new

        u = _gelu(proj_ref[:, 0:w_a])
        v = _head_layer_norm(_gelu(proj_ref[:, w_a:2 * w_a]), gsgu_ref[...])
        for c in range(w_a // LANES):
            vtmp_ref[c] = v[:, c * LANES:(c + 1) * LANES]
        for tt in range(t_new):
            for c in range(w_a // LANES):
                vsgu_ref[tt, c * LANES:(c + 1) * LANES, :] = (
                    vtmp_ref[c, pl.ds(tt, n_seq, stride=t_new), :].T)
        mix = bias_ref[...] + coef_ref[0] * v
        for dlt in range(1, t_new):
            mix = mix + coef_ref[dlt] * pltpu.roll(v, dlt, 0)
        ycat_ref[:, 0:w_a] = _rms(u * mix, gout_ref[:, 0:w_a]).astype(BF16)

        z = proj_ref[:, o_gc:o_gc + w_c] * proj_ref[:, o_hc:o_hc + w_c]
        z_ref[...] = z
        s2 = jnp.where(tpos >= 2, pltpu.roll(z, 2, 0), 0.0) + cexp2_ref[...]
        s1 = jnp.where(tpos >= 1, pltpu.roll(z, 1, 0), 0.0) + cexp1_ref[...]
        conv = s2 * wconv_ref[0:1, :] + s1 * wconv_ref[1:2, :] + z * wconv_ref[2:3, :]
        yc = proj_ref[:, o_gb:o_gb + w_c] * conv
        ycat_ref[:, w_a + w_b:] = _rms(yc, gout_ref[:, w_a + w_b:]).astype(BF16)

        for hd in range(n_heads):
            c, hf, kvh = hd // 2, hd % 2, hd // group
            piece = proj_ref[:, o_q + c * LANES:o_q + (c + 1) * LANES]
            if hf != kvh:
                piece = pltpu.roll(piece, HEAD_DIM, 1)
            qprep_ref[hd] = jnp.where(lo if kvh == 0 else ~lo, piece * (SCALE * LOG2E), 0.0)

        for c in range(rows_all // LANES):
            knt_ref[c] = proj_ref[c * LANES:(c + 1) * LANES, o_k:o_k + w_kv].T
            vnt_ref[c] = proj_ref[c * LANES:(c + 1) * LANES, o_v:o_v + w_kv].T

    pair_rows = 2 * t_new
    n_pairs = gsz // 2
    lrows = n_heads * pair_rows
    ri = lax.broadcasted_iota(jnp.int32, (lrows, 1), 0)
    r_in = ri % pair_rows
    seq_q = r_in // t_new
    t_q = r_in % t_new
    cj = lax.broadcasted_iota(jnp.int32, (1, 2 * WINDOW), 1)
    valid_c = (cj // WINDOW == seq_q) & (cj % WINDOW >= t_q + 1)
    dist_c = (WINDOW + t_q - cj % WINDOW).astype(F32)
    nj = lax.broadcasted_iota(jnp.int32, (1, LANES), 1)
    valid_n = (nj < pair_rows) & (nj // t_new == seq_q) & (nj % t_new <= t_q)
    dist_n = (t_q - nj % t_new).astype(F32)
    slope_col = jnp.zeros((lrows, 1), F32)
    for hd in range(n_heads):
        slope_col = jnp.where(ri // pair_rows == hd, slopes[hd], slope_col)
    sink = sinkcol_ref[...] * LOG2E
    bias_c = jnp.where(valid_c, (slope_col * LOG2E) * dist_c, -NEG)
    bias_n = jnp.where(valid_n, (slope_col * LOG2E) * dist_n, -NEG)
    zpad = jnp.zeros((LANES - pair_rows, w_kv), F32)

    keep = lane < WINDOW - t_new
    kn_t = knt_ref[g]
    vn_t = vnt_ref[g]

    def slide(src_ref, dst_ref, new_t, i):
        shift = (WINDOW - t_new - t_new * i) % LANES
        new = pltpu.roll(new_t, shift, 1) if shift else new_t
        dst_ref[i] = jnp.where(keep, pltpu.roll(src_ref[i], WINDOW - t_new, 1), new)

    r0 = pl.multiple_of(g * (n_pairs * pair_rows), n_pairs * pair_rows)
    sc_parts, sn_parts = [], []
    for p in range(n_pairs):
        rows = pl.ds(r0 + p * pair_rows, pair_rows)
        lhs = jnp.concatenate([qprep_ref[hd, rows, :] for hd in range(n_heads)], axis=0).astype(BF16)
        kn = proj_ref[rows, o_k:o_k + w_kv]
        kc = jnp.concatenate([kc_ref[2 * p], kc_ref[2 * p + 1]], axis=1)
        sc_parts.append(_dot(lhs, kc.astype(BF16)))
        sn_parts.append(_dot_nt(lhs, jnp.concatenate([kn, zpad], axis=0).astype(BF16)))
        slide(kc_ref, ko_ref, kn_t, 2 * p)
        slide(kc_ref, ko_ref, kn_t, 2 * p + 1)
    s_c = jnp.stack(sc_parts) - bias_c[None]
    s_n = jnp.stack(sn_parts) - bias_n[None]
    m = jnp.maximum(jnp.maximum(jnp.max(s_c, axis=-1, keepdims=True),
                                jnp.max(s_n, axis=-1, keepdims=True)), sink[None])
    p_c = jnp.exp2(s_c - m)
    p_n = jnp.exp2(s_n - m)
    den = (jnp.sum(p_c, axis=-1, keepdims=True) + jnp.sum(p_n, axis=-1, keepdims=True)
           + jnp.exp2(sink[None] - m))
    inv = 1.0 / den
    p_c = p_c.astype(BF16)
    p_n = p_n.astype(BF16)
    for p in range(n_pairs):
        rows = pl.ds(r0 + p * pair_rows, pair_rows)
        vn = proj_ref[rows, o_v:o_v + w_kv]
        vc = jnp.concatenate([vc_ref[2 * p], vc_ref[2 * p + 1]], axis=1)
        slide(vc_ref, vo_ref, vn_t, 2 * p)
        slide(vc_ref, vo_ref, vn_t, 2 * p + 1)
        o = (_dot_nt(p_c[p], vc.astype(BF16))
             + _dot(p_n[p], jnp.concatenate([vn, zpad], axis=0).astype(BF16))) * inv[p]
        cols = []
        for c in range(n_heads // 2):
            pieces = []
            for hf in range(2):
                hd = 2 * c + hf
                piece = o[hd * pair_rows:(hd + 1) * pair_rows]
                if hd // group != hf:
                    piece = pltpu.roll(piece, HEAD_DIM, 1)
                pieces.append(piece)
            cols.append(jnp.where(lo, pieces[0], pieces[1]))
        ybuf_ref[rows, :] = jnp.concatenate(cols, axis=1)

    @pl.when(g == pl.num_programs(0) - 1)
    def _():
        ycat_ref[:, w_a:w_a + w_b] = _rms(ybuf_ref[...], gout_ref[:, w_a:w_a + w_b]).astype(BF16)
        y = _dot(ycat_ref[...], wout_ref[...])
        xo_ref[...] = x_ref[...] + _rms(y, g3_ref[...])


def _mixer_sample(x_all, layer, n_prompt_rows, n_seq, t_new, norm_g, win, wout, gout_all, coef_all,
                  bias_all, gsgu_all, wconv_all, cexp1_all, cexp2_all, sink_col_all, kc_all, vc_all,
                  ks_stack, vs_stack, w_a, w_b, w_kv, w_c):
    d = x_all.shape[1]
    m = n_seq * t_new
    assert n_prompt_rows % m == 0
    in_w = win.shape[2]
    n_heads = w_b // HEAD_DIM
    gsz = LANES // t_new
    assert n_seq % gsz == 0 and gsz % 2 == 0 and 2 * t_new == 8 and WINDOW == LANES
    ng = n_seq // gsz
    const2 = lambda g: (0, 0)
    const3 = lambda g: (0, 0, 0)
    layer3 = lambda g: (layer, 0, 0)
    layer4 = lambda g: (layer, 0, 0, 0)
    step_in = lambda g: (layer, g, 0, 0)
    kernel = functools.partial(_mixs_kernel, layer=layer, slopes=_alibi_slopes(n_heads),
                               t_new=t_new, w_a=w_a, w_b=w_b, w_kv=w_kv, w_c=w_c)
    coef_bytes = coef_all.size // coef_all.shape[0] * 4
    depth = win.shape[0]
    vmem = ((win.size + wout.size) // depth * 2 + 4 * m * d * 4 + m * in_w * 4
            + 8 * gsz * WINDOW * w_kv * 4 + coef_bytes * 2 + 24 * MIB)
    return pl.pallas_call(
        kernel,
        grid=(ng,),
        in_specs=[
            pl.BlockSpec((m, d), lambda g: (n_prompt_rows // m, 0)),
            pl.BlockSpec(norm_g.shape, const3),
            pl.BlockSpec((None,) + win.shape[1:], layer3, pipeline_mode=pl.Buffered(1)),
            pl.BlockSpec((None,) + wout.shape[1:], layer3, pipeline_mode=pl.Buffered(1)),
            pl.BlockSpec(gout_all.shape, const2),
            pl.BlockSpec((None,) + coef_all.shape[1:], layer4),
            pl.BlockSpec((None,) + bias_all.shape[1:], layer3),
            pl.BlockSpec(gsgu_all.shape, const2),
            pl.BlockSpec(wconv_all.shape, const3),
            pl.BlockSpec((None,) + cexp1_all.shape[1:], layer3),
            pl.BlockSpec((None,) + cexp2_all.shape[1:], layer3),
            pl.BlockSpec((None,) + sink_col_all.shape[1:], layer3),
            pl.BlockSpec((None, gsz, w_kv, WINDOW), step_in),
            pl.BlockSpec((None, gsz, w_kv, WINDOW), step_in),
            pl.BlockSpec(memory_space=pl.ANY),
            pl.BlockSpec(memory_space=pl.ANY),
        ],
        out_specs=[
            pl.BlockSpec((m, d), const2),
            pl.BlockSpec((t_new, w_a, n_seq), const3),
            pl.BlockSpec((m, w_c), const2),
            pl.BlockSpec((None, gsz, w_kv, WINDOW), step_in),
            pl.BlockSpec((None, gsz, w_kv, WINDOW), step_in),
        ],
        out_shape=[
            jax.ShapeDtypeStruct((m, d), F32),
            jax.ShapeDtypeStruct((t_new, w_a, n_seq), F32),
            jax.ShapeDtypeStruct((m, w_c), F32),
            jax.ShapeDtypeStruct(ks_stack.shape, F32),
            jax.ShapeDtypeStruct(vs_stack.shape, F32),
        ],
        input_output_aliases={14: 3, 15: 4},
        scratch_shapes=[
            pltpu.VMEM((m, in_w), F32),
            pltpu.VMEM((n_heads, m, LANES), F32),
            pltpu.VMEM((m, w_b), F32),
            pltpu.VMEM((m, gout_all.shape[1]), BF16),
            pltpu.VMEM((ng, w_kv, LANES), F32),
            pltpu.VMEM((ng, w_kv, LANES), F32),
            pltpu.VMEM((w_a // LANES, m, LANES), F32),
        ],
        compiler_params=pltpu.CompilerParams(
            dimension_semantics=("arbitrary",), vmem_limit_bytes=vmem),
        name="mixer_sample",
    )(x_all, norm_g, win, wout, gout_all, coef_all, bias_all, gsgu_all, wconv_all, cexp1_all,
      cexp2_all, sink_col_all, kc_all, vc_all, ks_stack, vs_stack)


def kernel(x_prompt, x_sample, cache_swa_k, cache_swa_v, cache_conv, norm_g, w_ffn_gu, w_ffn_down,
           w_mix_in, w_mix_out, g_mix_out, w_sgu, b_sgu, g_sgu, attn_sinks, w_conv):
    batch, seq, d = x_prompt.shape
    n_seq, t_new, _ = x_sample.shape
    depth = norm_g.shape[0]
    w_a = g_sgu.shape[1]
    w_c = w_conv.shape[2]
    w_kv = KV_HEADS * HEAD_DIM
    w_b = g_mix_out.shape[1] - w_a - w_c
    n_heads_a = w_a // HEAD_DIM
    assert t_new <= CHUNK and seq % CHUNK == 0

    xp = x_prompt.reshape(batch * seq, d)
    xs = x_sample.reshape(n_seq * t_new, d)
    to_dp = lambda c: jnp.transpose(c, (0, 1, 3, 4, 2)).reshape(depth, c.shape[1], w_kv, c.shape[2])
    from_dp = lambda c: jnp.transpose(
        c.reshape(depth, c.shape[1], KV_HEADS, HEAD_DIM, c.shape[3]), (0, 1, 4, 2, 3))
    kc_all = to_dp(cache_swa_k)
    vc_all = to_dp(cache_swa_v)

    win_all = w_mix_in.astype(BF16)
    wout_all = w_mix_out.astype(BF16)
    bsgu_full_all = jnp.repeat(jnp.swapaxes(b_sgu, 1, 2), HEAD_DIM, axis=2)
    w4 = jnp.tril(w_sgu[:, :, :t_new, :t_new])
    tt = np.arange(t_new)
    coef_all = jnp.stack([
        jnp.where((tt >= dlt)[None, None, :], w4[:, :, tt, np.maximum(tt - dlt, 0)], 0.0)
        for dlt in range(t_new)], axis=1)
    coef_all = jnp.repeat(coef_all.transpose(0, 1, 3, 2), HEAD_DIM, axis=3)
    coef_all = jnp.tile(coef_all, (1, 1, n_seq, 1))
    bias_all = jnp.tile(jnp.repeat(jnp.swapaxes(b_sgu[:, :, :t_new], 1, 2), HEAD_DIM, axis=2),
                        (1, n_seq, 1))
    pad_t = lambda c: jnp.pad(c, ((0, 0), (0, 0), (0, t_new - c.shape[2]), (0, 0))).reshape(
        depth, n_seq * t_new, w_c)
    cexp2_all = pad_t(cache_conv)
    cexp1_all = pad_t(cache_conv[:, :, 1:])
    sink_col_all = jnp.repeat(attn_sinks, 2 * t_new, axis=1)[:, :, None]

    n_prompt_rows = batch * seq
    outs = {k: [] for k in ("sgu", "kp", "vp", "cp", "cs")}
    ks = jnp.zeros(kc_all.shape, F32)
    vs = jnp.zeros(vc_all.shape, F32)
    x_in = (xp, xs)
    for l in range(depth):
        x_all = _ffn(x_in, norm_g, w_ffn_gu, w_ffn_down, l, 0, n_prompt_rows, False)
        xp, kp, vp, cp = _mixer_prompt(
            x_all, l, batch, seq, norm_g, win_all, wout_all, g_mix_out, w_sgu, bsgu_full_all,
            g_sgu, attn_sinks, w_conv, w_a, w_b, w_kv, w_c)
        xs, vsgu, z_s, ks, vs = _mixer_sample(
            x_all, l, n_prompt_rows, n_seq, t_new, norm_g, win_all, wout_all, g_mix_out,
            coef_all, bias_all, g_sgu, w_conv, cexp1_all, cexp2_all, sink_col_all, kc_all, vc_all,
            ks, vs, w_a, w_b, w_kv, w_c)
        last = l == depth - 1
        res = _ffn((xp, xs), norm_g, w_ffn_gu, w_ffn_down, l, 1, n_prompt_rows, last)
        if last:
            xp, xs = res[0], res[1][:n_seq * t_new]
        else:
            x_in = (res,)

        outs["sgu"].append(vsgu)
        outs["kp"].append(kp)
        outs["vp"].append(vp)
        outs["cp"].append(cp)
        outs["cs"].append(z_s.reshape(n_seq, t_new, w_c)[:, t_new - (CONV_W - 1):])
    sgu = jnp.transpose(
        jnp.stack(outs["sgu"]).reshape(depth, t_new, n_heads_a, HEAD_DIM, n_seq), (0, 4, 1, 2, 3))
    return (xp.reshape(batch, seq, d), xs.reshape(n_seq, t_new, d), sgu,
            from_dp(jnp.stack(outs["kp"])), from_dp(jnp.stack(outs["vp"])),
            from_dp(ks), from_dp(vs),
            jnp.stack(outs["cp"]), jnp.stack(outs["cs"]))
```
